```python
import math
import jax
import jax.numpy as jnp
from jax import lax
import numpy as np

D_MODEL = 1024
BATCH = 8
SEQ = 4096
DEPTH = 1

HEAD_DIM = 64
DIL_GROUPS = ((128, 1), (512, 4), (2048, 16))
N_DIL_GROUPS = 3
DIL_HEADS = 4
MOBA_HEADS = 8
MOBA_BLOCK = 256
MOBA_TOPK = 3
MOBA_Q_CHUNK = 16
N_SELF_HEADS = N_DIL_GROUPS * DIL_HEADS + MOBA_HEADS
REL_BUCKETS = 32
REL_MAX_DIST = 2048
N_MEM = 256
CROSS_HEADS = 4
CROSS_HEAD_DIM = 128
PEER_HEADS = 8
PEER_N_KEYS = 128
PEER_N_EXPERTS = PEER_N_KEYS * PEER_N_KEYS
PEER_KEY_DIM = 256
PEER_TOPK = 16
PEER_TOKEN_CHUNK = 128
RMS_EPS = 1e-6
NEG_INF = -1e30
DIL_WIDTH = N_DIL_GROUPS * 3 * DIL_HEADS * HEAD_DIM
MOBA_WIDTH = 3 * MOBA_HEADS * HEAD_DIM
GATE_WIDTH = 2 * D_MODEL
IN_COLS = DIL_WIDTH + MOBA_WIDTH + GATE_WIDTH

kernel_name = 'hybrid_dilated_moba_peer_layer'


def rmsnorm(x, g):
    xf = x.astype(jnp.float32)
    y = xf * lax.rsqrt(jnp.mean(xf * xf, axis=-1, keepdims=True) + RMS_EPS)
    return (y * g.astype(jnp.float32)).astype(x.dtype)


def t5_bucket(n):
    max_exact = REL_BUCKETS // 2
    nf = jnp.maximum(n, 1).astype(jnp.float32)
    large = max_exact + (jnp.log(nf / max_exact) / math.log(REL_MAX_DIST / max_exact)
                         * (REL_BUCKETS - max_exact)).astype(jnp.int32)
    large = jnp.minimum(large, REL_BUCKETS - 1)
    return jnp.where(n < max_exact, n, large)


def dilated_group(q, k, v, window, dilation, bias_tab):
    B, S, H, hd = q.shape
    d = dilation
    blk = window // d
    L = S // d
    nb = -(-L // blk)
    Lp = nb * blk

    def to_sub(t):
        return t.reshape(B, L, d, H, hd).transpose(0, 2, 3, 1, 4)

    qs = jnp.pad(to_sub(q), ((0, 0), (0, 0), (0, 0), (0, Lp - L), (0, 0))).reshape(B, d, H, nb, blk, hd)

    def windows(t):
        tp = jnp.pad(to_sub(t), ((0, 0), (0, 0), (0, 0), (blk, Lp - L), (0, 0))).reshape(B, d, H, nb + 1, blk, hd)
        return jnp.concatenate([tp[:, :, :, :-1], tp[:, :, :, 1:]], axis=-2)

    kw = windows(k)
    vw = windows(v)
    logits = jnp.einsum('bdhnqc,bdhnkc->bdhnqk', qs, kw).astype(jnp.float32) * (hd ** -0.5)
    rel = (blk + jnp.arange(blk))[:, None] - jnp.arange(2 * blk)[None, :]
    in_window = (rel >= 0) & (rel <= blk)
    key_sub = (jnp.arange(nb)[:, None] - 1) * blk + jnp.arange(2 * blk)[None, :]
    mask = in_window[None] & (key_sub >= 0)[:, None, :]
    bias = bias_tab[t5_bucket(jnp.clip(rel, 0) * d)].transpose(2, 0, 1).astype(jnp.float32)
    logits = jnp.where(mask, logits + bias[:, None], NEG_INF)
    lse = jax.nn.logsumexp(logits, axis=-1)
    p = jnp.exp(logits - lse[..., None])
    out = jnp.einsum('bdhnqk,bdhnkc->bdhnqc', p, vw.astype(jnp.float32))
    out = out.reshape(B, d, H, Lp, hd)[:, :, :, :L].transpose(0, 3, 1, 2, 4).reshape(B, S, H, hd)
    lse = lse.reshape(B, d, H, Lp)[:, :, :, :L].transpose(0, 3, 1, 2).reshape(B, S, H)
    return out, lse


def moba(q, k, v, bias_tab):
    B, S, H, hd = q.shape
    blk = MOBA_BLOCK
    nblk = -(-S // blk)
    Sp = nblk * blk
    scale = hd ** -0.5

    def prep(t):
        return jnp.pad(t, ((0, 0), (0, Sp - S), (0, 0), (0, 0))).transpose(0, 2, 1, 3)

    qt, kt, vt = prep(q), prep(k), prep(v)
    qb = qt.reshape(B, H, nblk, blk, hd)
    kb = kt.reshape(B, H, nblk, blk, hd)
    vb = vt.reshape(B, H, nblk, blk, hd)
    table_t = bias_tab.T.astype(jnp.float32)

    rel = jnp.arange(blk)[:, None] - jnp.arange(blk)[None, :]
    own_bias = bias_tab[t5_bucket(jnp.clip(rel, 0))].transpose(2, 0, 1).astype(jnp.float32)
    lo = jnp.einsum('bhnqc,bhnkc->bhnqk', qb, kb).astype(jnp.float32) * scale
    lo = jnp.where(rel >= 0, lo + own_bias[:, None], NEG_INF)
    m_own = jnp.max(lo, axis=-1)
    p_own = jnp.exp(lo - m_own[..., None])
    s_own = jnp.sum(p_own, axis=-1).reshape(B, H, Sp)
    acc_own = jnp.einsum('bhnqk,bhnkc->bhnqc', p_own, vb.astype(jnp.float32)).reshape(B, H, Sp, hd)
    m_own = m_own.reshape(B, H, Sp)

    kmean = jnp.mean(kb.astype(jnp.float32), axis=-2)
    gs = jnp.einsum('bhsc,bhnc->bhsn', qt.astype(jnp.float32), kmean)
    q_blk = jnp.arange(Sp) // blk
    gs = jnp.where(jnp.arange(nblk)[None, :] < q_blk[:, None], gs, NEG_INF)
    topk = min(MOBA_TOPK, nblk)
    _, sel = lax.top_k(gs, topk)
    sel_ok = jnp.arange(topk)[None, :] < q_blk[:, None]

    qc_n = MOBA_Q_CHUNK
    nc = Sp // qc_n
    xs = (qt.reshape(B, H, nc, qc_n, hd).transpose(2, 0, 1, 3, 4),
          sel.reshape(B, H, nc, qc_n, topk).transpose(2, 0, 1, 3, 4),
          sel_ok.reshape(nc, qc_n, topk),
          jnp.arange(Sp).reshape(nc, qc_n))
    b_i = jnp.arange(B)[:, None, None, None]
    h_i = jnp.arange(H)[None, :, None, None]

    def past_blocks(args):
        qc, ic, okc, pos = args
        kg = kb[b_i, h_i, ic]
        vg = vb[b_i, h_i, ic]
        lg = jnp.einsum('bhqc,bhqnjc->bhqnj', qc, kg).astype(jnp.float32) * scale
        dist = pos[None, None, :, None, None] - (ic[..., None] * blk + jnp.arange(blk))
        lg = jnp.where(okc[None, None, :, :, None], lg + table_t[h_i[..., None], t5_bucket(dist)], NEG_INF)
        lg = lg.reshape(B, H, qc_n, topk * blk)
        m = jnp.max(lg, axis=-1)
        p = jnp.exp(lg - m[..., None])
        s = jnp.sum(p, axis=-1)
        acc = jnp.einsum('bhqk,bhqkc->bhqc', p, vg.reshape(B, H, qc_n, topk * blk, hd).astype(jnp.float32))
        return acc, s, m

    acc_p, s_p, m_p = lax.map(past_blocks, xs)
    acc_p = acc_p.transpose(1, 2, 0, 3, 4).reshape(B, H, Sp, hd)
    s_p = s_p.transpose(1, 2, 0, 3).reshape(B, H, Sp)
    m_p = m_p.transpose(1, 2, 0, 3).reshape(B, H, Sp)

    m = jnp.maximum(m_own, m_p)
    a_o = jnp.exp(m_own - m)
    a_p = jnp.exp(m_p - m)
    out = (acc_own * a_o[..., None] + acc_p * a_p[..., None]) / (s_own * a_o + s_p * a_p)[..., None]
    return out[:, :, :S].transpose(0, 2, 1, 3).reshape(B, S, H * hd)


def cross_attention(h, m, w_q, w_kv, w_o):
    B, S, _ = h.shape
    M = m.shape[1]
    q = (h @ w_q).reshape(B, S, CROSS_HEADS, CROSS_HEAD_DIM)
    kv = (m @ w_kv).reshape(B, M, 2, CROSS_HEADS, CROSS_HEAD_DIM)
    lg = jnp.einsum('bshc,bmhc->bhsm', q, kv[:, :, 0]).astype(jnp.float32) * (CROSS_HEAD_DIM ** -0.5)
    p = jax.nn.softmax(lg, axis=-1)
    o = jnp.einsum('bhsm,bmhc->bshc', p, kv[:, :, 1].astype(jnp.float32)).reshape(B, S, CROSS_HEADS * CROSS_HEAD_DIM)
    return o.astype(h.dtype) @ w_o


def peer(h, w_q, sub_keys, u, v):
    B, S, D = h.shape
    T = B * S
    hf = h.reshape(T, D)
    q = (hf @ w_q).reshape(T, PEER_HEADS, 2, PEER_KEY_DIM // 2)
    sc = jnp.einsum('thpc,pkc->thpk', q, sub_keys).astype(jnp.float32)
    v1, i1 = lax.top_k(sc, PEER_TOPK)
    cand = (v1[:, :, 0, :, None] + v1[:, :, 1, None, :]).reshape(T, PEER_HEADS, PEER_TOPK * PEER_TOPK)
    cidx = (i1[:, :, 0, :, None] * PEER_N_KEYS + i1[:, :, 1, None, :]).reshape(T, PEER_HEADS, PEER_TOPK * PEER_TOPK)
    top, pos = lax.top_k(cand, PEER_TOPK)
    experts = jnp.take_along_axis(cidx, pos, axis=-1).reshape(T, PEER_HEADS * PEER_TOPK)
    gates = jax.nn.softmax(top, axis=-1).reshape(T, PEER_HEADS * PEER_TOPK)
    C = PEER_TOKEN_CHUNK
    nc = T // C

    def token_block(args):
        xc, ec, gc = args
        a = jax.nn.gelu(jnp.einsum('cd,ced->ce', xc, u[ec]).astype(jnp.float32), approximate=False)
        return jnp.einsum('ce,ced->cd', gc * a, v[ec].astype(jnp.float32))

    y = lax.map(token_block, (hf.reshape(nc, C, D), experts.reshape(nc, C, -1), gates.reshape(nc, C, -1)))
    return y.reshape(B, S, D).astype(h.dtype)


def setup_inputs(seed: int = 0) -> dict:
    key = jax.random.key(seed)
    ks = jax.random.split(key, 20)
    f32 = jnp.float32
    D = D_MODEL

    def nrm(k, shape, scale):
        return jax.random.normal(k, shape, f32) * scale

    def gain(k, shape):
        return 1.0 + 0.05 * jax.random.normal(k, shape, f32)

    wa = DIL_HEADS * HEAD_DIM
    wb = MOBA_HEADS * HEAD_DIM
    wc = CROSS_HEADS * CROSS_HEAD_DIM
    return {
        'x': nrm(ks[0], (BATCH, SEQ, D), 1.0),
        'mem': nrm(ks[1], (BATCH, N_MEM, D), 1.0),
        'rel_bias': nrm(ks[2], (REL_BUCKETS, N_SELF_HEADS), 0.5),
        'g_mix': gain(ks[3], (DEPTH, D)),
        'w_in': nrm(ks[4], (DEPTH, D, IN_COLS), D ** -0.5),
        'b_gate': nrm(ks[5], (DEPTH, GATE_WIDTH), 0.02),
        'w_branch_a': nrm(ks[6], (DEPTH, wa, D), wa ** -0.5),
        'w_branch_b': nrm(ks[7], (DEPTH, wb, D), wb ** -0.5),
        'w_out': nrm(ks[8], (DEPTH, D, D), D ** -0.5),
        'g_cross': gain(ks[9], (DEPTH, D)),
        'g_mem': gain(ks[10], (DEPTH, D)),
        'w_q_cross': nrm(ks[11], (DEPTH, D, wc), D ** -0.5),
        'w_kv_cross': nrm(ks[12], (DEPTH, D, 2 * wc), D ** -0.5),
        'w_o_cross': nrm(ks[13], (DEPTH, wc, D), wc ** -0.5),
        'g_ffn': gain(ks[14], (DEPTH, D)),
        'w_q_peer': nrm(ks[15], (DEPTH, D, PEER_HEADS * PEER_KEY_DIM), D ** -0.5),
        'peer_sub_keys': nrm(ks[16], (DEPTH, 2, PEER_N_KEYS, PEER_KEY_DIM // 2), (PEER_KEY_DIM // 2) ** -0.5),
        'peer_u': nrm(ks[17], (DEPTH, PEER_N_EXPERTS, D), D ** -0.5),
        'peer_v': nrm(ks[18], (DEPTH, PEER_N_EXPERTS, D), 0.5),
        'g_final': gain(ks[19], (D,)),
    }


def reference(x, mem, rel_bias, g_mix, w_in, b_gate, w_branch_a, w_branch_b, w_out,
              g_cross, g_mem, w_q_cross, w_kv_cross, w_o_cross,
              g_ffn, w_q_peer, peer_sub_keys, peer_u, peer_v, g_final):
    B, S, D = x.shape
    for l in range(DEPTH):
        h = rmsnorm(x, g_mix[l])
        proj = h @ w_in[l]
        p_dil = proj[..., :DIL_WIDTH].reshape(B, S, N_DIL_GROUPS, 3, DIL_HEADS, HEAD_DIM)
        p_moba = proj[..., DIL_WIDTH:DIL_WIDTH + MOBA_WIDTH].reshape(B, S, 3, MOBA_HEADS, HEAD_DIM)
        p_gate = proj[..., DIL_WIDTH + MOBA_WIDTH:] + b_gate[l]
        outs = []
        lses = []
        for gi, (win, dil) in enumerate(DIL_GROUPS):
            o, s = dilated_group(p_dil[:, :, gi, 0], p_dil[:, :, gi, 1], p_dil[:, :, gi, 2], win, dil,
                                 rel_bias[:, gi * DIL_HEADS:(gi + 1) * DIL_HEADS])
            outs.append(o)
            lses.append(s)
        wgt = jax.nn.softmax(jnp.stack(lses, axis=0), axis=0)
        o_dil = jnp.sum(wgt[..., None] * jnp.stack(outs, axis=0), axis=0).reshape(B, S, DIL_HEADS * HEAD_DIM).astype(x.dtype)
        o_moba = moba(p_moba[:, :, 0], p_moba[:, :, 1], p_moba[:, :, 2],
                      rel_bias[:, N_DIL_GROUPS * DIL_HEADS:]).astype(x.dtype)
        gates = jax.nn.sigmoid(p_gate.astype(jnp.float32)).astype(x.dtype).reshape(B, S, 2, D)
        merged = gates[:, :, 0] * (o_dil @ w_branch_a[l]) + gates[:, :, 1] * (o_moba @ w_branch_b[l])
        x = x + merged @ w_out[l]
        x = x + cross_attention(rmsnorm(x, g_cross[l]), rmsnorm(mem, g_mem[l]),
                                w_q_cross[l], w_kv_cross[l], w_o_cross[l])
        x = x + peer(rmsnorm(x, g_ffn[l]), w_q_peer[l], peer_sub_keys[l], peer_u[l], peer_v[l])
    return rmsnorm(x, g_final)
```

```python
import functools
import math

import jax
import jax.numpy as jnp
import numpy as np
from jax import lax
from jax.experimental import pallas as pl
from jax.experimental.pallas import tpu as pltpu

HEAD_DIM = 64
DIL_GROUPS = ((128, 1), (512, 4), (2048, 16))
DIL_HEADS = 4
DIL_BLOCK = 128
MOBA_HEADS = 8
MOBA_BLOCK = 256
MOBA_TOPK = 3
MOBA_BIAS_TILES = 8
REL_BUCKETS = 32
REL_MAX_DIST = 2048
CROSS_HEADS = 4
CROSS_HEAD_DIM = 128
PEER_HEADS = 8
PEER_N_KEYS = 128
PEER_TOPK = 16
RMS_EPS = 1e-6
NEG_INF = -1e30
DIL_GROUP_COLS = 3 * DIL_HEADS * HEAD_DIM
DIL_WIDTH = len(DIL_GROUPS) * DIL_GROUP_COLS
MOBA_WIDTH = 3 * MOBA_HEADS * HEAD_DIM
QKV_WIDTH = DIL_WIDTH + MOBA_WIDTH

V7X_VMEM_BYTES = 64 * 1024 * 1024
VMEM_LIMIT = V7X_VMEM_BYTES * 3 // 4

BF16 = jnp.bfloat16
F32 = jnp.float32


def _params(*sem):
    return pltpu.CompilerParams(dimension_semantics=sem, vmem_limit_bytes=VMEM_LIMIT)


def _rms(x, g):
    return x * lax.rsqrt(jnp.mean(x * x, axis=-1, keepdims=True) + RMS_EPS) * g


def _nt_dot(a, b, precision=None):
    return lax.dot_general(a, b, (((1,), (1,)), ((), ())), preferred_element_type=F32, precision=precision)


def _norm_matmul_body(x_ref, g_ref, w_ref, *rest, sigmoid_bias):
    if sigmoid_bias:
        b_ref, o_ref, h_scr = rest
    else:
        o_ref, h_scr = rest

    @pl.when(pl.program_id(1) == 0)
    def _():
        h_scr[...] = _rms(x_ref[...], g_ref[...]).astype(h_scr.dtype)

    acc = jnp.dot(h_scr[...], w_ref[...], preferred_element_type=F32)
    if sigmoid_bias:
        acc = jax.nn.sigmoid(acc + b_ref[...])
    o_ref[...] = acc.astype(o_ref.dtype)


def _norm_matmul(x2d, g, w, *, out_dtype, tm, tn, bias=None, name):
    T, D = x2d.shape
    N = w.shape[1]
    tm = min(tm, T)
    assert T % tm == 0 and N % tn == 0
    in_specs = [
        pl.BlockSpec((tm, D), lambda i, j: (i, 0)),
        pl.BlockSpec((1, D), lambda i, j: (0, 0)),
        pl.BlockSpec((D, tn), lambda i, j: (0, j)),
    ]
    args = [x2d, g.reshape(1, D).astype(F32), w.astype(BF16)]
    if bias is not None:
        in_specs.append(pl.BlockSpec((1, tn), lambda i, j: (0, j)))
        args.append(bias.reshape(1, N).astype(F32))
    return pl.pallas_call(
        functools.partial(_norm_matmul_body, sigmoid_bias=bias is not None),
        grid=(T // tm, N // tn),
        in_specs=in_specs,
        out_specs=pl.BlockSpec((tm, tn), lambda i, j: (i, j)),
        out_shape=jax.ShapeDtypeStruct((T, N), out_dtype),
        scratch_shapes=[pltpu.VMEM((tm, D), BF16)],
        compiler_params=_params("parallel", "arbitrary"),
        name=name,
    )(*args)


def _t5_bucket_np(n):
    max_exact = REL_BUCKETS // 2
    nf = np.maximum(n, 1).astype(np.float32)
    large = max_exact + (np.log(nf / np.float32(max_exact)) / np.float32(math.log(REL_MAX_DIST / max_exact))
                         * np.float32(REL_BUCKETS - max_exact)).astype(np.int32)
    large = np.minimum(large, REL_BUCKETS - 1)
    return np.where(n < max_exact, n, large).astype(np.int32)


def _dil_bucket_map(d):
    blk = DIL_BLOCK
    rel = (blk + np.arange(blk))[:, None] - np.arange(2 * blk)[None, :]
    return _t5_bucket_np(np.clip(rel, 0, None) * d)


def _moba_bucket_map():
    blk = MOBA_BLOCK
    rel = np.arange(blk)[:, None] - np.arange(blk)[None, :]
    tiles = [_t5_bucket_np(np.clip(rel + dlt * blk, 0, None)) for dlt in range(MOBA_BIAS_TILES)]
    far = _t5_bucket_np(np.arange((MOBA_BIAS_TILES - 2) * blk + 1, 64 * blk))
    assert (far == REL_BUCKETS - 1).all()
    return np.stack(tiles)


def _bias_lookup_body(idx_ref, tab_ref, o_ref):
    h = pl.program_id(0)
    idx = idx_ref[...]

    def bucket(b, acc):
        return jnp.where(idx == b, tab_ref[b, h], acc)

    o_ref[0] = lax.fori_loop(0, REL_BUCKETS, bucket, jnp.zeros(idx.shape, F32))


def _bias_lookup(bucket_map, table, name):
    R, C = bucket_map.shape
    H = table.shape[1]
    rt = min(R, 256)
    assert R % rt == 0
    return pl.pallas_call(
        _bias_lookup_body,
        grid=(H, R // rt),
        in_specs=[pl.BlockSpec((rt, C), lambda h, i: (i, 0)), pl.BlockSpec(memory_space=pltpu.SMEM)],
        out_specs=pl.BlockSpec((1, rt, C), lambda h, i: (h, i, 0)),
        out_shape=jax.ShapeDtypeStruct((H, R, C), F32),
        compiler_params=_params("parallel", "parallel"),
        name=name,
    )(jnp.asarray(bucket_map), table.astype(F32))


def _dilated_body(q_ref, kp_ref, kc_ref, vp_ref, vc_ref, bias_ref, o_ref, lse_ref):
    blk, hd = DIL_BLOCK, HEAD_DIM
    n = pl.program_id(2)
    q = q_ref[0]
    k = jnp.concatenate([kp_ref[0], kc_ref[0]], axis=0)
    v = jnp.concatenate([vp_ref[0], vc_ref[0]], axis=0)
    qi = lax.broadcasted_iota(jnp.int32, (blk, 2 * blk), 0)
    kj = lax.broadcasted_iota(jnp.int32, (blk, 2 * blk), 1)
    rel = blk + qi - kj
    first_key = jnp.where(n > 0, 0, blk)
    mask = (rel >= 0) & (rel <= blk) & (kj >= first_key)
    for h in range(DIL_HEADS):
        sl = slice(h * hd, (h + 1) * hd)
        s = _nt_dot(q[:, sl], k[:, sl]) * (hd ** -0.5) + bias_ref[h]
        s = jnp.where(mask, s, NEG_INF)
        m = jnp.max(s, axis=-1, keepdims=True)
        p = jnp.exp(s - m)
        l = jnp.sum(p, axis=-1, keepdims=True)
        o = jnp.dot(p.astype(BF16), v[:, sl], preferred_element_type=F32) / l
        o_ref[0, :, sl] = o.astype(o_ref.dtype)
        lse_ref[0, :, sl] = jnp.broadcast_to(m + jnp.log(l), (blk, hd))


def _dilated_group(qkv, bias, gi, d):
    B, S, C = qkv.shape
    blk, gw = DIL_BLOCK, DIL_HEADS * HEAD_DIM
    L = S // d
    nb = L // blk
    assert L % blk == 0 and C % gw == 0
    cb = C // gw
    view = qkv.reshape(B, L, d * C)
    base = gi * 3

    def spec(j, prev):
        if prev:
            return pl.BlockSpec((1, blk, gw), lambda b, r, n: (b, jnp.maximum(n - 1, 0), r * cb + base + j))
        return pl.BlockSpec((1, blk, gw), lambda b, r, n: (b, n, r * cb + base + j))

    out_spec = pl.BlockSpec((1, blk, gw), lambda b, r, n: (b, n, r))
    o, lse = pl.pallas_call(
        _dilated_body,
        grid=(B, d, nb),
        in_specs=[spec(0, False), spec(1, True), spec(1, False), spec(2, True), spec(2, False),
                  pl.BlockSpec((DIL_HEADS, blk, 2 * blk), lambda b, r, n: (0, 0, 0))],
        out_specs=[out_spec, out_spec],
        out_shape=[jax.ShapeDtypeStruct((B, L, d * gw), BF16), jax.ShapeDtypeStruct((B, L, d * gw), F32)],
        compiler_params=_params("parallel", "parallel", "arbitrary"),
        name=f"dilated_g{gi}",
    )(view, view, view, view, view, bias)
    return o.reshape(B, S, gw), lse.reshape(B, S, gw)


def _moba_body(q_ref, k_ref, v_ref, bias_ref, o_ref, *, S):
    blk, hd = MOBA_BLOCK, HEAD_DIM
    nblk = S // blk
    scale = hd ** -0.5
    lane = lax.broadcasted_iota(jnp.int32, (1, 2 * hd), 1)
    kmean = jnp.mean(k_ref[0].astype(F32).reshape(nblk, blk, 2 * hd), axis=1)
    blk_id = lax.broadcasted_iota(jnp.int32, (blk, nblk), 1).astype(F32)
    qi = lax.broadcasted_iota(jnp.int32, (blk, blk), 0)
    kj = lax.broadcasted_iota(jnp.int32, (blk, blk), 1)

    def q_block(qb, _):
        row0 = pl.multiple_of(qb * blk, blk)
        q2 = q_ref[0, pl.ds(row0, blk), :]
        qbf = qb.astype(F32)
        outs = []
        for h in range(2):
            head_lanes = (lane >= h * hd) & (lane < (h + 1) * hd)
            qh = jnp.where(head_lanes, q2, jnp.zeros_like(q2))
            gs = _nt_dot(qh.astype(F32), kmean, precision=lax.Precision.HIGHEST)
            gs = jnp.where(blk_id < qbf, gs, NEG_INF)
            sel = jnp.zeros((blk, nblk), F32)
            for _r in range(MOBA_TOPK):
                gmax = jnp.max(gs, axis=-1, keepdims=True)
                first = jnp.min(jnp.where(gs == gmax, blk_id, float(nblk)), axis=-1, keepdims=True)
                pick = blk_id == first
                sel = jnp.where(pick & (blk_id < qbf), 1.0, sel)
                gs = jnp.where(pick, -jnp.inf, gs)
            k_own = k_ref[0, pl.ds(row0, blk), :]
            v_own = v_ref[0, pl.ds(row0, blk), :]
            s = _nt_dot(qh, k_own) * scale + bias_ref[0, h, 0]
            s = jnp.where(qi >= kj, s, NEG_INF)
            m = jnp.max(s, axis=-1, keepdims=True)
            p = jnp.exp(s - m)
            l = jnp.sum(p, axis=-1, keepdims=True)
            acc = jnp.dot(p.astype(BF16), v_own, preferred_element_type=F32)

            def past_block(n, carry):
                m, l, acc = carry
                col0 = pl.multiple_of(n * blk, blk)
                k_n = k_ref[0, pl.ds(col0, blk), :]
                v_n = v_ref[0, pl.ds(col0, blk), :]
                chosen = jnp.max(jnp.where(blk_id == n.astype(F32), sel, 0.0), axis=-1, keepdims=True) > 0.5
                tile = jnp.minimum(qb - n, MOBA_BIAS_TILES - 1)
                s = _nt_dot(qh, k_n) * scale + bias_ref[0, h, tile]
                s = jnp.where(chosen, s, NEG_INF)
                m_new = jnp.maximum(m, jnp.max(s, axis=-1, keepdims=True))
                a = jnp.exp(m - m_new)
                p = jnp.exp(s - m_new)
                l = l * a + jnp.sum(p, axis=-1, keepdims=True)
                acc = acc * a + jnp.dot(p.astype(BF16), v_n, preferred_element_type=F32)
                return m_new, l, acc

            m, l, acc = lax.fori_loop(0, qb, past_block, (m, l, acc))
            outs.append(acc / l)
        o = jnp.where(lane < hd, outs[0], outs[1])
        o_ref[0, pl.ds(row0, blk), :] = o.astype(o_ref.dtype)
        return 0

    lax.fori_loop(0, nblk, q_block, 0)


def _moba(qkv, bias):
    B, S, C = qkv.shape
    pw = 2 * HEAD_DIM
    npair = MOBA_HEADS // 2
    assert S % MOBA_BLOCK == 0
    c0 = DIL_WIDTH // pw

    def spec(j):
        return pl.BlockSpec((1, S, pw), lambda hp, b: (b, 0, c0 + j * npair + hp))

    return pl.pallas_call(
        functools.partial(_moba_body, S=S),
        grid=(npair, B),
        in_specs=[spec(0), spec(1), spec(2),
                  pl.BlockSpec((1, 2, MOBA_BIAS_TILES, MOBA_BLOCK, MOBA_BLOCK), lambda hp, b: (hp, 0, 0, 0, 0))],
        out_specs=pl.BlockSpec((1, S, pw), lambda hp, b: (b, 0, hp)),
        out_shape=jax.ShapeDtypeStruct((B, S, MOBA_HEADS * HEAD_DIM), BF16),
        compiler_params=_params("parallel", "parallel"),
        name="moba",
    )(qkv, qkv, qkv, bias)


def _merge_body(o0_ref, o1_ref, o2_ref, l0_ref, l1_ref, l2_ref, om_ref, gate_ref, x_ref,
                wa_ref, wb_ref, wo_ref, x1_ref):
    D = x_ref.shape[-1]
    l0, l1, l2 = l0_ref[...], l1_ref[...], l2_ref[...]
    mx = jnp.maximum(jnp.maximum(l0, l1), l2)
    e0, e1, e2 = jnp.exp(l0 - mx), jnp.exp(l1 - mx), jnp.exp(l2 - mx)
    o_dil = (e0 * o0_ref[...].astype(F32) + e1 * o1_ref[...].astype(F32) + e2 * o2_ref[...].astype(F32)) / (e0 + e1 + e2)
    a = jnp.dot(o_dil.astype(BF16), wa_ref[...], preferred_element_type=F32)
    b = jnp.dot(om_ref[...], wb_ref[...], preferred_element_type=F32)
    merged = gate_ref[:, :D].astype(F32) * a + gate_ref[:, D:].astype(F32) * b
    x1_ref[...] = x_ref[...] + jnp.dot(merged.astype(BF16), wo_ref[...], preferred_element_type=F32)


def _merge(o_dil, lse_dil, o_moba, gates, x2d, w_a, w_b, w_o, tm):
    T, D = x2d.shape
    tm = min(tm, T)

    def rows(width):
        return pl.BlockSpec((tm, width), lambda i: (i, 0))

    def whole(w):
        return pl.BlockSpec(w.shape, lambda i: (0, 0))

    ws = [w_a.astype(BF16), w_b.astype(BF16), w_o.astype(BF16)]
    gw = o_dil[0].shape[-1]
    return pl.pallas_call(
        _merge_body,
        grid=(T // tm,),
        in_specs=[rows(gw)] * 6 + [rows(o_moba.shape[-1]), rows(2 * D), rows(D)] + [whole(w) for w in ws],
        out_specs=rows(D),
        out_shape=jax.ShapeDtypeStruct((T, D), F32),
        compiler_params=_params("parallel"),
        name="merge",
    )(*o_dil, *lse_dil, o_moba, gates, x2d, *ws)


def _cross_body(q_ref, kv_ref, x_ref, wo_ref, x2_ref):
    hd = CROSS_HEAD_DIM
    width = CROSS_HEADS * hd
    heads = []
    for h in range(CROSS_HEADS):
        sl = slice(h * hd, (h + 1) * hd)
        s = _nt_dot(q_ref[0, :, sl], kv_ref[0, :, sl]) * (hd ** -0.5)
        m = jnp.max(s, axis=-1, keepdims=True)
        p = jnp.exp(s - m)
        l = jnp.sum(p, axis=-1, keepdims=True)
        v = kv_ref[0, :, width + h * hd: width + (h + 1) * hd]
        heads.append((jnp.dot(p.astype(BF16), v, preferred_element_type=F32) / l).astype(BF16))
    o = jnp.concatenate(heads, axis=-1)
    x2_ref[0] = x_ref[0] + jnp.dot(o, wo_ref[...], preferred_element_type=F32)


def _cross(q, kv, x, w_o, tm):
    B, S, D = x.shape
    M = kv.shape[1]
    tm = min(tm, S)
    w_o = w_o.astype(BF16)
    return pl.pallas_call(
        _cross_body,
        grid=(B, S // tm),
        in_specs=[pl.BlockSpec((1, tm, q.shape[-1]), lambda b, i: (b, i, 0)),
                  pl.BlockSpec((1, M, kv.shape[-1]), lambda b, i: (b, 0, 0)),
                  pl.BlockSpec((1, tm, D), lambda b, i: (b, i, 0)),
                  pl.BlockSpec(w_o.shape, lambda b, i: (0, 0))],
        out_specs=pl.BlockSpec((1, tm, D), lambda b, i: (b, i, 0)),
        out_shape=jax.ShapeDtypeStruct((B, S, D), F32),
        compiler_params=_params("parallel", "parallel"),
        name="cross",
    )(q, kv, x, w_o)


def _top_sorted(s, k):
    rows = []
    for _ in range(k):
        m = jnp.max(s, axis=0, keepdims=True)
        rows.append(m)
        s = jnp.where(s == m, -jnp.inf, s)
    return jnp.concatenate(rows, axis=0)


def _pair_sum_candidates(a, b):
    assert PEER_TOPK == 16
    row = lax.broadcasted_iota(jnp.int32, (PEER_TOPK, 1), 0)
    blocks = [a[0:1] + b]
    blocks += [a[i:i + 1] + b[0:8] for i in (1, 2, 3)]
    blocks += [jnp.where(row >= 4, b[0:1] + a, -jnp.inf)]
    blocks += [jnp.where(row[0:8] >= 4, b[j:j + 1] + a[0:8], -jnp.inf) for j in (1, 2)]
    return jnp.concatenate(blocks, axis=0)


def _peer_route_body(q_ref, keys_ref, s1_ref, s2_ref, e1_ref, e2_ref, tau_ref):
    nk, K = PEER_N_KEYS, PEER_TOPK
    kd = keys_ref.shape[-1]
    for h in range(PEER_HEADS):
        sc = []
        for p in range(2):
            c0 = (2 * h + p) * kd
            sc.append(_nt_dot(keys_ref[p], q_ref[:, c0:c0 + kd], precision=lax.Precision.HIGHEST))
        a = _top_sorted(sc[0], K)
        b = _top_sorted(sc[1], K)
        cand = _pair_sum_candidates(a, b)
        cur, tau = cand, cand[0:1]
        count = jnp.zeros_like(tau)
        for _ in range(K):
            m = jnp.max(cur, axis=0, keepdims=True)
            eq = cur == m
            tau = jnp.where(count < K, m, tau)
            count = count + jnp.sum(jnp.where(eq, 1.0, 0.0), axis=0, keepdims=True)
            cur = jnp.where(eq, -jnp.inf, cur)
        top = a[0:1] + b[0:1]
        z = jnp.sum(jnp.where(cand >= tau, jnp.exp(cand - top), 0.0), axis=0, keepdims=True)
        s1_ref[h] = sc[0]
        s2_ref[h] = sc[1]
        e1_ref[h] = jnp.exp(sc[0] - a[0:1]) / z
        e2_ref[h] = jnp.exp(sc[1] - b[0:1])
        tau_ref[pl.ds(h, 1), :] = tau


def _peer_route(q, sub_keys, tt):
    T = q.shape[0]
    tt = min(tt, T)
    nk = PEER_N_KEYS
    big = jax.ShapeDtypeStruct((PEER_HEADS, nk, T), F32)
    big_spec = pl.BlockSpec((PEER_HEADS, nk, tt), lambda i: (0, 0, i))
    return pl.pallas_call(
        _peer_route_body,
        grid=(T // tt,),
        in_specs=[pl.BlockSpec((tt, q.shape[1]), lambda i: (i, 0)),
                  pl.BlockSpec(sub_keys.shape, lambda i: (0, 0, 0))],
        out_specs=[big_spec] * 4 + [pl.BlockSpec((PEER_HEADS, tt), lambda i: (0, i))],
        out_shape=[big] * 4 + [jax.ShapeDtypeStruct((PEER_HEADS, T), F32)],
        compiler_params=_params("parallel"),
        name="peer_route",
    )(q, sub_keys.astype(F32))


def _peer_main_body(h_ref, u_ref, vt_ref, s1_ref, s2_ref, e1_ref, e2_ref, tau_ref, x_ref, g_ref,
                    o_ref, acc_ref, ga_ref, *, rows_per_step, lane_chunk):
    nk = PEER_N_KEYS
    j = pl.program_id(1)
    tt = h_ref.shape[0]

    @pl.when(j == 0)
    def _():
        acc_ref[...] = jnp.zeros_like(acc_ref)

    pre = _nt_dot(u_ref[...], h_ref[...])
    act = 0.5 * pre * (1.0 + lax.erf(pre * (2.0 ** -0.5)))
    for r in range(rows_per_step):
        i1 = j * rows_per_step + r
        for c in range(tt // lane_chunk):
            cs = slice(c * lane_chunk, (c + 1) * lane_chunk)
            gate = jnp.zeros((nk, lane_chunk), F32)
            for h in range(PEER_HEADS):
                s1 = s1_ref[h, pl.ds(i1, 1), cs]
                e1 = e1_ref[h, pl.ds(i1, 1), cs]
                picked = (s1 + s2_ref[h, :, cs]) >= tau_ref[pl.ds(h, 1), cs]
                gate = gate + jnp.where(picked, e1 * e2_ref[h, :, cs], 0.0)
            ga_ref[r * nk:(r + 1) * nk, cs] = (gate * act[r * nk:(r + 1) * nk, cs]).astype(ga_ref.dtype)
    acc_ref[...] += jnp.dot(vt_ref[...], ga_ref[...], preferred_element_type=F32)

    @pl.when(j == pl.num_programs(1) - 1)
    def _():
        o_ref[...] = _rms(x_ref[...] + acc_ref[...].T, g_ref[...])


def _peer_main(hn, u, vt, s1, s2, e1, e2, tau, x2d, g_final, tt, rows_per_step=2, lane_chunk=256):
    T, D = x2d.shape
    tt = min(tt, T)
    lane_chunk = min(lane_chunk, tt)
    nk = PEER_N_KEYS
    eb = rows_per_step * nk
    n_exp = u.shape[0]
    assert n_exp == nk * nk and nk % rows_per_step == 0
    score_spec = pl.BlockSpec((PEER_HEADS, nk, tt), lambda i, j: (0, 0, i))
    return pl.pallas_call(
        functools.partial(_peer_main_body, rows_per_step=rows_per_step, lane_chunk=lane_chunk),
        grid=(T // tt, n_exp // eb),
        in_specs=[pl.BlockSpec((tt, D), lambda i, j: (i, 0)),
                  pl.BlockSpec((eb, D), lambda i, j: (j, 0)),
                  pl.BlockSpec((D, eb), lambda i, j: (0, j)),
                  score_spec, score_spec, score_spec, score_spec,
                  pl.BlockSpec((PEER_HEADS, tt), lambda i, j: (0, i)),
                  pl.BlockSpec((tt, D), lambda i, j: (i, 0)),
                  pl.BlockSpec((1, D), lambda i, j: (0, 0))],
        out_specs=pl.BlockSpec((tt, D), lambda i, j: (i, 0)),
        out_shape=jax.ShapeDtypeStruct((T, D), F32),
        scratch_shapes=[pltpu.VMEM((D, tt), F32), pltpu.VMEM((eb, tt), BF16)],
        compiler_params=_params("parallel", "arbitrary"),
        name="peer_main",
    )(hn, u, vt, s1, s2, e1, e2, tau, x2d, g_final.reshape(1, D).astype(F32))


def _norm_cast_body(x_ref, g_ref, o_ref):
    o_ref[...] = _rms(x_ref[...], g_ref[...]).astype(o_ref.dtype)


def _norm_cast(x2d, g, tm):
    T, D = x2d.shape
    tm = min(tm, T)
    return pl.pallas_call(
        _norm_cast_body,
        grid=(T // tm,),
        in_specs=[pl.BlockSpec((tm, D), lambda i: (i, 0)), pl.BlockSpec((1, D), lambda i: (0, 0))],
        out_specs=pl.BlockSpec((tm, D), lambda i: (i, 0)),
        out_shape=jax.ShapeDtypeStruct((T, D), BF16),
        compiler_params=_params("parallel"),
        name="norm_cast",
    )(x2d, g.reshape(1, D).astype(F32))


def kernel(x, mem, rel_bias, g_mix, w_in, b_gate, w_branch_a, w_branch_b, w_out, g_cross, g_mem, w_q_cross,
           w_kv_cross, w_o_cross, g_ffn, w_q_peer, peer_sub_keys, peer_u, peer_v, g_final):
    B, S, D = x.shape
    T = B * S
    depth = w_in.shape[0]
    ndil = len(DIL_GROUPS) * DIL_HEADS
    x2d = x.reshape(T, D)

    dil_bias = [_bias_lookup(_dil_bucket_map(d), rel_bias[:, gi * DIL_HEADS:(gi + 1) * DIL_HEADS], f"dil_bias_g{gi}")
                for gi, (_, d) in enumerate(DIL_GROUPS)]
    mmap = _moba_bucket_map()
    moba_bias = _bias_lookup(mmap.reshape(-1, MOBA_BLOCK), rel_bias[:, ndil:], "moba_bias")
    moba_bias = moba_bias.reshape(MOBA_HEADS // 2, 2, MOBA_BIAS_TILES, MOBA_BLOCK, MOBA_BLOCK)

    assert depth == 1, "the final norm is fused into the last PEER sweep; one layer supported"
    l = 0
    qkv = _norm_matmul(x2d, g_mix[l], w_in[l][:, :QKV_WIDTH], out_dtype=BF16, tm=1024, tn=1280, name="proj_qkv")
    gates = _norm_matmul(x2d, g_mix[l], w_in[l][:, QKV_WIDTH:], out_dtype=BF16, tm=1024, tn=1024,
                         bias=b_gate[l], name="proj_gates")
    qkv3 = qkv.reshape(B, S, QKV_WIDTH)
    dil = [_dilated_group(qkv3, dil_bias[gi], gi, d) for gi, (_, d) in enumerate(DIL_GROUPS)]
    o_moba = _moba(qkv3, moba_bias).reshape(T, MOBA_HEADS * HEAD_DIM)
    o_dil = [o.reshape(T, -1) for o, _ in dil]
    lse_dil = [s.reshape(T, -1) for _, s in dil]
    x2d = _merge(o_dil, lse_dil, o_moba, gates, x2d, w_branch_a[l], w_branch_b[l], w_out[l], tm=512)
    M = mem.shape[1]
    q_c = _norm_matmul(x2d, g_cross[l], w_q_cross[l], out_dtype=BF16, tm=1024, tn=512, name="cross_q")
    kv = _norm_matmul(mem.reshape(B * M, D), g_mem[l], w_kv_cross[l], out_dtype=BF16, tm=1024, tn=1024, name="cross_kv")
    x2d = _cross(q_c.reshape(B, S, -1), kv.reshape(B, M, -1), x2d.reshape(B, S, D), w_o_cross[l], tm=1024).reshape(T, D)
    q_p = _norm_matmul(x2d, g_ffn[l], w_q_peer[l], out_dtype=F32, tm=1024, tn=1024, name="peer_q")
    hn = _norm_cast(x2d, g_ffn[l], tm=1024)
    s1, s2, e1, e2, tau = _peer_route(q_p, peer_sub_keys[l], tt=512)
    y = _peer_main(hn, peer_u[l].astype(BF16), peer_v[l].T.astype(BF16), s1, s2, e1, e2, tau, x2d, g_final, tt=512)
    return y.reshape(B, S, D)
```

```python
import functools
import math

import jax
import jax.numpy as jnp
import numpy as np
from jax import lax
from jax.experimental import pallas as pl
from jax.experimental.pallas import tpu as pltpu

HEAD_DIM = 64
DIL_GROUPS = ((128, 1), (512, 4), (2048, 16))
DIL_HEADS = 4
DIL_BLOCK = 128
MOBA_HEADS = 8
MOBA_BLOCK = 256
MOBA_TOPK = 3
MOBA_BIAS_TILES = 8
MOBA_GROUP = 4
MOBA_MASK = -(2.0 ** 100)
REL_BUCKETS = 32
REL_MAX_DIST = 2048
CROSS_HEADS = 4
CROSS_HEAD_DIM = 128
PEER_HEADS = 8
PEER_N_KEYS = 128
PEER_TOPK = 16
RMS_EPS = 1e-6
NEG_INF = -1e30
DIL_GROUP_COLS = 3 * DIL_HEADS * HEAD_DIM
DIL_WIDTH = len(DIL_GROUPS) * DIL_GROUP_COLS
MOBA_WIDTH = 3 * MOBA_HEADS * HEAD_DIM
QKV_WIDTH = DIL_WIDTH + MOBA_WIDTH

V7X_VMEM_BYTES = 64 * 1024 * 1024
VMEM_LIMIT = V7X_VMEM_BYTES * 3 // 4

BF16 = jnp.bfloat16
F32 = jnp.float32


def _params(*sem):
    return pltpu.CompilerParams(dimension_semantics=sem, vmem_limit_bytes=VMEM_LIMIT)


def _rms(x, g):
    return x * lax.rsqrt(jnp.mean(x * x, axis=-1, keepdims=True) + RMS_EPS) * g


def _nt_dot(a, b, precision=None):
    return lax.dot_general(a, b, (((1,), (1,)), ((), ())), preferred_element_type=F32, precision=precision)


def _norm_matmul_body(x_ref, g_ref, w_ref, *rest, sigmoid_bias):
    if sigmoid_bias:
        b_ref, o_ref, h_scr = rest
    else:
        o_ref, h_scr = rest

    @pl.when(pl.program_id(1) == 0)
    def _():
        h_scr[...] = _rms(x_ref[...], g_ref[...]).astype(h_scr.dtype)

    acc = jnp.dot(h_scr[...], w_ref[...], preferred_element_type=F32)
    if sigmoid_bias:
        acc = jax.nn.sigmoid(acc + b_ref[...])
    o_ref[...] = acc.astype(o_ref.dtype)


def _norm_matmul(x2d, g, w, *, out_dtype, tm, tn, bias=None, name):
    T, D = x2d.shape
    N = w.shape[1]
    tm = min(tm, T)
    assert T % tm == 0 and N % tn == 0
    in_specs = [
        pl.BlockSpec((tm, D), lambda i, j: (i, 0)),
        pl.BlockSpec((1, D), lambda i, j: (0, 0)),
        pl.BlockSpec((D, tn), lambda i, j: (0, j)),
    ]
    args = [x2d, g.reshape(1, D).astype(F32), w.astype(BF16)]
    if bias is not None:
        in_specs.append(pl.BlockSpec((1, tn), lambda i, j: (0, j)))
        args.append(bias.reshape(1, N).astype(F32))
    return pl.pallas_call(
        functools.partial(_norm_matmul_body, sigmoid_bias=bias is not None),
        grid=(T // tm, N // tn),
        in_specs=in_specs,
        out_specs=pl.BlockSpec((tm, tn), lambda i, j: (i, j)),
        out_shape=jax.ShapeDtypeStruct((T, N), out_dtype),
        scratch_shapes=[pltpu.VMEM((tm, D), BF16)],
        compiler_params=_params("parallel", "arbitrary"),
        name=name,
    )(*args)


def _t5_bucket_np(n):
    max_exact = REL_BUCKETS // 2
    nf = np.maximum(n, 1).astype(np.float32)
    large = max_exact + (np.log(nf / np.float32(max_exact)) / np.float32(math.log(REL_MAX_DIST / max_exact))
                         * np.float32(REL_BUCKETS - max_exact)).astype(np.int32)
    large = np.minimum(large, REL_BUCKETS - 1)
    return np.where(n < max_exact, n, large).astype(np.int32)


def _dil_bucket_map(d):
    blk = DIL_BLOCK
    rel = (blk + np.arange(blk))[:, None] - np.arange(2 * blk)[None, :]
    return _t5_bucket_np(np.clip(rel, 0, None) * d)


def _moba_bucket_map():
    blk = MOBA_BLOCK
    rel = np.arange(blk)[:, None] - np.arange(blk)[None, :]
    tiles = [_t5_bucket_np(np.clip(rel + dlt * blk, 0, None)) for dlt in range(MOBA_BIAS_TILES)]
    tiles[0] = np.where(rel >= 0, tiles[0], -1)
    far = _t5_bucket_np(np.arange((MOBA_BIAS_TILES - 2) * blk + 1, 64 * blk))
    assert (far == REL_BUCKETS - 1).all()
    return np.stack(tiles)


def _bias_lookup_body(idx_ref, tab_ref, o_ref):
    h = pl.program_id(0)
    idx = idx_ref[...]

    def bucket(b, acc):
        return jnp.where(idx == b, tab_ref[b, h], acc)

    o_ref[0] = lax.fori_loop(0, REL_BUCKETS, bucket, jnp.full(idx.shape, NEG_INF, F32))


def _bias_lookup(bucket_map, table, name):
    R, C = bucket_map.shape
    H = table.shape[1]
    rt = min(R, 256)
    assert R % rt == 0
    return pl.pallas_call(
        _bias_lookup_body,
        grid=(H, R // rt),
        in_specs=[pl.BlockSpec((rt, C), lambda h, i: (i, 0)), pl.BlockSpec(memory_space=pltpu.SMEM)],
        out_specs=pl.BlockSpec((1, rt, C), lambda h, i: (h, i, 0)),
        out_shape=jax.ShapeDtypeStruct((H, R, C), F32),
        compiler_params=_params("parallel", "parallel"),
        name=name,
    )(jnp.asarray(bucket_map), table.astype(F32))


def _dilated_body(q_ref, kp_ref, kc_ref, vp_ref, vc_ref, bias_ref, o_ref, lse_ref):
    blk, hd = DIL_BLOCK, HEAD_DIM
    n = pl.program_id(2)
    q = q_ref[0]
    k = jnp.concatenate([kp_ref[0], kc_ref[0]], axis=0)
    v = jnp.concatenate([vp_ref[0], vc_ref[0]], axis=0)
    qi = lax.broadcasted_iota(jnp.int32, (blk, 2 * blk), 0)
    kj = lax.broadcasted_iota(jnp.int32, (blk, 2 * blk), 1)
    rel = blk + qi - kj
    first_key = jnp.where(n > 0, 0, blk)
    mask = (rel >= 0) & (rel <= blk) & (kj >= first_key)
    for h in range(DIL_HEADS):
        sl = slice(h * hd, (h + 1) * hd)
        s = _nt_dot(q[:, sl], k[:, sl]) * (hd ** -0.5) + bias_ref[h]
        s = jnp.where(mask, s, NEG_INF)
        m = jnp.max(s, axis=-1, keepdims=True)
        p = jnp.exp(s - m)
        l = jnp.sum(p, axis=-1, keepdims=True)
        o = jnp.dot(p.astype(BF16), v[:, sl], preferred_element_type=F32) / l
        o_ref[0, :, sl] = o.astype(o_ref.dtype)
        lse_ref[0, :, sl] = jnp.broadcast_to(m + jnp.log(l), (blk, hd))


def _dilated_group(qkv, bias, gi, d):
    B, S, C = qkv.shape
    blk, gw = DIL_BLOCK, DIL_HEADS * HEAD_DIM
    L = S // d
    nb = L // blk
    assert L % blk == 0 and C % gw == 0
    cb = C // gw
    view = qkv.reshape(B, L, d * C)
    base = gi * 3

    def spec(j, prev):
        if prev:
            return pl.BlockSpec((1, blk, gw), lambda b, r, n: (b, jnp.maximum(n - 1, 0), r * cb + base + j))
        return pl.BlockSpec((1, blk, gw), lambda b, r, n: (b, n, r * cb + base + j))

    out_spec = pl.BlockSpec((1, blk, gw), lambda b, r, n: (b, n, r))
    o, lse = pl.pallas_call(
        _dilated_body,
        grid=(B, d, nb),
        in_specs=[spec(0, False), spec(1, True), spec(1, False), spec(2, True), spec(2, False),
                  pl.BlockSpec((DIL_HEADS, blk, 2 * blk), lambda b, r, n: (0, 0, 0))],
        out_specs=[out_spec, out_spec],
        out_shape=[jax.ShapeDtypeStruct((B, L, d * gw), BF16), jax.ShapeDtypeStruct((B, L, d * gw), F32)],
        compiler_params=_params("parallel", "parallel", "arbitrary"),
        name=f"dilated_g{gi}",
    )(view, view, view, view, view, bias)
    return o.reshape(B, S, gw), lse.reshape(B, S, gw)


def _moba_body(q_ref, k_ref, v_ref, bias_ref, o_ref, qa_ref, ka_ref, *, S):
    blk, hd = MOBA_BLOCK, HEAD_DIM
    nblk = S // blk
    scale = hd ** -0.5
    lane = lax.broadcasted_iota(jnp.int32, (1, 2 * hd), 1)
    kmean = jnp.mean(k_ref[0].astype(F32).reshape(nblk, blk, 2 * hd), axis=1)
    grp = MOBA_GROUP * blk
    row_blk = lax.broadcasted_iota(jnp.int32, (S, 2 * hd), 0) // blk
    lane_f = lane.astype(F32)
    q2, k2 = q_ref[0], k_ref[0]

    for h in range(2):
        head_lanes = (lane >= h * hd) & (lane < (h + 1) * hd)
        off = hd * (1 - h)
        blk_id = lane - off
        in_range = (blk_id >= 0) & (blk_id < nblk)
        km = jnp.where(head_lanes, kmean, 0.0)
        pieces = ([jnp.zeros((off, 2 * hd), F32)] if off else []) + [km, jnp.zeros((2 * hd - off - nblk, 2 * hd), F32)]
        km_rows = jnp.concatenate(pieces, axis=0)
        qh = jnp.where(head_lanes, q2, jnp.zeros_like(q2))
        gs = _nt_dot(qh.astype(F32), km_rows, precision=lax.Precision.HIGHEST)
        past = in_range & (blk_id < row_blk)
        gs = jnp.where(past, gs, jnp.where(in_range, NEG_INF, -jnp.inf))
        sel = jnp.zeros(gs.shape, F32)
        for _r in range(MOBA_TOPK):
            gmax = jnp.max(gs, axis=-1, keepdims=True)
            first = jnp.min(jnp.where(gs == gmax, lane_f, 4.0 * hd), axis=-1, keepdims=True)
            pick = lane_f == first
            sel = jnp.where(pick & past, 1.0, sel)
            gs = jnp.where(pick, -jnp.inf, gs)
        own = in_range & (blk_id == row_blk)
        penalty = jnp.where((sel > 0.5) | own | jnp.logical_not(in_range), 0.0, MOBA_MASK)
        qa_ref[h] = jnp.where(head_lanes, q2 * scale, penalty.astype(BF16))
        ka_ref[h] = jnp.where(head_lanes, k2, jnp.where(own, 1.0, 0.0).astype(BF16))

    def q_block(qb, _):
        row0 = pl.multiple_of(qb * blk, blk)
        n_grp = qb // MOBA_GROUP + 1
        qa = [qa_ref[h, pl.ds(row0, blk), :] for h in range(2)]

        def key_group(i, carry):
            g = n_grp - 1 - i
            col0 = pl.multiple_of(g * grp, grp)
            vg = v_ref[0, pl.ds(col0, grp), :]
            out = []
            for h in range(2):
                m, l, acc = carry[h]
                s = _nt_dot(qa[h], ka_ref[h, pl.ds(col0, grp), :])
                parts = []
                for jj in range(MOBA_GROUP):
                    tile = jnp.clip(qb - (g * MOBA_GROUP + jj), 0, MOBA_BIAS_TILES - 1)
                    parts.append(s[:, jj * blk:(jj + 1) * blk] + bias_ref[0, h, tile])
                m_new = jnp.maximum(m, jnp.max(functools.reduce(jnp.maximum, parts), axis=-1, keepdims=True))
                a = jnp.exp(m - m_new)
                ps = [jnp.exp(p - m_new) for p in parts]
                l = l * a + jnp.sum(functools.reduce(jnp.add, ps), axis=-1, keepdims=True)
                p_all = jnp.concatenate([p.astype(BF16) for p in ps], axis=-1)
                acc = acc * a + jnp.dot(p_all, vg, preferred_element_type=F32)
                out.append((m_new, l, acc))
            return tuple(out)

        init = tuple((jnp.full((blk, 1), -jnp.inf, F32), jnp.zeros((blk, 1), F32), jnp.zeros((blk, 2 * hd), F32))
                     for _ in range(2))
        res = lax.fori_loop(0, n_grp, key_group, init)
        o = jnp.where(lane < hd, res[0][2] / res[0][1], res[1][2] / res[1][1])
        o_ref[0, pl.ds(row0, blk), :] = o.astype(o_ref.dtype)
        return 0

    lax.fori_loop(0, nblk, q_block, 0)


def _moba(qkv, bias):
    B, S, C = qkv.shape
    pw = 2 * HEAD_DIM
    npair = MOBA_HEADS // 2
    assert S % (MOBA_GROUP * MOBA_BLOCK) == 0 and S // MOBA_BLOCK <= HEAD_DIM
    c0 = DIL_WIDTH // pw

    def spec(j):
        return pl.BlockSpec((1, S, pw), lambda hp, b: (b, 0, c0 + j * npair + hp))

    return pl.pallas_call(
        functools.partial(_moba_body, S=S),
        grid=(npair, B),
        in_specs=[spec(0), spec(1), spec(2),
                  pl.BlockSpec((1, 2, MOBA_BIAS_TILES, MOBA_BLOCK, MOBA_BLOCK), lambda hp, b: (hp, 0, 0, 0, 0))],
        out_specs=pl.BlockSpec((1, S, pw), lambda hp, b: (b, 0, hp)),
        out_shape=jax.ShapeDtypeStruct((B, S, MOBA_HEADS * HEAD_DIM), BF16),
        scratch_shapes=[pltpu.VMEM((2, S, pw), BF16), pltpu.VMEM((2, S, pw), BF16)],
        compiler_params=_params("parallel", "parallel"),
        name="moba",
    )(qkv, qkv, qkv, bias)


def _merge_body(o0_ref, o1_ref, o2_ref, l0_ref, l1_ref, l2_ref, om_ref, gate_ref, x_ref,
                wa_ref, wb_ref, wo_ref, x1_ref):
    D = x_ref.shape[-1]
    l0, l1, l2 = l0_ref[...], l1_ref[...], l2_ref[...]
    mx = jnp.maximum(jnp.maximum(l0, l1), l2)
    e0, e1, e2 = jnp.exp(l0 - mx), jnp.exp(l1 - mx), jnp.exp(l2 - mx)
    o_dil = (e0 * o0_ref[...].astype(F32) + e1 * o1_ref[...].astype(F32) + e2 * o2_ref[...].astype(F32)) / (e0 + e1 + e2)
    a = jnp.dot(o_dil.astype(BF16), wa_ref[...], preferred_element_type=F32)
    b = jnp.dot(om_ref[...], wb_ref[...], preferred_element_type=F32)
    merged = gate_ref[:, :D].astype(F32) * a + gate_ref[:, D:].astype(F32) * b
    x1_ref[...] = x_ref[...] + jnp.dot(merged.astype(BF16), wo_ref[...], preferred_element_type=F32)


def _merge(o_dil, lse_dil, o_moba, gates, x2d, w_a, w_b, w_o, tm):
    T, D = x2d.shape
    tm = min(tm, T)

    def rows(width):
        return pl.BlockSpec((tm, width), lambda i: (i, 0))

    def whole(w):
        return pl.BlockSpec(w.shape, lambda i: (0, 0))

    ws = [w_a.astype(BF16), w_b.astype(BF16), w_o.astype(BF16)]
    gw = o_dil[0].shape[-1]
    return pl.pallas_call(
        _merge_body,
        grid=(T // tm,),
        in_specs=[rows(gw)] * 6 + [rows(o_moba.shape[-1]), rows(2 * D), rows(D)] + [whole(w) for w in ws],
        out_specs=rows(D),
        out_shape=jax.ShapeDtypeStruct((T, D), F32),
        compiler_params=_params("parallel"),
        name="merge",
    )(*o_dil, *lse_dil, o_moba, gates, x2d, *ws)


def _cross_body(q_ref, kv_ref, x_ref, wo_ref, x2_ref):
    hd = CROSS_HEAD_DIM
    width = CROSS_HEADS * hd
    heads = []
    for h in range(CROSS_HEADS):
        sl = slice(h * hd, (h + 1) * hd)
        s = _nt_dot(q_ref[0, :, sl], kv_ref[0, :, sl]) * (hd ** -0.5)
        m = jnp.max(s, axis=-1, keepdims=True)
        p = jnp.exp(s - m)
        l = jnp.sum(p, axis=-1, keepdims=True)
        v = kv_ref[0, :, width + h * hd: width + (h + 1) * hd]
        heads.append((jnp.dot(p.astype(BF16), v, preferred_element_type=F32) / l).astype(BF16))
    o = jnp.concatenate(heads, axis=-1)
    x2_ref[0] = x_ref[0] + jnp.dot(o, wo_ref[...], preferred_element_type=F32)


def _cross(q, kv, x, w_o, tm):
    B, S, D = x.shape
    M = kv.shape[1]
    tm = min(tm, S)
    w_o = w_o.astype(BF16)
    return pl.pallas_call(
        _cross_body,
        grid=(B, S // tm),
        in_specs=[pl.BlockSpec((1, tm, q.shape[-1]), lambda b, i: (b, i, 0)),
                  pl.BlockSpec((1, M, kv.shape[-1]), lambda b, i: (b, 0, 0)),
                  pl.BlockSpec((1, tm, D), lambda b, i: (b, i, 0)),
                  pl.BlockSpec(w_o.shape, lambda b, i: (0, 0))],
        out_specs=pl.BlockSpec((1, tm, D), lambda b, i: (b, i, 0)),
        out_shape=jax.ShapeDtypeStruct((B, S, D), F32),
        compiler_params=_params("parallel", "parallel"),
        name="cross",
    )(q, kv, x, w_o)


def _top_sorted(s, k):
    rows = []
    for _ in range(k):
        m = jnp.max(s, axis=0, keepdims=True)
        rows.append(m)
        s = jnp.where(s == m, -jnp.inf, s)
    return jnp.concatenate(rows, axis=0)


def _pair_sum_candidates(a, b):
    assert PEER_TOPK == 16
    row = lax.broadcasted_iota(jnp.int32, (PEER_TOPK, 1), 0)
    blocks = [a[0:1] + b]
    blocks += [a[i:i + 1] + b[0:8] for i in (1, 2, 3)]
    blocks += [jnp.where(row >= 4, b[0:1] + a, -jnp.inf)]
    blocks += [jnp.where(row[0:8] >= 4, b[j:j + 1] + a[0:8], -jnp.inf) for j in (1, 2)]
    return jnp.concatenate(blocks, axis=0)


def _peer_route_body(q_ref, keys_ref, thr_ref, e1_ref, s2_ref, e2_ref):
    nk, K = PEER_N_KEYS, PEER_TOPK
    kd = keys_ref.shape[-1]
    for h in range(PEER_HEADS):
        sc = []
        for p in range(2):
            c0 = (2 * h + p) * kd
            sc.append(_nt_dot(keys_ref[p], q_ref[:, c0:c0 + kd], precision=lax.Precision.HIGHEST))
        a = _top_sorted(sc[0], K + 1)
        b = _top_sorted(sc[1], K + 1)
        cand = _pair_sum_candidates(a[:K], b[:K])
        cur, tau = cand, cand[0:1]
        count = jnp.zeros_like(tau)
        for _ in range(K):
            m = jnp.max(cur, axis=0, keepdims=True)
            eq = cur == m
            tau = jnp.where(count < K, m, tau)
            count = count + jnp.sum(jnp.where(eq, 1.0, 0.0), axis=0, keepdims=True)
            cur = jnp.where(eq, -jnp.inf, cur)
        below = jnp.max(jnp.where(cand < tau, cand, -jnp.inf), axis=0, keepdims=True)
        for corner in (a[0:1] + b[K:K + 1], a[K:K + 1] + b[0:1]):
            below = jnp.maximum(below, jnp.where(corner < tau, corner, -jnp.inf))
        below = jnp.where(below == -jnp.inf, tau, below)
        cut = 0.5 * (tau + below)
        top = a[0:1] + b[0:1]
        z = jnp.sum(jnp.where(cand >= tau, jnp.exp(cand - top), 0.0), axis=0, keepdims=True)
        thr_ref[h] = cut - sc[0]
        e1_ref[h] = jnp.exp(sc[0] - a[0:1]) / z
        s2_ref[h] = sc[1]
        e2_ref[h] = jnp.exp(sc[1] - b[0:1])


def _peer_route(q, sub_keys, tt):
    T = q.shape[0]
    tt = min(tt, T)
    nk = PEER_N_KEYS
    big = jax.ShapeDtypeStruct((PEER_HEADS, nk, T), F32)
    big_spec = pl.BlockSpec((PEER_HEADS, nk, tt), lambda i: (0, 0, i))
    return pl.pallas_call(
        _peer_route_body,
        grid=(T // tt,),
        in_specs=[pl.BlockSpec((tt, q.shape[1]), lambda i: (i, 0)),
                  pl.BlockSpec(sub_keys.shape, lambda i: (0, 0, 0))],
        out_specs=[big_spec] * 4,
        out_shape=[big] * 4,
        compiler_params=_params("parallel"),
        name="peer_route",
    )(q, sub_keys.astype(F32))


def _peer_main_body(h_ref, u_ref, vt_ref, thr_ref, e1_ref, s2_ref, e2_ref, x_ref, g_ref,
                    o_ref, acc_ref, pre_ref, ga_ref, *, rows_per_step):
    nk = PEER_N_KEYS
    lanes = 128
    j = pl.program_id(1)
    tt = h_ref.shape[0]

    @pl.when(j == 0)
    def _():
        acc_ref[...] = jnp.zeros_like(acc_ref)

    i1_base = pl.multiple_of(j * rows_per_step, rows_per_step)
    group = 4
    sub = 64
    n_sub = nk // sub

    pre_ref[...] = _nt_dot(u_ref[...], h_ref[...])

    def gate_tile(ti, carry):
        cs = pl.ds(pl.multiple_of((ti // n_sub) * lanes, lanes), lanes)
        k0 = pl.multiple_of((ti % n_sub) * sub, sub)
        for r0 in range(0, rows_per_step, group):
            gates = [None] * group
            for h in range(PEER_HEADS):
                thr8 = thr_ref[h, pl.ds(i1_base, rows_per_step), cs]
                e18 = e1_ref[h, pl.ds(i1_base, rows_per_step), cs]
                s2 = s2_ref[h, pl.ds(k0, sub), cs]
                e2 = e2_ref[h, pl.ds(k0, sub), cs]
                for g in range(group):
                    r = r0 + g
                    w = jnp.where(s2 >= thr8[r:r + 1], e2, 0.0) * e18[r:r + 1]
                    gates[g] = w if gates[g] is None else gates[g] + w
            for g in range(group):
                rs = pl.ds(pl.multiple_of((r0 + g) * nk + k0, sub), sub)
                pre = pre_ref[rs, cs]
                act = 0.5 * pre * (1.0 + lax.erf(pre * (2.0 ** -0.5)))
                ga_ref[rs, cs] = (gates[g] * act).astype(ga_ref.dtype)
        return carry

    lax.fori_loop(0, (tt // lanes) * n_sub, gate_tile, 0)
    acc_ref[...] += jnp.dot(vt_ref[...], ga_ref[...], preferred_element_type=F32)

    @pl.when(j == pl.num_programs(1) - 1)
    def _():
        o_ref[...] = _rms(x_ref[...] + acc_ref[...].T, g_ref[...])


def _peer_main(hn, u, vt, thr, e1, s2, e2, x2d, g_final, tt, rows_per_step):
    T, D = x2d.shape
    tt = min(tt, T)
    nk = PEER_N_KEYS
    eb = rows_per_step * nk
    n_exp = u.shape[0]
    assert n_exp == nk * nk and nk % rows_per_step == 0
    score_spec = pl.BlockSpec((PEER_HEADS, nk, tt), lambda i, j: (0, 0, i))
    return pl.pallas_call(
        functools.partial(_peer_main_body, rows_per_step=rows_per_step),
        grid=(T // tt, n_exp // eb),
        in_specs=[pl.BlockSpec((tt, D), lambda i, j: (i, 0)),
                  pl.BlockSpec((eb, D), lambda i, j: (j, 0)),
                  pl.BlockSpec((D, eb), lambda i, j: (0, j)),
                  score_spec, score_spec, score_spec, score_spec,
                  pl.BlockSpec((tt, D), lambda i, j: (i, 0)),
                  pl.BlockSpec((1, D), lambda i, j: (0, 0))],
        out_specs=pl.BlockSpec((tt, D), lambda i, j: (i, 0)),
        out_shape=jax.ShapeDtypeStruct((T, D), F32),
        scratch_shapes=[pltpu.VMEM((D, tt), F32), pltpu.VMEM((eb, tt), F32), pltpu.VMEM((eb, tt), BF16)],
        compiler_params=_params("parallel", "arbitrary"),
        name="peer_main",
    )(hn, u, vt, thr, e1, s2, e2, x2d, g_final.reshape(1, D).astype(F32))


def _norm_cast_body(x_ref, g_ref, o_ref):
    o_ref[...] = _rms(x_ref[...], g_ref[...]).astype(o_ref.dtype)


def _norm_cast(x2d, g, tm):
    T, D = x2d.shape
    tm = min(tm, T)
    return pl.pallas_call(
        _norm_cast_body,
        grid=(T // tm,),
        in_specs=[pl.BlockSpec((tm, D), lambda i: (i, 0)), pl.BlockSpec((1, D), lambda i: (0, 0))],
        out_specs=pl.BlockSpec((tm, D), lambda i: (i, 0)),
        out_shape=jax.ShapeDtypeStruct((T, D), BF16),
        compiler_params=_params("parallel"),
        name="norm_cast",
    )(x2d, g.reshape(1, D).astype(F32))


def kernel(x, mem, rel_bias, g_mix, w_in, b_gate, w_branch_a, w_branch_b, w_out, g_cross, g_mem, w_q_cross,
           w_kv_cross, w_o_cross, g_ffn, w_q_peer, peer_sub_keys, peer_u, peer_v, g_final):
    B, S, D = x.shape
    T = B * S
    depth = w_in.shape[0]
    ndil = len(DIL_GROUPS) * DIL_HEADS
    x2d = x.reshape(T, D)

    dil_bias = [_bias_lookup(_dil_bucket_map(d), rel_bias[:, gi * DIL_HEADS:(gi + 1) * DIL_HEADS], f"dil_bias_g{gi}")
                for gi, (_, d) in enumerate(DIL_GROUPS)]
    mmap = _moba_bucket_map()
    moba_bias = _bias_lookup(mmap.reshape(-1, MOBA_BLOCK), rel_bias[:, ndil:], "moba_bias")
    moba_bias = moba_bias.reshape(MOBA_HEADS // 2, 2, MOBA_BIAS_TILES, MOBA_BLOCK, MOBA_BLOCK)

    assert depth == 1, "the final norm is fused into the last PEER sweep; one layer supported"
    l = 0
    qkv = _norm_matmul(x2d, g_mix[l], w_in[l][:, :QKV_WIDTH], out_dtype=BF16, tm=1024, tn=1280, name="proj_qkv")
    gates = _norm_matmul(x2d, g_mix[l], w_in[l][:, QKV_WIDTH:], out_dtype=BF16, tm=1024, tn=1024,
                         bias=b_gate[l], name="proj_gates")
    qkv3 = qkv.reshape(B, S, QKV_WIDTH)
    dil = [_dilated_group(qkv3, dil_bias[gi], gi, d) for gi, (_, d) in enumerate(DIL_GROUPS)]
    o_moba = _moba(qkv3, moba_bias).reshape(T, MOBA_HEADS * HEAD_DIM)
    o_dil = [o.reshape(T, -1) for o, _ in dil]
    lse_dil = [s.reshape(T, -1) for _, s in dil]
    x2d = _merge(o_dil, lse_dil, o_moba, gates, x2d, w_branch_a[l], w_branch_b[l], w_out[l], tm=512)
    M = mem.shape[1]
    q_c = _norm_matmul(x2d, g_cross[l], w_q_cross[l], out_dtype=BF16, tm=1024, tn=512, name="cross_q")
    kv = _norm_matmul(mem.reshape(B * M, D), g_mem[l], w_kv_cross[l], out_dtype=BF16, tm=1024, tn=1024, name="cross_kv")
    x2d = _cross(q_c.reshape(B, S, -1), kv.reshape(B, M, -1), x2d.reshape(B, S, D), w_o_cross[l], tm=1024).reshape(T, D)
    q_p = _norm_matmul(x2d, g_ffn[l], w_q_peer[l], out_dtype=F32, tm=1024, tn=1024, name="peer_q")
    hn = _norm_cast(x2d, g_ffn[l], tm=1024)
    thr, e1, s2, e2 = _peer_route(q_p, peer_sub_keys[l], tt=512)
    y = _peer_main(hn, peer_u[l].astype(BF16), peer_v[l].T.astype(BF16), thr, e1, s2, e2, x2d, g_final,
                   tt=512, rows_per_step=8)
    return y.reshape(B, S, D)
```

```python
import functools
import math

import jax
import jax.numpy as jnp
import numpy as np
from jax import lax
from jax.experimental import pallas as pl
from jax.experimental.pallas import tpu as pltpu

HEAD_DIM = 64
DIL_GROUPS = ((128, 1), (512, 4), (2048, 16))
DIL_HEADS = 4
DIL_BLOCK = 128
MOBA_HEADS = 8
MOBA_BLOCK = 256
MOBA_TOPK = 3
MOBA_BIAS_TILES = 8
MOBA_GROUP = 4
MOBA_MASK = -(2.0 ** 100)
REL_BUCKETS = 32
REL_MAX_DIST = 2048
CROSS_HEADS = 4
CROSS_HEAD_DIM = 128
PEER_HEADS = 8
PEER_N_KEYS = 128
PEER_TOPK = 16
RMS_EPS = 1e-6
NEG_INF = -1e30
DIL_GROUP_COLS = 3 * DIL_HEADS * HEAD_DIM
DIL_WIDTH = len(DIL_GROUPS) * DIL_GROUP_COLS
MOBA_WIDTH = 3 * MOBA_HEADS * HEAD_DIM
QKV_WIDTH = DIL_WIDTH + MOBA_WIDTH

LANES = 128
V7X_VMEM_BYTES = 64 * 1024 * 1024
VMEM_LIMIT = V7X_VMEM_BYTES * 3 // 4

BF16 = jnp.bfloat16
F32 = jnp.float32


def _params(*sem):
    return pltpu.CompilerParams(dimension_semantics=sem, vmem_limit_bytes=VMEM_LIMIT)


def _rms(x, g):
    return x * lax.rsqrt(jnp.mean(x * x, axis=-1, keepdims=True) + RMS_EPS) * g


def _nt_dot(a, b, precision=None):
    return lax.dot_general(a, b, (((1,), (1,)), ((), ())), preferred_element_type=F32, precision=precision)


def _norm_matmul_body(x_ref, g_ref, w_ref, *rest, sigmoid_bias, dil):
    rest = list(rest)
    b_ref = rest.pop(0) if sigmoid_bias else None
    o_ref, h_scr = rest[:2]
    rows = h_scr.shape[0] // dil

    @pl.when(pl.program_id(1) == 0)
    def _():
        if dil == 1:
            h_scr[...] = _rms(x_ref[...], g_ref[...]).astype(h_scr.dtype)
        else:
            cols_ref = rest[2]
            n_slab = cols_ref.shape[0]
            for c in range(n_slab):
                cols_ref[c] = x_ref[:, c * LANES:(c + 1) * LANES]
            for r in range(dil):
                x = jnp.concatenate([cols_ref[c, pl.ds(r, rows, stride=dil), :] for c in range(n_slab)], axis=-1)
                h_scr[r * rows:(r + 1) * rows, :] = _rms(x, g_ref[...]).astype(h_scr.dtype)

    acc = jnp.dot(h_scr[...], w_ref[...], preferred_element_type=F32)
    if sigmoid_bias:
        acc = jax.nn.sigmoid(acc + b_ref[...])
    if dil == 1:
        o_ref[...] = acc.astype(o_ref.dtype)
    else:
        for r in range(dil):
            o_ref[0, r] = acc[r * rows:(r + 1) * rows].astype(o_ref.dtype)


def _norm_matmul(x2d, g, w, *, out_dtype, tm, tn, bias=None, dil=1, seq=None, name):
    T, D = x2d.shape
    N = w.shape[1]
    tm = min(tm, T if seq is None else seq)
    assert T % tm == 0 and N % tn == 0
    in_specs = [
        pl.BlockSpec((tm, D), lambda i, j: (i, 0)),
        pl.BlockSpec((1, D), lambda i, j: (0, 0)),
        pl.BlockSpec((D, tn), lambda i, j: (0, j)),
    ]
    args = [x2d, g.reshape(1, D).astype(F32), w.astype(BF16)]
    if bias is not None:
        in_specs.append(pl.BlockSpec((1, tn), lambda i, j: (0, j)))
        args.append(bias.reshape(1, N).astype(F32))
    scratch = [pltpu.VMEM((tm, D), BF16)]
    if dil == 1:
        out_spec = pl.BlockSpec((tm, tn), lambda i, j: (i, j))
        out_shape = jax.ShapeDtypeStruct((T, N), out_dtype)
    else:
        assert seq % tm == 0 and tm % (8 * dil) == 0 and D % LANES == 0
        per_seq = seq // tm
        out_spec = pl.BlockSpec((1, dil, tm // dil, tn), lambda i, j: (i // per_seq, 0, i % per_seq, j))
        out_shape = jax.ShapeDtypeStruct((T // seq, dil, seq // dil, N), out_dtype)
        scratch.append(pltpu.VMEM((D // LANES, tm, LANES), F32))
    return pl.pallas_call(
        functools.partial(_norm_matmul_body, sigmoid_bias=bias is not None, dil=dil),
        grid=(T // tm, N // tn),
        in_specs=in_specs,
        out_specs=out_spec,
        out_shape=out_shape,
        scratch_shapes=scratch,
        compiler_params=_params("parallel", "arbitrary"),
        name=name,
    )(*args)


def _t5_bucket_np(n):
    max_exact = REL_BUCKETS // 2
    nf = np.maximum(n, 1).astype(np.float32)
    large = max_exact + (np.log(nf / np.float32(max_exact)) / np.float32(math.log(REL_MAX_DIST / max_exact))
                         * np.float32(REL_BUCKETS - max_exact)).astype(np.int32)
    large = np.minimum(large, REL_BUCKETS - 1)
    return np.where(n < max_exact, n, large).astype(np.int32)


def _dil_bucket_map(d):
    blk = DIL_BLOCK
    rel = (blk + np.arange(blk))[:, None] - np.arange(2 * blk)[None, :]
    return _t5_bucket_np(np.clip(rel, 0, None) * d)


def _moba_bucket_map():
    blk = MOBA_BLOCK
    rel = np.arange(blk)[:, None] - np.arange(blk)[None, :]
    tiles = [_t5_bucket_np(np.clip(rel + dlt * blk, 0, None)) for dlt in range(MOBA_BIAS_TILES)]
    tiles[0] = np.where(rel >= 0, tiles[0], -1)
    far = _t5_bucket_np(np.arange((MOBA_BIAS_TILES - 2) * blk + 1, 64 * blk))
    assert (far == REL_BUCKETS - 1).all()
    return np.stack(tiles)


def _bias_lookup_body(idx_ref, tab_ref, o_ref):
    h = pl.program_id(0)
    idx = idx_ref[...]

    def bucket(b, acc):
        return jnp.where(idx == b, tab_ref[b, h], acc)

    o_ref[0] = lax.fori_loop(0, REL_BUCKETS, bucket, jnp.full(idx.shape, NEG_INF, F32))


def _bias_lookup(bucket_map, table, name):
    R, C = bucket_map.shape
    H = table.shape[1]
    rt = min(R, 256)
    assert R % rt == 0
    return pl.pallas_call(
        _bias_lookup_body,
        grid=(H, R // rt),
        in_specs=[pl.BlockSpec((rt, C), lambda h, i: (i, 0)), pl.BlockSpec(memory_space=pltpu.SMEM)],
        out_specs=pl.BlockSpec((1, rt, C), lambda h, i: (h, i, 0)),
        out_shape=jax.ShapeDtypeStruct((H, R, C), F32),
        compiler_params=_params("parallel", "parallel"),
        name=name,
    )(jnp.asarray(bucket_map), table.astype(F32))


def _dilated_body(q_ref, kp_ref, kc_ref, vp_ref, vc_ref, bias_ref, o_ref, *, qblocks):
    blk, hd = DIL_BLOCK, HEAD_DIM
    gw = DIL_HEADS * hd
    n = pl.program_id(2)
    k = jnp.concatenate([kp_ref[0, 0], kc_ref[0, 0]], axis=0)
    v = jnp.concatenate([vp_ref[0, 0], vc_ref[0, 0]], axis=0)
    lane = lax.broadcasted_iota(jnp.int32, (1, gw), 1)
    qi = lax.broadcasted_iota(jnp.int32, (blk, 2 * blk), 0)
    kj = lax.broadcasted_iota(jnp.int32, (blk, 2 * blk), 1)
    rel = blk + qi - kj
    band = (rel >= 0) & (rel <= blk)
    first_key = jnp.where(n > 0, 0, blk)
    for i in range(qblocks):
        rows = slice(i * blk, (i + 1) * blk)
        q = q_ref[0, 0, rows, :]
        kw = k[i * blk:(i + 2) * blk]
        vw = v[i * blk:(i + 2) * blk]
        mask = band & (kj >= first_key) if i == 0 else band
        out = jnp.zeros((blk, gw), F32)
        lse = jnp.zeros((blk, gw), F32)
        for h in range(DIL_HEADS):
            head_lanes = (lane >= h * hd) & (lane < (h + 1) * hd)
            qh = jnp.where(head_lanes, q, jnp.zeros_like(q))
            s = _nt_dot(qh, kw) * (hd ** -0.5) + bias_ref[h]
            s = jnp.where(mask, s, NEG_INF)
            m = jnp.max(s, axis=-1, keepdims=True)
            p = jnp.exp(s - m)
            l = jnp.sum(p, axis=-1, keepdims=True)
            pv = jnp.dot(p.astype(BF16), vw, preferred_element_type=F32)
            out = jnp.where(head_lanes, pv / l, out)
            lse = jnp.where(head_lanes, m + jnp.log(l), lse)
        o_ref[0, 0, rows, :gw] = out
        o_ref[0, 0, rows, gw:] = lse


def _dilated_group(qkv, bias, name):
    B, d, L, _ = qkv.shape
    blk, gw = DIL_BLOCK, DIL_HEADS * HEAD_DIM
    assert L % blk == 0
    qblocks = min(4, L // blk)
    assert L % (qblocks * blk) == 0
    rows = qblocks * blk

    def spec(j, prev):
        if prev:
            return pl.BlockSpec((1, 1, blk, gw), lambda b, r, n: (b, r, jnp.maximum(n * qblocks - 1, 0), j))
        return pl.BlockSpec((1, 1, rows, gw), lambda b, r, n: (b, r, n, j))

    return pl.pallas_call(
        functools.partial(_dilated_body, qblocks=qblocks),
        grid=(B, d, L // rows),
        in_specs=[spec(0, False), spec(1, True), spec(1, False), spec(2, True), spec(2, False),
                  pl.BlockSpec((DIL_HEADS, blk, 2 * blk), lambda b, r, n: (0, 0, 0))],
        out_specs=pl.BlockSpec((1, 1, rows, 2 * gw), lambda b, r, n: (b, r, n, 0)),
        out_shape=jax.ShapeDtypeStruct((B, d, L, 2 * gw), F32),
        compiler_params=_params("parallel", "parallel", "arbitrary"),
        name=name,
    )(qkv, qkv, qkv, qkv, qkv, bias)


def _moba_body(q_ref, k_ref, v_ref, bias_ref, o_ref, qa_ref, ka_ref, *, S):
    blk, hd = MOBA_BLOCK, HEAD_DIM
    nblk = S // blk
    scale = hd ** -0.5
    lane = lax.broadcasted_iota(jnp.int32, (1, 2 * hd), 1)
    kmean = jnp.mean(k_ref[0].astype(F32).reshape(nblk, blk, 2 * hd), axis=1)
    grp = MOBA_GROUP * blk
    row_blk = lax.broadcasted_iota(jnp.int32, (S, 2 * hd), 0) // blk
    lane_f = lane.astype(F32)
    q2, k2 = q_ref[0], k_ref[0]

    for h in range(2):
        head_lanes = (lane >= h * hd) & (lane < (h + 1) * hd)
        off = hd * (1 - h)
        blk_id = lane - off
        in_range = (blk_id >= 0) & (blk_id < nblk)
        km = jnp.where(head_lanes, kmean, 0.0)
        pieces = ([jnp.zeros((off, 2 * hd), F32)] if off else []) + [km, jnp.zeros((2 * hd - off - nblk, 2 * hd), F32)]
        km_rows = jnp.concatenate(pieces, axis=0)
        qh = jnp.where(head_lanes, q2, jnp.zeros_like(q2))
        gs = _nt_dot(qh.astype(F32), km_rows, precision=lax.Precision.HIGHEST)
        past = in_range & (blk_id < row_blk)
        gs = jnp.where(past, gs, jnp.where(in_range, NEG_INF, -jnp.inf))
        sel = jnp.zeros(gs.shape, F32)
        for _r in range(MOBA_TOPK):
            gmax = jnp.max(gs, axis=-1, keepdims=True)
            first = jnp.min(jnp.where(gs == gmax, lane_f, 4.0 * hd), axis=-1, keepdims=True)
            pick = lane_f == first
            sel = jnp.where(pick & past, 1.0, sel)
            gs = jnp.where(pick, -jnp.inf, gs)
        own = in_range & (blk_id == row_blk)
        penalty = jnp.where((sel > 0.5) | own | jnp.logical_not(in_range), 0.0, MOBA_MASK)
        qa_ref[h] = jnp.where(head_lanes, q2 * scale, penalty.astype(BF16))
        ka_ref[h] = jnp.where(head_lanes, k2, jnp.where(own, 1.0, 0.0).astype(BF16))

    def q_block(qb, _):
        row0 = pl.multiple_of(qb * blk, blk)
        n_grp = qb // MOBA_GROUP + 1
        qa = [qa_ref[h, pl.ds(row0, blk), :] for h in range(2)]

        def key_group(i, carry):
            g = n_grp - 1 - i
            col0 = pl.multiple_of(g * grp, grp)
            vg = v_ref[0, pl.ds(col0, grp), :]
            out = []
            for h in range(2):
                m, l, acc = carry[h]
                s = _nt_dot(qa[h], ka_ref[h, pl.ds(col0, grp), :])
                parts = []
                for jj in range(MOBA_GROUP):
                    tile = jnp.clip(qb - (g * MOBA_GROUP + jj), 0, MOBA_BIAS_TILES - 1)
                    parts.append(s[:, jj * blk:(jj + 1) * blk] + bias_ref[0, h, tile])
                m_new = jnp.maximum(m, jnp.max(functools.reduce(jnp.maximum, parts), axis=-1, keepdims=True))
                a = jnp.exp(m - m_new)
                ps = [jnp.exp(p - m_new) for p in parts]
                l = l * a + jnp.sum(functools.reduce(jnp.add, ps), axis=-1, keepdims=True)
                p_all = jnp.concatenate([p.astype(BF16) for p in ps], axis=-1)
                acc = acc * a + jnp.dot(p_all, vg, preferred_element_type=F32)
                out.append((m_new, l, acc))
            return tuple(out)

        init = tuple((jnp.full((blk, 1), -jnp.inf, F32), jnp.zeros((blk, 1), F32), jnp.zeros((blk, 2 * hd), F32))
                     for _ in range(2))
        res = lax.fori_loop(0, n_grp, key_group, init)
        o = jnp.where(lane < hd, res[0][2] / res[0][1], res[1][2] / res[1][1])
        o_ref[0, pl.ds(row0, blk), :] = o.astype(o_ref.dtype)
        return 0

    lax.fori_loop(0, nblk, q_block, 0)


def _moba(qkv, bias, first_col):
    B, S, C = qkv.shape
    pw = 2 * HEAD_DIM
    npair = MOBA_HEADS // 2
    assert S % (MOBA_GROUP * MOBA_BLOCK) == 0 and S // MOBA_BLOCK <= HEAD_DIM and first_col % pw == 0
    c0 = first_col // pw

    def spec(j):
        return pl.BlockSpec((1, S, pw), lambda hp, b: (b, 0, c0 + j * npair + hp))

    return pl.pallas_call(
        functools.partial(_moba_body, S=S),
        grid=(npair, B),
        in_specs=[spec(0), spec(1), spec(2),
                  pl.BlockSpec((1, 2, MOBA_BIAS_TILES, MOBA_BLOCK, MOBA_BLOCK), lambda hp, b: (hp, 0, 0, 0, 0))],
        out_specs=pl.BlockSpec((1, S, pw), lambda hp, b: (b, 0, hp)),
        out_shape=jax.ShapeDtypeStruct((B, S, MOBA_HEADS * HEAD_DIM), BF16),
        scratch_shapes=[pltpu.VMEM((2, S, pw), BF16), pltpu.VMEM((2, S, pw), BF16)],
        compiler_params=_params("parallel", "parallel"),
        name="moba",
    )(qkv, qkv, qkv, bias)


def _merge_body(d0_ref, d1_ref, d2_ref, om_ref, gate_ref, x_ref, wa_ref, wb_ref, wo_ref, x1_ref, t1_ref, t2_ref):
    D = x_ref.shape[-1]
    gw = DIL_HEADS * HEAD_DIM
    n_slab = t1_ref.shape[0]
    for src, dst in ((d1_ref, t1_ref), (d2_ref, t2_ref)):
        dil, rows = src.shape[1], src.shape[2]
        for r in range(dil):
            for c in range(n_slab):
                dst[c, pl.ds(r, rows, stride=dil), :] = src[0, r, :, c * LANES:(c + 1) * LANES]
    g0 = d0_ref[...]
    g1 = jnp.concatenate([t1_ref[c] for c in range(n_slab)], axis=-1)
    g2 = jnp.concatenate([t2_ref[c] for c in range(n_slab)], axis=-1)
    l0, l1, l2 = g0[:, gw:], g1[:, gw:], g2[:, gw:]
    mx = jnp.maximum(jnp.maximum(l0, l1), l2)
    e0, e1, e2 = jnp.exp(l0 - mx), jnp.exp(l1 - mx), jnp.exp(l2 - mx)
    o_dil = (e0 * g0[:, :gw] + e1 * g1[:, :gw] + e2 * g2[:, :gw]) / (e0 + e1 + e2)
    a = jnp.dot(o_dil.astype(BF16), wa_ref[...], preferred_element_type=F32)
    b = jnp.dot(om_ref[...], wb_ref[...], preferred_element_type=F32)
    merged = gate_ref[:, :D].astype(F32) * a + gate_ref[:, D:].astype(F32) * b
    x1_ref[...] = x_ref[...] + jnp.dot(merged.astype(BF16), wo_ref[...], preferred_element_type=F32)


def _merge(dil_out, o_moba, gates, x2d, w_a, w_b, w_o, seq, tm):
    T, D = x2d.shape
    tm = min(tm, seq)
    per_seq = seq // tm
    assert seq % tm == 0 and dil_out[0].shape[1] == 1

    def rows(width):
        return pl.BlockSpec((tm, width), lambda i: (i, 0))

    def whole(w):
        return pl.BlockSpec(w.shape, lambda i: (0, 0))

    def residue_major(a):
        d, width = a.shape[1], a.shape[3]
        assert tm % (8 * d) == 0
        return pl.BlockSpec((1, d, tm // d, width), lambda i: (i // per_seq, 0, i % per_seq, 0))

    ws = [w_a.astype(BF16), w_b.astype(BF16), w_o.astype(BF16)]
    width = dil_out[0].shape[-1]
    return pl.pallas_call(
        _merge_body,
        grid=(T // tm,),
        in_specs=[rows(width), residue_major(dil_out[1]), residue_major(dil_out[2]),
                  rows(o_moba.shape[-1]), rows(2 * D), rows(D)] + [whole(w) for w in ws],
        out_specs=rows(D),
        out_shape=jax.ShapeDtypeStruct((T, D), F32),
        scratch_shapes=[pltpu.VMEM((width // LANES, tm, LANES), F32)] * 2,
        compiler_params=_params("parallel"),
        name="merge",
    )(dil_out[0].reshape(T, width), dil_out[1], dil_out[2], o_moba, gates, x2d, *ws)


def _cross_body(q_ref, kv_ref, x_ref, wo_ref, x2_ref):
    hd = CROSS_HEAD_DIM
    width = CROSS_HEADS * hd
    heads = []
    for h in range(CROSS_HEADS):
        sl = slice(h * hd, (h + 1) * hd)
        s = _nt_dot(q_ref[0, :, sl], kv_ref[0, :, sl]) * (hd ** -0.5)
        m = jnp.max(s, axis=-1, keepdims=True)
        p = jnp.exp(s - m)
        l = jnp.sum(p, axis=-1, keepdims=True)
        v = kv_ref[0, :, width + h * hd: width + (h + 1) * hd]
        heads.append((jnp.dot(p.astype(BF16), v, preferred_element_type=F32) / l).astype(BF16))
    o = jnp.concatenate(heads, axis=-1)
    x2_ref[0] = x_ref[0] + jnp.dot(o, wo_ref[...], preferred_element_type=F32)


def _cross(q, kv, x, w_o, tm):
    B, S, D = x.shape
    M = kv.shape[1]
    tm = min(tm, S)
    w_o = w_o.astype(BF16)
    return pl.pallas_call(
        _cross_body,
        grid=(B, S // tm),
        in_specs=[pl.BlockSpec((1, tm, q.shape[-1]), lambda b, i: (b, i, 0)),
                  pl.BlockSpec((1, M, kv.shape[-1]), lambda b, i: (b, 0, 0)),
                  pl.BlockSpec((1, tm, D), lambda b, i: (b, i, 0)),
                  pl.BlockSpec(w_o.shape, lambda b, i: (0, 0))],
        out_specs=pl.BlockSpec((1, tm, D), lambda b, i: (b, i, 0)),
        out_shape=jax.ShapeDtypeStruct((B, S, D), F32),
        compiler_params=_params("parallel", "parallel"),
        name="cross",
    )(q, kv, x, w_o)


def _top_sorted(s, k):
    rows = []
    for _ in range(k):
        m = jnp.max(s, axis=0, keepdims=True)
        rows.append(m)
        s = jnp.where(s == m, -jnp.inf, s)
    return jnp.concatenate(rows, axis=0)


def _pair_sum_candidates(a, b):
    assert PEER_TOPK == 16
    row = lax.broadcasted_iota(jnp.int32, (PEER_TOPK, 1), 0)
    blocks = [a[0:1] + b]
    blocks += [a[i:i + 1] + b[0:8] for i in (1, 2, 3)]
    blocks += [jnp.where(row >= 4, b[0:1] + a, -jnp.inf)]
    blocks += [jnp.where(row[0:8] >= 4, b[j:j + 1] + a[0:8], -jnp.inf) for j in (1, 2)]
    return jnp.concatenate(blocks, axis=0)


def _peer_route_body(q_ref, keys_ref, thr_ref, e1_ref, s2_ref, e2_ref):
    nk, K = PEER_N_KEYS, PEER_TOPK
    kd = keys_ref.shape[-1]
    for h in range(PEER_HEADS):
        sc = []
        for p in range(2):
            c0 = (2 * h + p) * kd
            sc.append(_nt_dot(keys_ref[p], q_ref[:, c0:c0 + kd], precision=lax.Precision.HIGHEST))
        a = _top_sorted(sc[0], K + 1)
        b = _top_sorted(sc[1], K + 1)
        cand = _pair_sum_candidates(a[:K], b[:K])
        cur, tau = cand, cand[0:1]
        count = jnp.zeros_like(tau)
        for _ in range(K):
            m = jnp.max(cur, axis=0, keepdims=True)
            eq = cur == m
            tau = jnp.where(count < K, m, tau)
            count = count + jnp.sum(jnp.where(eq, 1.0, 0.0), axis=0, keepdims=True)
            cur = jnp.where(eq, -jnp.inf, cur)
        below = jnp.max(jnp.where(cand < tau, cand, -jnp.inf), axis=0, keepdims=True)
        for corner in (a[0:1] + b[K:K + 1], a[K:K + 1] + b[0:1]):
            below = jnp.maximum(below, jnp.where(corner < tau, corner, -jnp.inf))
        below = jnp.where(below == -jnp.inf, tau, below)
        cut = 0.5 * (tau + below)
        top = a[0:1] + b[0:1]
        z = jnp.sum(jnp.where(cand >= tau, jnp.exp(cand - top), 0.0), axis=0, keepdims=True)
        thr_ref[h] = cut - sc[0]
        e1_ref[h] = jnp.exp(sc[0] - a[0:1]) / z
        s2_ref[h] = sc[1]
        e2_ref[h] = jnp.exp(sc[1] - b[0:1])


def _peer_route(q, sub_keys, tt):
    T = q.shape[0]
    tt = min(tt, T)
    nk = PEER_N_KEYS
    big = jax.ShapeDtypeStruct((PEER_HEADS, nk, T), F32)
    big_spec = pl.BlockSpec((PEER_HEADS, nk, tt), lambda i: (0, 0, i))
    return pl.pallas_call(
        _peer_route_body,
        grid=(T // tt,),
        in_specs=[pl.BlockSpec((tt, q.shape[1]), lambda i: (i, 0)),
                  pl.BlockSpec(sub_keys.shape, lambda i: (0, 0, 0))],
        out_specs=[big_spec] * 4,
        out_shape=[big] * 4,
        compiler_params=_params("parallel"),
        name="peer_route",
    )(q, sub_keys.astype(F32))


def _peer_main_body(h_ref, u_ref, vt_ref, thr_ref, e1_ref, s2_ref, e2_ref, x_ref, g_ref,
                    o_ref, acc_ref, pre_ref, ga_ref, *, rows_per_step):
    nk = PEER_N_KEYS
    lanes = 128
    j = pl.program_id(1)
    tt = h_ref.shape[0]

    @pl.when(j == 0)
    def _():
        acc_ref[...] = jnp.zeros_like(acc_ref)

    i1_base = pl.multiple_of(j * rows_per_step, rows_per_step)
    group = 4
    sub = 64
    n_sub = nk // sub

    pre_ref[...] = _nt_dot(u_ref[...], h_ref[...])

    def gate_tile(ti, carry):
        cs = pl.ds(pl.multiple_of((ti // n_sub) * lanes, lanes), lanes)
        k0 = pl.multiple_of((ti % n_sub) * sub, sub)
        for r0 in range(0, rows_per_step, group):
            gates = [None] * group
            for h in range(PEER_HEADS):
                thr8 = thr_ref[h, pl.ds(i1_base, rows_per_step), cs]
                e18 = e1_ref[h, pl.ds(i1_base, rows_per_step), cs]
                s2 = s2_ref[h, pl.ds(k0, sub), cs]
                e2 = e2_ref[h, pl.ds(k0, sub), cs]
                for g in range(group):
                    r = r0 + g
                    w = jnp.where(s2 >= thr8[r:r + 1], e2, 0.0) * e18[r:r + 1]
                    gates[g] = w if gates[g] is None else gates[g] + w
            for g in range(group):
                rs = pl.ds(pl.multiple_of((r0 + g) * nk + k0, sub), sub)
                pre = pre_ref[rs, cs]
                act = 0.5 * pre * (1.0 + lax.erf(pre * (2.0 ** -0.5)))
                ga_ref[rs, cs] = (gates[g] * act).astype(ga_ref.dtype)
        return carry

    lax.fori_loop(0, (tt // lanes) * n_sub, gate_tile, 0)
    acc_ref[...] += jnp.dot(vt_ref[...], ga_ref[...], preferred_element_type=F32)

    @pl.when(j == pl.num_programs(1) - 1)
    def _():
        o_ref[...] = _rms(x_ref[...] + acc_ref[...].T, g_ref[...])


def _peer_main(hn, u, vt, thr, e1, s2, e2, x2d, g_final, tt, rows_per_step):
    T, D = x2d.shape
    tt = min(tt, T)
    nk = PEER_N_KEYS
    eb = rows_per_step * nk
    n_exp = u.shape[0]
    assert n_exp == nk * nk and nk % rows_per_step == 0
    score_spec = pl.BlockSpec((PEER_HEADS, nk, tt), lambda i, j: (0, 0, i))
    return pl.pallas_call(
        functools.partial(_peer_main_body, rows_per_step=rows_per_step),
        grid=(T // tt, n_exp // eb),
        in_specs=[pl.BlockSpec((tt, D), lambda i, j: (i, 0)),
                  pl.BlockSpec((eb, D), lambda i, j: (j, 0)),
                  pl.BlockSpec((D, eb), lambda i, j: (0, j)),
                  score_spec, score_spec, score_spec, score_spec,
                  pl.BlockSpec((tt, D), lambda i, j: (i, 0)),
                  pl.BlockSpec((1, D), lambda i, j: (0, 0))],
        out_specs=pl.BlockSpec((tt, D), lambda i, j: (i, 0)),
        out_shape=jax.ShapeDtypeStruct((T, D), F32),
        scratch_shapes=[pltpu.VMEM((D, tt), F32), pltpu.VMEM((eb, tt), F32), pltpu.VMEM((eb, tt), BF16)],
        compiler_params=_params("parallel", "arbitrary"),
        name="peer_main",
    )(hn, u, vt, thr, e1, s2, e2, x2d, g_final.reshape(1, D).astype(F32))


def _norm_cast_body(x_ref, g_ref, o_ref):
    o_ref[...] = _rms(x_ref[...], g_ref[...]).astype(o_ref.dtype)


def _norm_cast(x2d, g, tm):
    T, D = x2d.shape
    tm = min(tm, T)
    return pl.pallas_call(
        _norm_cast_body,
        grid=(T // tm,),
        in_specs=[pl.BlockSpec((tm, D), lambda i: (i, 0)), pl.BlockSpec((1, D), lambda i: (0, 0))],
        out_specs=pl.BlockSpec((tm, D), lambda i: (i, 0)),
        out_shape=jax.ShapeDtypeStruct((T, D), BF16),
        compiler_params=_params("parallel"),
        name="norm_cast",
    )(x2d, g.reshape(1, D).astype(F32))


def kernel(x, mem, rel_bias, g_mix, w_in, b_gate, w_branch_a, w_branch_b, w_out, g_cross, g_mem, w_q_cross,
           w_kv_cross, w_o_cross, g_ffn, w_q_peer, peer_sub_keys, peer_u, peer_v, g_final):
    B, S, D = x.shape
    T = B * S
    depth = w_in.shape[0]
    ndil = len(DIL_GROUPS) * DIL_HEADS
    x2d = x.reshape(T, D)

    dil_bias = [_bias_lookup(_dil_bucket_map(d), rel_bias[:, gi * DIL_HEADS:(gi + 1) * DIL_HEADS], f"dil_bias_g{gi}")
                for gi, (_, d) in enumerate(DIL_GROUPS)]
    mmap = _moba_bucket_map()
    moba_bias = _bias_lookup(mmap.reshape(-1, MOBA_BLOCK), rel_bias[:, ndil:], "moba_bias")
    moba_bias = moba_bias.reshape(MOBA_HEADS // 2, 2, MOBA_BIAS_TILES, MOBA_BLOCK, MOBA_BLOCK)

    assert depth == 1, "the final norm is fused into the last PEER sweep; one layer supported"
    l = 0
    gc = DIL_GROUP_COLS
    assert DIL_GROUPS[0][1] == 1
    w_tok = jnp.concatenate([w_in[l][:, :gc], w_in[l][:, DIL_WIDTH:QKV_WIDTH]], axis=1)
    tok = _norm_matmul(x2d, g_mix[l], w_tok, out_dtype=BF16, tm=1024, tn=gc, name="proj_tok")
    dil_qkv = [tok.reshape(B, 1, S, -1)]
    for gi, (_, d) in enumerate(DIL_GROUPS[1:], start=1):
        dil_qkv.append(_norm_matmul(x2d, g_mix[l], w_in[l][:, gi * gc:(gi + 1) * gc], out_dtype=BF16, tm=1024, tn=gc,
                                    dil=d, seq=S, name=f"proj_dil_g{gi}"))
    gates = _norm_matmul(x2d, g_mix[l], w_in[l][:, QKV_WIDTH:], out_dtype=BF16, tm=1024, tn=1024,
                         bias=b_gate[l], name="proj_gates")
    dil_out = [_dilated_group(a, dil_bias[gi], f"dilated_g{gi}") for gi, a in enumerate(dil_qkv)]
    o_moba = _moba(tok.reshape(B, S, -1), moba_bias, first_col=gc).reshape(T, MOBA_HEADS * HEAD_DIM)
    x2d = _merge(dil_out, o_moba, gates, x2d, w_branch_a[l], w_branch_b[l], w_out[l], seq=S, tm=512)
    M = mem.shape[1]
    q_c = _norm_matmul(x2d, g_cross[l], w_q_cross[l], out_dtype=BF16, tm=1024, tn=512, name="cross_q")
    kv = _norm_matmul(mem.reshape(B * M, D), g_mem[l], w_kv_cross[l], out_dtype=BF16, tm=1024, tn=1024, name="cross_kv")
    x2d = _cross(q_c.reshape(B, S, -1), kv.reshape(B, M, -1), x2d.reshape(B, S, D), w_o_cross[l], tm=1024).reshape(T, D)
    q_p = _norm_matmul(x2d, g_ffn[l], w_q_peer[l], out_dtype=F32, tm=1024, tn=1024, name="peer_q")
    hn = _norm_cast(x2d, g_ffn[l], tm=1024)
    thr, e1, s2, e2 = _peer_route(q_p, peer_sub_keys[l], tt=512)
    y = _peer_main(hn, peer_u[l].astype(BF16), peer_v[l].T.astype(BF16), thr, e1, s2, e2, x2d, g_final,
                   tt=512, rows_per_step=8)
    return y.reshape(B, S, D)
```

```python
import functools
import math

import jax
import jax.numpy as jnp
import numpy as np
from jax import lax
from jax.experimental import pallas as pl
from jax.experimental.pallas import tpu as pltpu

HEAD_DIM = 64
DIL_GROUPS = ((128, 1), (512, 4), (2048, 16))
DIL_HEADS = 4
DIL_BLOCK = 128
MOBA_HEADS = 8
MOBA_BLOCK = 256
MOBA_TOPK = 3
MOBA_BIAS_TILES = 8
MOBA_GROUP = 4
MOBA_MASK = -(2.0 ** 100)
REL_BUCKETS = 32
REL_MAX_DIST = 2048
CROSS_HEADS = 4
CROSS_HEAD_DIM = 128
PEER_HEADS = 8
PEER_N_KEYS = 128
PEER_TOPK = 16
RMS_EPS = 1e-6
NEG_INF = -1e30
DIL_GROUP_COLS = 3 * DIL_HEADS * HEAD_DIM
DIL_WIDTH = len(DIL_GROUPS) * DIL_GROUP_COLS
MOBA_WIDTH = 3 * MOBA_HEADS * HEAD_DIM
QKV_WIDTH = DIL_WIDTH + MOBA_WIDTH

LANES = 128
V7X_VMEM_BYTES = 64 * 1024 * 1024
VMEM_LIMIT = V7X_VMEM_BYTES * 3 // 4

BF16 = jnp.bfloat16
F32 = jnp.float32


def _params(*sem):
    return pltpu.CompilerParams(dimension_semantics=sem, vmem_limit_bytes=VMEM_LIMIT)


def _rms(x, g):
    return x * lax.rsqrt(jnp.mean(x * x, axis=-1, keepdims=True) + RMS_EPS) * g


def _nt_dot(a, b, precision=None):
    return lax.dot_general(a, b, (((1,), (1,)), ((), ())), preferred_element_type=F32, precision=precision)


def _norm_matmul_body(x_ref, g_ref, w_ref, *rest, sigmoid_bias, dil):
    rest = list(rest)
    b_ref = rest.pop(0) if sigmoid_bias else None
    o_ref, h_scr = rest[:2]
    rows = h_scr.shape[0] // dil

    @pl.when(pl.program_id(1) == 0)
    def _():
        if dil == 1:
            h_scr[...] = _rms(x_ref[...], g_ref[...]).astype(h_scr.dtype)
        else:
            cols_ref = rest[2]
            n_slab = cols_ref.shape[0]
            for c in range(n_slab):
                cols_ref[c] = x_ref[:, c * LANES:(c + 1) * LANES]
            for r in range(dil):
                x = jnp.concatenate([cols_ref[c, pl.ds(r, rows, stride=dil), :] for c in range(n_slab)], axis=-1)
                h_scr[r * rows:(r + 1) * rows, :] = _rms(x, g_ref[...]).astype(h_scr.dtype)

    acc = jnp.dot(h_scr[...], w_ref[...], preferred_element_type=F32)
    if sigmoid_bias:
        acc = jax.nn.sigmoid(acc + b_ref[...])
    if dil == 1:
        o_ref[...] = acc.astype(o_ref.dtype)
    else:
        for r in range(dil):
            o_ref[0, r] = acc[r * rows:(r + 1) * rows].astype(o_ref.dtype)


def _norm_matmul(x2d, g, w, *, out_dtype, tm, tn, bias=None, dil=1, seq=None, emit_normed=False, name):
    T, D = x2d.shape
    N = w.shape[1]
    tm = min(tm, T if seq is None else seq)
    assert T % tm == 0 and N % tn == 0
    in_specs = [
        pl.BlockSpec((tm, D), lambda i, j: (i, 0)),
        pl.BlockSpec((1, D), lambda i, j: (0, 0)),
        pl.BlockSpec((D, tn), lambda i, j: (0, j)),
    ]
    args = [x2d, g.reshape(1, D).astype(F32), w.astype(BF16)]
    if bias is not None:
        in_specs.append(pl.BlockSpec((1, tn), lambda i, j: (0, j)))
        args.append(bias.reshape(1, N).astype(F32))
    assert not (emit_normed and dil > 1)
    scratch = [] if emit_normed else [pltpu.VMEM((tm, D), BF16)]
    if dil == 1:
        out_spec = pl.BlockSpec((tm, tn), lambda i, j: (i, j))
        out_shape = jax.ShapeDtypeStruct((T, N), out_dtype)
    else:
        assert seq % tm == 0 and tm % (8 * dil) == 0 and D % LANES == 0
        per_seq = seq // tm
        out_spec = pl.BlockSpec((1, dil, tm // dil, tn), lambda i, j: (i // per_seq, 0, i % per_seq, j))
        out_shape = jax.ShapeDtypeStruct((T // seq, dil, seq // dil, N), out_dtype)
        scratch.append(pltpu.VMEM((D // LANES, tm, LANES), F32))
    if emit_normed:
        out_spec = [out_spec, pl.BlockSpec((tm, D), lambda i, j: (i, 0))]
        out_shape = [out_shape, jax.ShapeDtypeStruct((T, D), BF16)]
    return pl.pallas_call(
        functools.partial(_norm_matmul_body, sigmoid_bias=bias is not None, dil=dil),
        grid=(T // tm, N // tn),
        in_specs=in_specs,
        out_specs=out_spec,
        out_shape=out_shape,
        scratch_shapes=scratch,
        compiler_params=_params("parallel", "arbitrary"),
        name=name,
    )(*args)


def _t5_bucket_np(n):
    max_exact = REL_BUCKETS // 2
    nf = np.maximum(n, 1).astype(np.float32)
    large = max_exact + (np.log(nf / np.float32(max_exact)) / np.float32(math.log(REL_MAX_DIST / max_exact))
                         * np.float32(REL_BUCKETS - max_exact)).astype(np.int32)
    large = np.minimum(large, REL_BUCKETS - 1)
    return np.where(n < max_exact, n, large).astype(np.int32)


def _dil_bucket_map(d):
    blk = DIL_BLOCK
    rel = (blk + np.arange(blk))[:, None] - np.arange(2 * blk)[None, :]
    return _t5_bucket_np(np.clip(rel, 0, None) * d)


def _moba_bucket_map():
    blk = MOBA_BLOCK
    rel = np.arange(blk)[:, None] - np.arange(blk)[None, :]
    tiles = [_t5_bucket_np(np.clip(rel + dlt * blk, 0, None)) for dlt in range(MOBA_BIAS_TILES)]
    tiles[0] = np.where(rel >= 0, tiles[0], -1)
    far = _t5_bucket_np(np.arange((MOBA_BIAS_TILES - 2) * blk + 1, 64 * blk))
    assert (far == REL_BUCKETS - 1).all()
    return np.stack(tiles)


def _bias_lookup_body(idx_ref, tab_ref, o_ref):
    h = pl.program_id(0)
    idx = idx_ref[...]

    def bucket(b, acc):
        return jnp.where(idx == b, tab_ref[b, h], acc)

    o_ref[0] = lax.fori_loop(0, REL_BUCKETS, bucket, jnp.full(idx.shape, NEG_INF, F32))


def _bias_lookup(bucket_map, table, name):
    R, C = bucket_map.shape
    H = table.shape[1]
    rt = min(R, 256)
    assert R % rt == 0
    return pl.pallas_call(
        _bias_lookup_body,
        grid=(H, R // rt),
        in_specs=[pl.BlockSpec((rt, C), lambda h, i: (i, 0)), pl.BlockSpec(memory_space=pltpu.SMEM)],
        out_specs=pl.BlockSpec((1, rt, C), lambda h, i: (h, i, 0)),
        out_shape=jax.ShapeDtypeStruct((H, R, C), F32),
        compiler_params=_params("parallel", "parallel"),
        name=name,
    )(jnp.asarray(bucket_map), table.astype(F32))


def _dilated_body(q_ref, kp_ref, kc_ref, vp_ref, vc_ref, bias_ref, o_ref, *, qblocks):
    blk, hd = DIL_BLOCK, HEAD_DIM
    gw = DIL_HEADS * hd
    n = pl.program_id(2)
    k = jnp.concatenate([kp_ref[0, 0], kc_ref[0, 0]], axis=0)
    v = jnp.concatenate([vp_ref[0, 0], vc_ref[0, 0]], axis=0)
    lane = lax.broadcasted_iota(jnp.int32, (1, gw), 1)
    qi = lax.broadcasted_iota(jnp.int32, (blk, 2 * blk), 0)
    kj = lax.broadcasted_iota(jnp.int32, (blk, 2 * blk), 1)
    rel = blk + qi - kj
    band = (rel >= 0) & (rel <= blk)
    first_key = jnp.where(n > 0, 0, blk)
    for i in range(qblocks):
        rows = slice(i * blk, (i + 1) * blk)
        q = q_ref[0, 0, rows, :]
        kw = k[i * blk:(i + 2) * blk]
        vw = v[i * blk:(i + 2) * blk]
        mask = band & (kj >= first_key) if i == 0 else band
        head_lanes = [(lane >= h * hd) & (lane < (h + 1) * hd) for h in range(DIL_HEADS)]
        q_all = jnp.concatenate([jnp.where(hl, q, jnp.zeros_like(q)) for hl in head_lanes], axis=0)
        s = _nt_dot(q_all, kw) * (hd ** -0.5) + bias_ref[...]
        s = jnp.where(jnp.concatenate([mask] * DIL_HEADS, axis=0), s, NEG_INF)
        m = jnp.max(s, axis=-1, keepdims=True)
        p = jnp.exp(s - m)
        l = jnp.sum(p, axis=-1, keepdims=True)
        pv = jnp.dot(p.astype(BF16), vw, preferred_element_type=F32) / l
        lse_all = m + jnp.log(l)
        out = jnp.zeros((blk, gw), F32)
        lse = jnp.zeros((blk, gw), F32)
        for h, hl in enumerate(head_lanes):
            out = jnp.where(hl, pv[h * blk:(h + 1) * blk], out)
            lse = jnp.where(hl, lse_all[h * blk:(h + 1) * blk], lse)
        o_ref[0, 0, rows, :gw] = out
        o_ref[0, 0, rows, gw:] = lse


def _dilated_group(qkv, bias, name):
    B, d, L, _ = qkv.shape
    blk, gw = DIL_BLOCK, DIL_HEADS * HEAD_DIM
    assert L % blk == 0
    qblocks = min(4, L // blk)
    assert L % (qblocks * blk) == 0
    rows = qblocks * blk

    def spec(j, prev):
        if prev:
            return pl.BlockSpec((1, 1, blk, gw), lambda b, r, n: (b, r, jnp.maximum(n * qblocks - 1, 0), j))
        return pl.BlockSpec((1, 1, rows, gw), lambda b, r, n: (b, r, n, j))

    return pl.pallas_call(
        functools.partial(_dilated_body, qblocks=qblocks),
        grid=(B, d, L // rows),
        in_specs=[spec(0, False), spec(1, True), spec(1, False), spec(2, True), spec(2, False),
                  pl.BlockSpec((DIL_HEADS * blk, 2 * blk), lambda b, r, n: (0, 0))],
        out_specs=pl.BlockSpec((1, 1, rows, 2 * gw), lambda b, r, n: (b, r, n, 0)),
        out_shape=jax.ShapeDtypeStruct((B, d, L, 2 * gw), F32),
        compiler_params=_params("parallel", "parallel", "arbitrary"),
        name=name,
    )(qkv, qkv, qkv, qkv, qkv, bias.reshape(DIL_HEADS * blk, 2 * blk))


def _moba_body(q_ref, k_ref, v_ref, bias_ref, o_ref, qa_ref, ka_ref, *, S):
    blk, hd = MOBA_BLOCK, HEAD_DIM
    nblk = S // blk
    scale = hd ** -0.5
    lane = lax.broadcasted_iota(jnp.int32, (1, 2 * hd), 1)
    kmean = jnp.mean(k_ref[0].astype(F32).reshape(nblk, blk, 2 * hd), axis=1)
    grp = MOBA_GROUP * blk
    row_blk = lax.broadcasted_iota(jnp.int32, (S, 2 * hd), 0) // blk
    q2, k2 = q_ref[0], k_ref[0]

    ka_ref[:, :2 * hd] = k2
    ka_ref[:, 2 * hd:] = jnp.where(lane == row_blk, 1.0, 0.0).astype(BF16)
    blk_t = lax.broadcasted_iota(jnp.int32, (nblk, S), 0)
    q_blk_t = lax.broadcasted_iota(jnp.int32, (nblk, S), 1) // blk
    past = blk_t < q_blk_t
    blk_f = blk_t.astype(F32)
    for h in range(2):
        head_lanes = (lane >= h * hd) & (lane < (h + 1) * hd)
        qh = jnp.where(head_lanes, q2, jnp.zeros_like(q2))
        gs = _nt_dot(jnp.where(head_lanes, kmean, 0.0), qh.astype(F32), precision=lax.Precision.HIGHEST)
        gs = jnp.where(past, gs, NEG_INF)
        sel = jnp.zeros(gs.shape, F32)
        for _r in range(MOBA_TOPK):
            gmax = jnp.max(gs, axis=0, keepdims=True)
            first = jnp.min(jnp.where(gs == gmax, blk_f, float(nblk)), axis=0, keepdims=True)
            pick = blk_f == first
            sel = jnp.where(pick & past, 1.0, sel)
            gs = jnp.where(pick, -jnp.inf, gs)
        penalty_t = jnp.where((sel > 0.5) | (blk_t == q_blk_t), 0.0, MOBA_MASK)
        penalty_t = jnp.concatenate([penalty_t, jnp.zeros((2 * hd - nblk, S), F32)], axis=0)
        qa_ref[h, :, :2 * hd] = qh * scale
        qa_ref[h, :, 2 * hd:] = penalty_t.T.astype(BF16)

    def q_block(qb, _):
        row0 = pl.multiple_of(qb * blk, blk)
        n_grp = qb // MOBA_GROUP + 1
        qa = jnp.concatenate([qa_ref[h, pl.ds(row0, blk), :] for h in range(2)], axis=0)

        def key_group(i, carry):
            m, l, acc = carry
            g = n_grp - 1 - i
            col0 = pl.multiple_of(g * grp, grp)
            s = _nt_dot(qa, ka_ref[pl.ds(col0, grp), :])
            parts = []
            for jj in range(MOBA_GROUP):
                tile = jnp.clip(qb - (g * MOBA_GROUP + jj), 0, MOBA_BIAS_TILES - 1)
                bias = jnp.concatenate([bias_ref[0, h, tile] for h in range(2)], axis=0)
                parts.append(s[:, jj * blk:(jj + 1) * blk] + bias)
            m_new = jnp.maximum(m, jnp.max(functools.reduce(jnp.maximum, parts), axis=-1, keepdims=True))
            a = jnp.exp(m - m_new)
            ps = [jnp.exp(p - m_new) for p in parts]
            l = l * a + jnp.sum(functools.reduce(jnp.add, ps), axis=-1, keepdims=True)
            p_all = jnp.concatenate([p.astype(BF16) for p in ps], axis=-1)
            pv = jnp.dot(p_all, v_ref[0, pl.ds(col0, grp), :], preferred_element_type=F32)
            return m_new, l, acc * a + pv

        init = (jnp.full((2 * blk, 1), -jnp.inf, F32), jnp.zeros((2 * blk, 1), F32), jnp.zeros((2 * blk, 2 * hd), F32))
        _, l, acc = lax.fori_loop(0, n_grp, key_group, init)
        o = acc / l
        o_ref[0, pl.ds(row0, blk), :] = jnp.where(lane < hd, o[:blk], o[blk:]).astype(o_ref.dtype)
        return 0

    lax.fori_loop(0, nblk, q_block, 0)


def _moba(qkv, bias, first_col):
    B, S, C = qkv.shape
    pw = 2 * HEAD_DIM
    npair = MOBA_HEADS // 2
    assert S % (MOBA_GROUP * MOBA_BLOCK) == 0 and S // MOBA_BLOCK <= HEAD_DIM and first_col % pw == 0
    c0 = first_col // pw

    def spec(j):
        return pl.BlockSpec((1, S, pw), lambda hp, b: (b, 0, c0 + j * npair + hp))

    return pl.pallas_call(
        functools.partial(_moba_body, S=S),
        grid=(npair, B),
        in_specs=[spec(0), spec(1), spec(2),
                  pl.BlockSpec((1, 2, MOBA_BIAS_TILES, MOBA_BLOCK, MOBA_BLOCK), lambda hp, b: (hp, 0, 0, 0, 0))],
        out_specs=pl.BlockSpec((1, S, pw), lambda hp, b: (b, 0, hp)),
        out_shape=jax.ShapeDtypeStruct((B, S, MOBA_HEADS * HEAD_DIM), BF16),
        scratch_shapes=[pltpu.VMEM((2, S, 2 * pw), BF16), pltpu.VMEM((S, 2 * pw), BF16)],
        compiler_params=_params("parallel", "parallel"),
        name="moba",
    )(qkv, qkv, qkv, bias)


def _merge_body(d0_ref, d1_ref, d2_ref, om_ref, gate_ref, x_ref, wa_ref, wb_ref, wo_ref, x1_ref, t1_ref, t2_ref):
    D = x_ref.shape[-1]
    gw = DIL_HEADS * HEAD_DIM
    n_slab = t1_ref.shape[0]
    for src, dst in ((d1_ref, t1_ref), (d2_ref, t2_ref)):
        dil, rows = src.shape[1], src.shape[2]
        for r in range(dil):
            for c in range(n_slab):
                dst[c, pl.ds(r, rows, stride=dil), :] = src[0, r, :, c * LANES:(c + 1) * LANES]
    g0 = d0_ref[...]
    g1 = jnp.concatenate([t1_ref[c] for c in range(n_slab)], axis=-1)
    g2 = jnp.concatenate([t2_ref[c] for c in range(n_slab)], axis=-1)
    l0, l1, l2 = g0[:, gw:], g1[:, gw:], g2[:, gw:]
    mx = jnp.maximum(jnp.maximum(l0, l1), l2)
    e0, e1, e2 = jnp.exp(l0 - mx), jnp.exp(l1 - mx), jnp.exp(l2 - mx)
    o_dil = (e0 * g0[:, :gw] + e1 * g1[:, :gw] + e2 * g2[:, :gw]) / (e0 + e1 + e2)
    a = jnp.dot(o_dil.astype(BF16), wa_ref[...], preferred_element_type=F32)
    b = jnp.dot(om_ref[...], wb_ref[...], preferred_element_type=F32)
    merged = gate_ref[:, :D].astype(F32) * a + gate_ref[:, D:].astype(F32) * b
    x1_ref[...] = x_ref[...] + jnp.dot(merged.astype(BF16), wo_ref[...], preferred_element_type=F32)


def _merge(dil_out, o_moba, gates, x2d, w_a, w_b, w_o, seq, tm):
    T, D = x2d.shape
    tm = min(tm, seq)
    per_seq = seq // tm
    assert seq % tm == 0 and dil_out[0].shape[1] == 1

    def rows(width):
        return pl.BlockSpec((tm, width), lambda i: (i, 0))

    def whole(w):
        return pl.BlockSpec(w.shape, lambda i: (0, 0))

    def residue_major(a):
        d, width = a.shape[1], a.shape[3]
        assert tm % (8 * d) == 0
        return pl.BlockSpec((1, d, tm // d, width), lambda i: (i // per_seq, 0, i % per_seq, 0))

    ws = [w_a.astype(BF16), w_b.astype(BF16), w_o.astype(BF16)]
    width = dil_out[0].shape[-1]
    return pl.pallas_call(
        _merge_body,
        grid=(T // tm,),
        in_specs=[rows(width), residue_major(dil_out[1]), residue_major(dil_out[2]),
                  rows(o_moba.shape[-1]), rows(2 * D), rows(D)] + [whole(w) for w in ws],
        out_specs=rows(D),
        out_shape=jax.ShapeDtypeStruct((T, D), F32),
        scratch_shapes=[pltpu.VMEM((width // LANES, tm, LANES), F32)] * 2,
        compiler_params=_params("parallel"),
        name="merge",
    )(dil_out[0].reshape(T, width), dil_out[1], dil_out[2], o_moba, gates, x2d, *ws)


def _cross_body(q_ref, kv_ref, x_ref, wo_ref, x2_ref):
    hd = CROSS_HEAD_DIM
    width = CROSS_HEADS * hd
    heads = []
    for h in range(CROSS_HEADS):
        sl = slice(h * hd, (h + 1) * hd)
        s = _nt_dot(q_ref[0, :, sl], kv_ref[0, :, sl]) * (hd ** -0.5)
        m = jnp.max(s, axis=-1, keepdims=True)
        p = jnp.exp(s - m)
        l = jnp.sum(p, axis=-1, keepdims=True)
        v = kv_ref[0, :, width + h * hd: width + (h + 1) * hd]
        heads.append((jnp.dot(p.astype(BF16), v, preferred_element_type=F32) / l).astype(BF16))
    o = jnp.concatenate(heads, axis=-1)
    x2_ref[0] = x_ref[0] + jnp.dot(o, wo_ref[...], preferred_element_type=F32)


def _cross(q, kv, x, w_o, tm):
    B, S, D = x.shape
    M = kv.shape[1]
    tm = min(tm, S)
    w_o = w_o.astype(BF16)
    return pl.pallas_call(
        _cross_body,
        grid=(B, S // tm),
        in_specs=[pl.BlockSpec((1, tm, q.shape[-1]), lambda b, i: (b, i, 0)),
                  pl.BlockSpec((1, M, kv.shape[-1]), lambda b, i: (b, 0, 0)),
                  pl.BlockSpec((1, tm, D), lambda b, i: (b, i, 0)),
                  pl.BlockSpec(w_o.shape, lambda b, i: (0, 0))],
        out_specs=pl.BlockSpec((1, tm, D), lambda b, i: (b, i, 0)),
        out_shape=jax.ShapeDtypeStruct((B, S, D), F32),
        compiler_params=_params("parallel", "parallel"),
        name="cross",
    )(q, kv, x, w_o)


def _top_sorted(s, k):
    rows = []
    for _ in range(k):
        m = jnp.max(s, axis=0, keepdims=True)
        rows.append(m)
        s = jnp.where(s == m, -jnp.inf, s)
    return jnp.concatenate(rows, axis=0)


def _pair_sum_candidates(a, b):
    assert PEER_TOPK == 16
    row = lax.broadcasted_iota(jnp.int32, (PEER_TOPK, 1), 0)
    blocks = [a[0:1] + b]
    blocks += [a[i:i + 1] + b[0:8] for i in (1, 2, 3)]
    blocks += [jnp.where(row >= 4, b[0:1] + a, -jnp.inf)]
    blocks += [jnp.where(row[0:8] >= 4, b[j:j + 1] + a[0:8], -jnp.inf) for j in (1, 2)]
    return jnp.concatenate(blocks, axis=0)


def _peer_route_body(q_ref, keys_ref, thr_ref, e1_ref, s2_ref, e2_ref):
    nk, K = PEER_N_KEYS, PEER_TOPK
    kd = keys_ref.shape[-1]
    for h in range(PEER_HEADS):
        sc = []
        for p in range(2):
            c0 = (2 * h + p) * kd
            sc.append(_nt_dot(keys_ref[p], q_ref[:, c0:c0 + kd], precision=lax.Precision.HIGHEST))
        a = _top_sorted(sc[0], K + 1)
        b = _top_sorted(sc[1], K + 1)
        cand = _pair_sum_candidates(a[:K], b[:K])
        cur, tau = cand, cand[0:1]
        count = jnp.zeros_like(tau)
        for _ in range(K):
            m = jnp.max(cur, axis=0, keepdims=True)
            eq = cur == m
            tau = jnp.where(count < K, m, tau)
            count = count + jnp.sum(jnp.where(eq, 1.0, 0.0), axis=0, keepdims=True)
            cur = jnp.where(eq, -jnp.inf, cur)
        below = jnp.max(jnp.where(cand < tau, cand, -jnp.inf), axis=0, keepdims=True)
        for corner in (a[0:1] + b[K:K + 1], a[K:K + 1] + b[0:1]):
            below = jnp.maximum(below, jnp.where(corner < tau, corner, -jnp.inf))
        below = jnp.where(below == -jnp.inf, tau, below)
        cut = 0.5 * (tau + below)
        top = a[0:1] + b[0:1]
        z = jnp.sum(jnp.where(cand >= tau, jnp.exp(cand - top), 0.0), axis=0, keepdims=True)
        thr_ref[h] = cut - sc[0]
        e1_ref[h] = jnp.exp(sc[0] - a[0:1]) * (0.5 / z)
        s2_ref[h] = sc[1]
        e2_ref[h] = jnp.exp(sc[1] - b[0:1])


def _peer_route(q, sub_keys, tt):
    T = q.shape[0]
    tt = min(tt, T)
    nk = PEER_N_KEYS
    big = jax.ShapeDtypeStruct((PEER_HEADS, nk, T), F32)
    big_spec = pl.BlockSpec((PEER_HEADS, nk, tt), lambda i: (0, 0, i))
    return pl.pallas_call(
        _peer_route_body,
        grid=(T // tt,),
        in_specs=[pl.BlockSpec((tt, q.shape[1]), lambda i: (i, 0)),
                  pl.BlockSpec(sub_keys.shape, lambda i: (0, 0, 0))],
        out_specs=[big_spec] * 4,
        out_shape=[big] * 4,
        compiler_params=_params("parallel"),
        name="peer_route",
    )(q, sub_keys.astype(F32))


def _peer_main_body(h_ref, u_ref, vt_ref, thr_ref, e1_ref, s2_ref, e2_ref, x_ref, g_ref,
                    o_ref, acc_ref, pre_ref, ga_ref, ht_ref, *, rows_per_step):
    nk = PEER_N_KEYS
    lanes = 128
    j = pl.program_id(1)
    tt = h_ref.shape[0]

    @pl.when(j == 0)
    def _():
        acc_ref[...] = jnp.zeros_like(acc_ref)
        ht_ref[...] = h_ref[...].astype(F32).T.astype(ht_ref.dtype)

    i1_base = pl.multiple_of(j * rows_per_step, rows_per_step)
    group = 4
    sub = 64
    n_sub = nk // sub

    pre_ref[...] = jnp.dot(u_ref[...], ht_ref[...], preferred_element_type=F32)

    def gate_tile(ti, carry):
        cs = pl.ds(pl.multiple_of((ti // n_sub) * lanes, lanes), lanes)
        k0 = pl.multiple_of((ti % n_sub) * sub, sub)
        for r0 in range(0, rows_per_step, group):
            gates = [None] * group
            for h in range(PEER_HEADS):
                thr8 = thr_ref[h, pl.ds(i1_base, rows_per_step), cs]
                e18 = e1_ref[h, pl.ds(i1_base, rows_per_step), cs]
                s2 = s2_ref[h, pl.ds(k0, sub), cs]
                e2 = e2_ref[h, pl.ds(k0, sub), cs]
                for g in range(group):
                    r = r0 + g
                    w = jnp.where(s2 >= thr8[r:r + 1], e2, 0.0) * e18[r:r + 1]
                    gates[g] = w if gates[g] is None else gates[g] + w
            for g in range(group):
                rs = pl.ds(pl.multiple_of((r0 + g) * nk + k0, sub), sub)
                pre = pre_ref[rs, cs]
                act = pre * (1.0 + lax.erf(pre * (2.0 ** -0.5)))
                ga_ref[rs, cs] = (gates[g] * act).astype(ga_ref.dtype)
        return carry

    lax.fori_loop(0, (tt // lanes) * n_sub, gate_tile, 0)
    acc_ref[...] += jnp.dot(vt_ref[0], ga_ref[...], preferred_element_type=F32)

    @pl.when(j == pl.num_programs(1) - 1)
    def _():
        o_ref[...] = _rms(x_ref[...] + acc_ref[...].T, g_ref[...])


def _peer_main(hn, u, v, thr, e1, s2, e2, x2d, g_final, tt, rows_per_step):
    T, D = x2d.shape
    tt = min(tt, T)
    nk = PEER_N_KEYS
    eb = rows_per_step * nk
    n_exp = u.shape[0]
    assert n_exp == nk * nk and nk % rows_per_step == 0
    u = u.astype(BF16)
    vt = v.astype(BF16).reshape(n_exp // eb, eb, D).transpose(0, 2, 1)
    score_spec = pl.BlockSpec((PEER_HEADS, nk, tt), lambda i, j: (0, 0, i))
    return pl.pallas_call(
        functools.partial(_peer_main_body, rows_per_step=rows_per_step),
        grid=(T // tt, n_exp // eb),
        in_specs=[pl.BlockSpec((tt, D), lambda i, j: (i, 0)),
                  pl.BlockSpec((eb, D), lambda i, j: (j, 0)),
                  pl.BlockSpec((1, D, eb), lambda i, j: (j, 0, 0)),
                  score_spec, score_spec, score_spec, score_spec,
                  pl.BlockSpec((tt, D), lambda i, j: (i, 0)),
                  pl.BlockSpec((1, D), lambda i, j: (0, 0))],
        out_specs=pl.BlockSpec((tt, D), lambda i, j: (i, 0)),
        out_shape=jax.ShapeDtypeStruct((T, D), F32),
        scratch_shapes=[pltpu.VMEM((D, tt), F32), pltpu.VMEM((eb, tt), F32), pltpu.VMEM((eb, tt), BF16),
                        pltpu.VMEM((D, tt), BF16)],
        compiler_params=_params("parallel", "arbitrary"),
        name="peer_main",
    )(hn, u, vt, thr, e1, s2, e2, x2d, g_final.reshape(1, D).astype(F32))


def kernel(x, mem, rel_bias, g_mix, w_in, b_gate, w_branch_a, w_branch_b, w_out, g_cross, g_mem, w_q_cross,
           w_kv_cross, w_o_cross, g_ffn, w_q_peer, peer_sub_keys, peer_u, peer_v, g_final):
    B, S, D = x.shape
    T = B * S
    depth = w_in.shape[0]
    ndil = len(DIL_GROUPS) * DIL_HEADS
    x2d = x.reshape(T, D)

    dil_bias = [_bias_lookup(_dil_bucket_map(d), rel_bias[:, gi * DIL_HEADS:(gi + 1) * DIL_HEADS], f"dil_bias_g{gi}")
                for gi, (_, d) in enumerate(DIL_GROUPS)]
    mmap = _moba_bucket_map()
    moba_bias = _bias_lookup(mmap.reshape(-1, MOBA_BLOCK), rel_bias[:, ndil:], "moba_bias")
    moba_bias = moba_bias.reshape(MOBA_HEADS // 2, 2, MOBA_BIAS_TILES, MOBA_BLOCK, MOBA_BLOCK)

    assert depth == 1, "the final norm is fused into the last PEER sweep; one layer supported"
    l = 0
    gc = DIL_GROUP_COLS
    assert DIL_GROUPS[0][1] == 1
    w_tok = jnp.concatenate([w_in[l][:, :gc], w_in[l][:, DIL_WIDTH:QKV_WIDTH]], axis=1)
    tok = _norm_matmul(x2d, g_mix[l], w_tok, out_dtype=BF16, tm=1024, tn=gc, name="proj_tok")
    dil_qkv = [tok.reshape(B, 1, S, -1)]
    for gi, (_, d) in enumerate(DIL_GROUPS[1:], start=1):
        dil_qkv.append(_norm_matmul(x2d, g_mix[l], w_in[l][:, gi * gc:(gi + 1) * gc], out_dtype=BF16, tm=1024, tn=gc,
                                    dil=d, seq=S, name=f"proj_dil_g{gi}"))
    gates = _norm_matmul(x2d, g_mix[l], w_in[l][:, QKV_WIDTH:], out_dtype=BF16, tm=1024, tn=1024,
                         bias=b_gate[l], name="proj_gates")
    dil_out = [_dilated_group(a, dil_bias[gi], f"dilated_g{gi}") for gi, a in enumerate(dil_qkv)]
    o_moba = _moba(tok.reshape(B, S, -1), moba_bias, first_col=gc).reshape(T, MOBA_HEADS * HEAD_DIM)
    x2d = _merge(dil_out, o_moba, gates, x2d, w_branch_a[l], w_branch_b[l], w_out[l], seq=S, tm=512)
    M = mem.shape[1]
    q_c = _norm_matmul(x2d, g_cross[l], w_q_cross[l], out_dtype=BF16, tm=1024, tn=512, name="cross_q")
    kv = _norm_matmul(mem.reshape(B * M, D), g_mem[l], w_kv_cross[l], out_dtype=BF16, tm=1024, tn=1024, name="cross_kv")
    x2d = _cross(q_c.reshape(B, S, -1), kv.reshape(B, M, -1), x2d.reshape(B, S, D), w_o_cross[l], tm=1024).reshape(T, D)
    q_p, hn = _norm_matmul(x2d, g_ffn[l], w_q_peer[l], out_dtype=F32, tm=1024, tn=1024, emit_normed=True, name="peer_q")
    thr, e1, s2, e2 = _peer_route(q_p, peer_sub_keys[l], tt=512)
    y = _peer_main(hn, peer_u[l], peer_v[l], thr, e1, s2, e2, x2d, g_final,
                   tt=512, rows_per_step=8)
    return y.reshape(B, S, D)
```

```python
import functools
import math

import jax
import jax.numpy as jnp
import numpy as np
from jax import lax
from jax.experimental import pallas as pl
from jax.experimental.pallas import tpu as pltpu

HEAD_DIM = 64
DIL_GROUPS = ((128, 1), (512, 4), (2048, 16))
DIL_HEADS = 4
DIL_BLOCK = 128
MOBA_HEADS = 8
MOBA_BLOCK = 256
MOBA_TOPK = 3
MOBA_BIAS_TILES = 8
MOBA_GROUP = 4
MOBA_MASK = -(2.0 ** 100)
REL_BUCKETS = 32
REL_MAX_DIST = 2048
CROSS_HEADS = 4
CROSS_HEAD_DIM = 128
PEER_HEADS = 8
PEER_N_KEYS = 128
PEER_TOPK = 16
RMS_EPS = 1e-6
NEG_INF = -1e30
DIL_GROUP_COLS = 3 * DIL_HEADS * HEAD_DIM
DIL_WIDTH = len(DIL_GROUPS) * DIL_GROUP_COLS
MOBA_WIDTH = 3 * MOBA_HEADS * HEAD_DIM
QKV_WIDTH = DIL_WIDTH + MOBA_WIDTH

LANES = 128
V7X_VMEM_BYTES = 64 * 1024 * 1024
VMEM_LIMIT = V7X_VMEM_BYTES * 3 // 4

BF16 = jnp.bfloat16
F32 = jnp.float32


def _params(*sem):
    return pltpu.CompilerParams(dimension_semantics=sem, vmem_limit_bytes=VMEM_LIMIT)


def _rms(x, g):
    return x * lax.rsqrt(jnp.mean(x * x, axis=-1, keepdims=True) + RMS_EPS) * g


def _nt_dot(a, b, precision=None):
    return lax.dot_general(a, b, (((1,), (1,)), ((), ())), preferred_element_type=F32, precision=precision)


def _norm_matmul_body(x_ref, g_ref, w_ref, *rest, sigmoid_bias, dil):
    rest = list(rest)
    b_ref = rest.pop(0) if sigmoid_bias else None
    o_ref, h_scr = rest[:2]
    rows = h_scr.shape[0] // dil

    @pl.when(pl.program_id(1) == 0)
    def _():
        if dil == 1:
            h_scr[...] = _rms(x_ref[...], g_ref[...]).astype(h_scr.dtype)
        else:
            cols_ref = rest[2]
            n_slab = cols_ref.shape[0]
            for c in range(n_slab):
                cols_ref[c] = x_ref[:, c * LANES:(c + 1) * LANES]
            for r in range(dil):
                x = jnp.concatenate([cols_ref[c, pl.ds(r, rows, stride=dil), :] for c in range(n_slab)], axis=-1)
                h_scr[r * rows:(r + 1) * rows, :] = _rms(x, g_ref[...]).astype(h_scr.dtype)

    acc = jnp.dot(h_scr[...], w_ref[...], preferred_element_type=F32)
    if sigmoid_bias:
        acc = jax.nn.sigmoid(acc + b_ref[...])
    if dil == 1:
        o_ref[...] = acc.astype(o_ref.dtype)
    else:
        for r in range(dil):
            o_ref[0, r] = acc[r * rows:(r + 1) * rows].astype(o_ref.dtype)


def _norm_matmul(x2d, g, w, *, out_dtype, tm, tn, bias=None, dil=1, seq=None, emit_normed=False, name):
    T, D = x2d.shape
    N = w.shape[1]
    tm = min(tm, T if seq is None else seq)
    assert T % tm == 0 and N % tn == 0
    in_specs = [
        pl.BlockSpec((tm, D), lambda i, j: (i, 0)),
        pl.BlockSpec((1, D), lambda i, j: (0, 0)),
        pl.BlockSpec((D, tn), lambda i, j: (0, j)),
    ]
    args = [x2d, g.reshape(1, D).astype(F32), w.astype(BF16)]
    if bias is not None:
        in_specs.append(pl.BlockSpec((1, tn), lambda i, j: (0, j)))
        args.append(bias.reshape(1, N).astype(F32))
    assert not (emit_normed and dil > 1)
    scratch = [] if emit_normed else [pltpu.VMEM((tm, D), BF16)]
    if dil == 1:
        out_spec = pl.BlockSpec((tm, tn), lambda i, j: (i, j))
        out_shape = jax.ShapeDtypeStruct((T, N), out_dtype)
    else:
        assert seq % tm == 0 and tm % (8 * dil) == 0 and D % LANES == 0
        per_seq = seq // tm
        out_spec = pl.BlockSpec((1, dil, tm // dil, tn), lambda i, j: (i // per_seq, 0, i % per_seq, j))
        out_shape = jax.ShapeDtypeStruct((T // seq, dil, seq // dil, N), out_dtype)
        scratch.append(pltpu.VMEM((D // LANES, tm, LANES), F32))
    if emit_normed:
        out_spec = [out_spec, pl.BlockSpec((tm, D), lambda i, j: (i, 0))]
        out_shape = [out_shape, jax.ShapeDtypeStruct((T, D), BF16)]
    return pl.pallas_call(
        functools.partial(_norm_matmul_body, sigmoid_bias=bias is not None, dil=dil),
        grid=(T // tm, N // tn),
        in_specs=in_specs,
        out_specs=out_spec,
        out_shape=out_shape,
        scratch_shapes=scratch,
        compiler_params=_params("parallel", "arbitrary"),
        name=name,
    )(*args)


def _t5_bucket_np(n):
    max_exact = REL_BUCKETS // 2
    nf = np.maximum(n, 1).astype(np.float32)
    large = max_exact + (np.log(nf / np.float32(max_exact)) / np.float32(math.log(REL_MAX_DIST / max_exact))
                         * np.float32(REL_BUCKETS - max_exact)).astype(np.int32)
    large = np.minimum(large, REL_BUCKETS - 1)
    return np.where(n < max_exact, n, large).astype(np.int32)


def _dil_bucket_map(d):
    blk = DIL_BLOCK
    rel = (blk + np.arange(blk))[:, None] - np.arange(2 * blk)[None, :]
    return _t5_bucket_np(np.clip(rel, 0, None) * d)


def _moba_bucket_map():
    blk = MOBA_BLOCK
    rel = np.arange(blk)[:, None] - np.arange(blk)[None, :]
    tiles = [_t5_bucket_np(np.clip(rel + dlt * blk, 0, None)) for dlt in range(MOBA_BIAS_TILES)]
    tiles[0] = np.where(rel >= 0, tiles[0], -1)
    far = _t5_bucket_np(np.arange((MOBA_BIAS_TILES - 2) * blk + 1, 64 * blk))
    assert (far == REL_BUCKETS - 1).all()
    return np.stack(tiles)


def _bias_lookup_body(idx_ref, tab_ref, o_ref):
    h = pl.program_id(0)
    idx = idx_ref[...]

    def bucket(b, acc):
        return jnp.where(idx == b, tab_ref[b, h], acc)

    o_ref[0] = lax.fori_loop(0, REL_BUCKETS, bucket, jnp.full(idx.shape, NEG_INF, F32))


def _bias_lookup(bucket_map, table, name):
    R, C = bucket_map.shape
    H = table.shape[1]
    rt = min(R, 256)
    assert R % rt == 0
    return pl.pallas_call(
        _bias_lookup_body,
        grid=(H, R // rt),
        in_specs=[pl.BlockSpec((rt, C), lambda h, i: (i, 0)), pl.BlockSpec(memory_space=pltpu.SMEM)],
        out_specs=pl.BlockSpec((1, rt, C), lambda h, i: (h, i, 0)),
        out_shape=jax.ShapeDtypeStruct((H, R, C), F32),
        compiler_params=_params("parallel", "parallel"),
        name=name,
    )(jnp.asarray(bucket_map), table.astype(F32))


def _dilated_body(q_ref, kp_ref, kc_ref, vp_ref, vc_ref, bias_ref, o_ref, *, qblocks):
    blk, hd = DIL_BLOCK, HEAD_DIM
    gw = DIL_HEADS * hd
    n = pl.program_id(2)
    k = jnp.concatenate([kp_ref[0, 0], kc_ref[0, 0]], axis=0)
    v = jnp.concatenate([vp_ref[0, 0], vc_ref[0, 0]], axis=0)
    lane = lax.broadcasted_iota(jnp.int32, (1, gw), 1)
    qi = lax.broadcasted_iota(jnp.int32, (blk, 2 * blk), 0)
    kj = lax.broadcasted_iota(jnp.int32, (blk, 2 * blk), 1)
    rel = blk + qi - kj
    band = (rel >= 0) & (rel <= blk)
    first_key = jnp.where(n > 0, 0, blk)
    for i in range(qblocks):
        rows = slice(i * blk, (i + 1) * blk)
        q = q_ref[0, 0, rows, :]
        kw = k[i * blk:(i + 2) * blk]
        vw = v[i * blk:(i + 2) * blk]
        mask = band & (kj >= first_key) if i == 0 else band
        head_lanes = [(lane >= h * hd) & (lane < (h + 1) * hd) for h in range(DIL_HEADS)]
        q_all = jnp.concatenate([jnp.where(hl, q, jnp.zeros_like(q)) for hl in head_lanes], axis=0)
        s = _nt_dot(q_all, kw) * (hd ** -0.5) + bias_ref[...]
        s = jnp.where(jnp.concatenate([mask] * DIL_HEADS, axis=0), s, NEG_INF)
        m = jnp.max(s, axis=-1, keepdims=True)
        p = jnp.exp(s - m)
        l = jnp.sum(p, axis=-1, keepdims=True)
        pv = jnp.dot(p.astype(BF16), vw, preferred_element_type=F32) / l
        lse_all = m + jnp.log(l)
        out = jnp.zeros((blk, gw), F32)
        lse = jnp.zeros((blk, gw), F32)
        for h, hl in enumerate(head_lanes):
            out = jnp.where(hl, pv[h * blk:(h + 1) * blk], out)
            lse = jnp.where(hl, lse_all[h * blk:(h + 1) * blk], lse)
        o_ref[0, 0, rows, :gw] = out
        o_ref[0, 0, rows, gw:] = lse


def _dilated_group(qkv, bias, name):
    B, d, L, _ = qkv.shape
    blk, gw = DIL_BLOCK, DIL_HEADS * HEAD_DIM
    assert L % blk == 0
    qblocks = min(4, L // blk)
    assert L % (qblocks * blk) == 0
    rows = qblocks * blk

    def spec(j, prev):
        if prev:
            return pl.BlockSpec((1, 1, blk, gw), lambda b, r, n: (b, r, jnp.maximum(n * qblocks - 1, 0), j))
        return pl.BlockSpec((1, 1, rows, gw), lambda b, r, n: (b, r, n, j))

    return pl.pallas_call(
        functools.partial(_dilated_body, qblocks=qblocks),
        grid=(B, d, L // rows),
        in_specs=[spec(0, False), spec(1, True), spec(1, False), spec(2, True), spec(2, False),
                  pl.BlockSpec((DIL_HEADS * blk, 2 * blk), lambda b, r, n: (0, 0))],
        out_specs=pl.BlockSpec((1, 1, rows, 2 * gw), lambda b, r, n: (b, r, n, 0)),
        out_shape=jax.ShapeDtypeStruct((B, d, L, 2 * gw), F32),
        compiler_params=_params("parallel", "parallel", "arbitrary"),
        name=name,
    )(qkv, qkv, qkv, qkv, qkv, bias.reshape(DIL_HEADS * blk, 2 * blk))


def _moba_body(q_ref, k_ref, v_ref, bias_ref, o_ref, qa_ref, ka_ref, *, S):
    blk, hd = MOBA_BLOCK, HEAD_DIM
    nblk = S // blk
    scale = hd ** -0.5
    lane = lax.broadcasted_iota(jnp.int32, (1, 2 * hd), 1)
    kmean = jnp.mean(k_ref[0].astype(F32).reshape(nblk, blk, 2 * hd), axis=1)
    grp = MOBA_GROUP * blk
    row_blk = lax.broadcasted_iota(jnp.int32, (S, 2 * hd), 0) // blk
    q2, k2 = q_ref[0], k_ref[0]

    ka_ref[:, :2 * hd] = k2
    ka_ref[:, 2 * hd:] = jnp.where(lane == row_blk, 1.0, 0.0).astype(BF16)
    blk_t = lax.broadcasted_iota(jnp.int32, (nblk, S), 0)
    q_blk_t = lax.broadcasted_iota(jnp.int32, (nblk, S), 1) // blk
    past = blk_t < q_blk_t
    blk_f = blk_t.astype(F32)
    for h in range(2):
        head_lanes = (lane >= h * hd) & (lane < (h + 1) * hd)
        qh = jnp.where(head_lanes, q2, jnp.zeros_like(q2))
        gs = _nt_dot(jnp.where(head_lanes, kmean, 0.0), qh.astype(F32), precision=lax.Precision.HIGHEST)
        gs = jnp.where(past, gs, NEG_INF)
        sel = jnp.zeros(gs.shape, F32)
        for _r in range(MOBA_TOPK):
            gmax = jnp.max(gs, axis=0, keepdims=True)
            first = jnp.min(jnp.where(gs == gmax, blk_f, float(nblk)), axis=0, keepdims=True)
            pick = blk_f == first
            sel = jnp.where(pick & past, 1.0, sel)
            gs = jnp.where(pick, -jnp.inf, gs)
        penalty_t = jnp.where((sel > 0.5) | (blk_t == q_blk_t), 0.0, MOBA_MASK)
        penalty_t = jnp.concatenate([penalty_t, jnp.zeros((2 * hd - nblk, S), F32)], axis=0)
        qa_ref[h, :, :2 * hd] = qh * scale
        qa_ref[h, :, 2 * hd:] = penalty_t.T.astype(BF16)

    def q_block(qb, _):
        row0 = pl.multiple_of(qb * blk, blk)
        n_grp = qb // MOBA_GROUP + 1
        qa = jnp.concatenate([qa_ref[h, pl.ds(row0, blk), :] for h in range(2)], axis=0)

        def key_group(i, carry):
            m, l, acc = carry
            g = n_grp - 1 - i
            col0 = pl.multiple_of(g * grp, grp)
            s = _nt_dot(qa, ka_ref[pl.ds(col0, grp), :])
            parts = []
            for jj in range(MOBA_GROUP):
                tile = jnp.clip(qb - (g * MOBA_GROUP + jj), 0, MOBA_BIAS_TILES - 1)
                bias = jnp.concatenate([bias_ref[0, h, tile] for h in range(2)], axis=0)
                parts.append(s[:, jj * blk:(jj + 1) * blk] + bias)
            m_new = jnp.maximum(m, jnp.max(functools.reduce(jnp.maximum, parts), axis=-1, keepdims=True))
            a = jnp.exp(m - m_new)
            ps = [jnp.exp(p - m_new) for p in parts]
            l = l * a + jnp.sum(functools.reduce(jnp.add, ps), axis=-1, keepdims=True)
            p_all = jnp.concatenate([p.astype(BF16) for p in ps], axis=-1)
            pv = jnp.dot(p_all, v_ref[0, pl.ds(col0, grp), :], preferred_element_type=F32)
            return m_new, l, acc * a + pv

        init = (jnp.full((2 * blk, 1), -jnp.inf, F32), jnp.zeros((2 * blk, 1), F32), jnp.zeros((2 * blk, 2 * hd), F32))
        _, l, acc = lax.fori_loop(0, n_grp, key_group, init)
        o = acc / l
        o_ref[0, pl.ds(row0, blk), :] = jnp.where(lane < hd, o[:blk], o[blk:]).astype(o_ref.dtype)
        return 0

    lax.fori_loop(0, nblk, q_block, 0)


def _moba(qkv, bias, first_col):
    B, S, C = qkv.shape
    pw = 2 * HEAD_DIM
    npair = MOBA_HEADS // 2
    assert S % (MOBA_GROUP * MOBA_BLOCK) == 0 and S // MOBA_BLOCK <= HEAD_DIM and first_col % pw == 0
    c0 = first_col // pw

    def spec(j):
        return pl.BlockSpec((1, S, pw), lambda hp, b: (b, 0, c0 + j * npair + hp))

    return pl.pallas_call(
        functools.partial(_moba_body, S=S),
        grid=(npair, B),
        in_specs=[spec(0), spec(1), spec(2),
                  pl.BlockSpec((1, 2, MOBA_BIAS_TILES, MOBA_BLOCK, MOBA_BLOCK), lambda hp, b: (hp, 0, 0, 0, 0))],
        out_specs=pl.BlockSpec((1, S, pw), lambda hp, b: (b, 0, hp)),
        out_shape=jax.ShapeDtypeStruct((B, S, MOBA_HEADS * HEAD_DIM), BF16),
        scratch_shapes=[pltpu.VMEM((2, S, 2 * pw), BF16), pltpu.VMEM((S, 2 * pw), BF16)],
        compiler_params=_params("parallel", "parallel"),
        name="moba",
    )(qkv, qkv, qkv, bias)


def _merge_body(d0_ref, d1_ref, d2_ref, om_ref, gate_ref, x_ref, wa_ref, wb_ref, wo_ref, x1_ref, t1_ref, t2_ref):
    D = x_ref.shape[-1]
    gw = DIL_HEADS * HEAD_DIM
    n_slab = t1_ref.shape[0]
    for src, dst in ((d1_ref, t1_ref), (d2_ref, t2_ref)):
        dil, rows = src.shape[1], src.shape[2]
        for r in range(dil):
            for c in range(n_slab):
                dst[c, pl.ds(r, rows, stride=dil), :] = src[0, r, :, c * LANES:(c + 1) * LANES]
    g0 = d0_ref[...]
    g1 = jnp.concatenate([t1_ref[c] for c in range(n_slab)], axis=-1)
    g2 = jnp.concatenate([t2_ref[c] for c in range(n_slab)], axis=-1)
    l0, l1, l2 = g0[:, gw:], g1[:, gw:], g2[:, gw:]
    mx = jnp.maximum(jnp.maximum(l0, l1), l2)
    e0, e1, e2 = jnp.exp(l0 - mx), jnp.exp(l1 - mx), jnp.exp(l2 - mx)
    o_dil = (e0 * g0[:, :gw] + e1 * g1[:, :gw] + e2 * g2[:, :gw]) / (e0 + e1 + e2)
    a = jnp.dot(o_dil.astype(BF16), wa_ref[...], preferred_element_type=F32)
    b = jnp.dot(om_ref[...], wb_ref[...], preferred_element_type=F32)
    merged = gate_ref[:, :D].astype(F32) * a + gate_ref[:, D:].astype(F32) * b
    x1_ref[...] = x_ref[...] + jnp.dot(merged.astype(BF16), wo_ref[...], preferred_element_type=F32)


def _merge(dil_out, o_moba, gates, x2d, w_a, w_b, w_o, seq, tm):
    T, D = x2d.shape
    tm = min(tm, seq)
    per_seq = seq // tm
    assert seq % tm == 0 and dil_out[0].shape[1] == 1

    def rows(width):
        return pl.BlockSpec((tm, width), lambda i: (i, 0))

    def whole(w):
        return pl.BlockSpec(w.shape, lambda i: (0, 0))

    def residue_major(a):
        d, width = a.shape[1], a.shape[3]
        assert tm % (8 * d) == 0
        return pl.BlockSpec((1, d, tm // d, width), lambda i: (i // per_seq, 0, i % per_seq, 0))

    ws = [w_a.astype(BF16), w_b.astype(BF16), w_o.astype(BF16)]
    width = dil_out[0].shape[-1]
    return pl.pallas_call(
        _merge_body,
        grid=(T // tm,),
        in_specs=[rows(width), residue_major(dil_out[1]), residue_major(dil_out[2]),
                  rows(o_moba.shape[-1]), rows(2 * D), rows(D)] + [whole(w) for w in ws],
        out_specs=rows(D),
        out_shape=jax.ShapeDtypeStruct((T, D), F32),
        scratch_shapes=[pltpu.VMEM((width // LANES, tm, LANES), F32)] * 2,
        compiler_params=_params("parallel"),
        name="merge",
    )(dil_out[0].reshape(T, width), dil_out[1], dil_out[2], o_moba, gates, x2d, *ws)


def _cross_body(q_ref, kv_ref, x_ref, wo_ref, x2_ref):
    hd = CROSS_HEAD_DIM
    width = CROSS_HEADS * hd
    heads = []
    for h in range(CROSS_HEADS):
        sl = slice(h * hd, (h + 1) * hd)
        s = _nt_dot(q_ref[0, :, sl], kv_ref[0, :, sl]) * (hd ** -0.5)
        m = jnp.max(s, axis=-1, keepdims=True)
        p = jnp.exp(s - m)
        l = jnp.sum(p, axis=-1, keepdims=True)
        v = kv_ref[0, :, width + h * hd: width + (h + 1) * hd]
        heads.append((jnp.dot(p.astype(BF16), v, preferred_element_type=F32) / l).astype(BF16))
    o = jnp.concatenate(heads, axis=-1)
    x2_ref[0] = x_ref[0] + jnp.dot(o, wo_ref[...], preferred_element_type=F32)


def _cross(q, kv, x, w_o, tm):
    B, S, D = x.shape
    M = kv.shape[1]
    tm = min(tm, S)
    w_o = w_o.astype(BF16)
    return pl.pallas_call(
        _cross_body,
        grid=(B, S // tm),
        in_specs=[pl.BlockSpec((1, tm, q.shape[-1]), lambda b, i: (b, i, 0)),
                  pl.BlockSpec((1, M, kv.shape[-1]), lambda b, i: (b, 0, 0)),
                  pl.BlockSpec((1, tm, D), lambda b, i: (b, i, 0)),
                  pl.BlockSpec(w_o.shape, lambda b, i: (0, 0))],
        out_specs=pl.BlockSpec((1, tm, D), lambda b, i: (b, i, 0)),
        out_shape=jax.ShapeDtypeStruct((B, S, D), F32),
        compiler_params=_params("parallel", "parallel"),
        name="cross",
    )(q, kv, x, w_o)


def _top_sorted(s, k):
    rows = []
    for _ in range(k):
        m = jnp.max(s, axis=0, keepdims=True)
        rows.append(m)
        s = jnp.where(s == m, -jnp.inf, s)
    return jnp.concatenate(rows, axis=0)


def _pair_sum_candidates(a, b):
    assert PEER_TOPK == 16
    row = lax.broadcasted_iota(jnp.int32, (PEER_TOPK, 1), 0)
    blocks = [a[0:1] + b]
    blocks += [a[i:i + 1] + b[0:8] for i in (1, 2, 3)]
    blocks += [jnp.where(row >= 4, b[0:1] + a, -jnp.inf)]
    blocks += [jnp.where(row[0:8] >= 4, b[j:j + 1] + a[0:8], -jnp.inf) for j in (1, 2)]
    return jnp.concatenate(blocks, axis=0)


def _peer_route_body(q_ref, keys_ref, thr_ref, e1_ref, s2_ref, e2_ref):
    nk, K = PEER_N_KEYS, PEER_TOPK
    kd = keys_ref.shape[-1]
    for h in range(PEER_HEADS):
        sc = []
        for p in range(2):
            c0 = (2 * h + p) * kd
            sc.append(_nt_dot(keys_ref[p], q_ref[:, c0:c0 + kd], precision=lax.Precision.HIGHEST))
        a = _top_sorted(sc[0], K + 1)
        b = _top_sorted(sc[1], K + 1)
        cand = _pair_sum_candidates(a[:K], b[:K])
        cur, tau = cand, cand[0:1]
        count = jnp.zeros_like(tau)
        for _ in range(K):
            m = jnp.max(cur, axis=0, keepdims=True)
            eq = cur == m
            tau = jnp.where(count < K, m, tau)
            count = count + jnp.sum(jnp.where(eq, 1.0, 0.0), axis=0, keepdims=True)
            cur = jnp.where(eq, -jnp.inf, cur)
        below = jnp.max(jnp.where(cand < tau, cand, -jnp.inf), axis=0, keepdims=True)
        for corner in (a[0:1] + b[K:K + 1], a[K:K + 1] + b[0:1]):
            below = jnp.maximum(below, jnp.where(corner < tau, corner, -jnp.inf))
        below = jnp.where(below == -jnp.inf, tau, below)
        cut = 0.5 * (tau + below)
        top = a[0:1] + b[0:1]
        z = jnp.sum(jnp.where(cand >= tau, jnp.exp(cand - top), 0.0), axis=0, keepdims=True)
        thr = cut - sc[0]
        e1 = jnp.exp(sc[0] - a[0:1]) * (0.5 / z)
        for c in range(thr_ref.shape[1]):
            thr_ref[h, c] = thr[:, c * LANES:(c + 1) * LANES]
            e1_ref[h, c] = e1[:, c * LANES:(c + 1) * LANES]
        s2_ref[h] = sc[1]
        e2_ref[h] = jnp.exp(sc[1] - b[0:1])


def _peer_route(q, sub_keys, tt):
    T = q.shape[0]
    tt = min(tt, T)
    nk = PEER_N_KEYS
    assert tt % LANES == 0
    big = jax.ShapeDtypeStruct((PEER_HEADS, nk, T), F32)
    big_spec = pl.BlockSpec((PEER_HEADS, nk, tt), lambda i: (0, 0, i))
    slab = jax.ShapeDtypeStruct((PEER_HEADS, T // LANES, nk, LANES), F32)
    slab_spec = pl.BlockSpec((PEER_HEADS, tt // LANES, nk, LANES), lambda i: (0, i, 0, 0))
    return pl.pallas_call(
        _peer_route_body,
        grid=(T // tt,),
        in_specs=[pl.BlockSpec((tt, q.shape[1]), lambda i: (i, 0)),
                  pl.BlockSpec(sub_keys.shape, lambda i: (0, 0, 0))],
        out_specs=[slab_spec, slab_spec, big_spec, big_spec],
        out_shape=[slab, slab, big, big],
        compiler_params=_params("parallel"),
        name="peer_route",
    )(q, sub_keys.astype(F32))


def _peer_main_body(h_ref, u_ref, vt_ref, thr_ref, e1_ref, s2_ref, e2_ref, x_ref, g_ref,
                    o_ref, acc_ref, pre_ref, ga_ref, ht_ref, *, rows_per_step):
    nk = PEER_N_KEYS
    lanes = 128
    j = pl.program_id(1)
    tt = h_ref.shape[0]

    @pl.when(j == 0)
    def _():
        acc_ref[...] = jnp.zeros_like(acc_ref)
        ht_ref[...] = h_ref[...].astype(F32).T.astype(ht_ref.dtype)

    i1_base = pl.multiple_of(j * rows_per_step, rows_per_step)
    group = 4
    sub = 64
    n_sub = nk // sub

    pre_ref[...] = jnp.dot(u_ref[...], ht_ref[...], preferred_element_type=F32)

    def gate_tile(ti, carry):
        cs = pl.ds(pl.multiple_of((ti // n_sub) * lanes, lanes), lanes)
        k0 = pl.multiple_of((ti % n_sub) * sub, sub)
        for r0 in range(0, rows_per_step, group):
            gates = [None] * group
            for h in range(PEER_HEADS):
                s2 = s2_ref[h, pl.ds(k0, sub), cs]
                e2 = e2_ref[h, pl.ds(k0, sub), cs]
                for g in range(group):
                    r = r0 + g
                    thr = thr_ref[h, ti // n_sub, pl.ds(i1_base + r, 8, stride=0), :]
                    e1 = e1_ref[h, ti // n_sub, pl.ds(i1_base + r, 8, stride=0), :]
                    w = (jnp.where(s2.reshape(sub // 8, 8, lanes) >= thr, e2.reshape(sub // 8, 8, lanes), 0.0)
                         * e1).reshape(sub, lanes)
                    gates[g] = w if gates[g] is None else gates[g] + w
            for g in range(group):
                rs = pl.ds(pl.multiple_of((r0 + g) * nk + k0, sub), sub)
                pre = pre_ref[rs, cs]
                act = pre * (1.0 + lax.erf(pre * (2.0 ** -0.5)))
                ga_ref[rs, cs] = (gates[g] * act).astype(ga_ref.dtype)
        return carry

    lax.fori_loop(0, (tt // lanes) * n_sub, gate_tile, 0)
    acc_ref[...] += jnp.dot(vt_ref[0], ga_ref[...], preferred_element_type=F32)

    @pl.when(j == pl.num_programs(1) - 1)
    def _():
        o_ref[...] = _rms(x_ref[...] + acc_ref[...].T, g_ref[...])


def _peer_main(hn, u, v, thr, e1, s2, e2, x2d, g_final, tt, rows_per_step):
    T, D = x2d.shape
    tt = min(tt, T)
    nk = PEER_N_KEYS
    eb = rows_per_step * nk
    n_exp = u.shape[0]
    assert n_exp == nk * nk and nk % rows_per_step == 0
    u = u.astype(BF16)
    vt = v.astype(BF16).reshape(n_exp // eb, eb, D).transpose(0, 2, 1)
    score_spec = pl.BlockSpec((PEER_HEADS, nk, tt), lambda i, j: (0, 0, i))
    slab_spec = pl.BlockSpec((PEER_HEADS, tt // LANES, nk, LANES), lambda i, j: (0, i, 0, 0))
    return pl.pallas_call(
        functools.partial(_peer_main_body, rows_per_step=rows_per_step),
        grid=(T // tt, n_exp // eb),
        in_specs=[pl.BlockSpec((tt, D), lambda i, j: (i, 0)),
                  pl.BlockSpec((eb, D), lambda i, j: (j, 0)),
                  pl.BlockSpec((1, D, eb), lambda i, j: (j, 0, 0)),
                  slab_spec, slab_spec, score_spec, score_spec,
                  pl.BlockSpec((tt, D), lambda i, j: (i, 0)),
                  pl.BlockSpec((1, D), lambda i, j: (0, 0))],
        out_specs=pl.BlockSpec((tt, D), lambda i, j: (i, 0)),
        out_shape=jax.ShapeDtypeStruct((T, D), F32),
        scratch_shapes=[pltpu.VMEM((D, tt), F32), pltpu.VMEM((eb, tt), F32), pltpu.VMEM((eb, tt), BF16),
                        pltpu.VMEM((D, tt), BF16)],
        compiler_params=_params("parallel", "arbitrary"),
        name="peer_main",
    )(hn, u, vt, thr, e1, s2, e2, x2d, g_final.reshape(1, D).astype(F32))


def kernel(x, mem, rel_bias, g_mix, w_in, b_gate, w_branch_a, w_branch_b, w_out, g_cross, g_mem, w_q_cross,
           w_kv_cross, w_o_cross, g_ffn, w_q_peer, peer_sub_keys, peer_u, peer_v, g_final):
    B, S, D = x.shape
    T = B * S
    depth = w_in.shape[0]
    ndil = len(DIL_GROUPS) * DIL_HEADS
    x2d = x.reshape(T, D)

    dil_bias = [_bias_lookup(_dil_bucket_map(d), rel_bias[:, gi * DIL_HEADS:(gi + 1) * DIL_HEADS], f"dil_bias_g{gi}")
                for gi, (_, d) in enumerate(DIL_GROUPS)]
    mmap = _moba_bucket_map()
    moba_bias = _bias_lookup(mmap.reshape(-1, MOBA_BLOCK), rel_bias[:, ndil:], "moba_bias")
    moba_bias = moba_bias.reshape(MOBA_HEADS // 2, 2, MOBA_BIAS_TILES, MOBA_BLOCK, MOBA_BLOCK)

    assert depth == 1, "the final norm is fused into the last PEER sweep; one layer supported"
    l = 0
    gc = DIL_GROUP_COLS
    assert DIL_GROUPS[0][1] == 1
    w_tok = jnp.concatenate([w_in[l][:, :gc], w_in[l][:, DIL_WIDTH:QKV_WIDTH]], axis=1)
    tok = _norm_matmul(x2d, g_mix[l], w_tok, out_dtype=BF16, tm=1024, tn=gc, name="proj_tok")
    dil_qkv = [tok.reshape(B, 1, S, -1)]
    for gi, (_, d) in enumerate(DIL_GROUPS[1:], start=1):
        dil_qkv.append(_norm_matmul(x2d, g_mix[l], w_in[l][:, gi * gc:(gi + 1) * gc], out_dtype=BF16, tm=1024, tn=gc,
                                    dil=d, seq=S, name=f"proj_dil_g{gi}"))
    gates = _norm_matmul(x2d, g_mix[l], w_in[l][:, QKV_WIDTH:], out_dtype=BF16, tm=1024, tn=1024,
                         bias=b_gate[l], name="proj_gates")
    dil_out = [_dilated_group(a, dil_bias[gi], f"dilated_g{gi}") for gi, a in enumerate(dil_qkv)]
    o_moba = _moba(tok.reshape(B, S, -1), moba_bias, first_col=gc).reshape(T, MOBA_HEADS * HEAD_DIM)
    x2d = _merge(dil_out, o_moba, gates, x2d, w_branch_a[l], w_branch_b[l], w_out[l], seq=S, tm=512)
    M = mem.shape[1]
    q_c = _norm_matmul(x2d, g_cross[l], w_q_cross[l], out_dtype=BF16, tm=1024, tn=512, name="cross_q")
    kv = _norm_matmul(mem.reshape(B * M, D), g_mem[l], w_kv_cross[l], out_dtype=BF16, tm=1024, tn=1024, name="cross_kv")
    x2d = _cross(q_c.reshape(B, S, -1), kv.reshape(B, M, -1), x2d.reshape(B, S, D), w_o_cross[l], tm=1024).reshape(T, D)
    q_p, hn = _norm_matmul(x2d, g_ffn[l], w_q_peer[l], out_dtype=F32, tm=1024, tn=1024, emit_normed=True, name="peer_q")
    thr, e1, s2, e2 = _peer_route(q_p, peer_sub_keys[l], tt=512)
    y = _peer_main(hn, peer_u[l], peer_v[l], thr, e1, s2, e2, x2d, g_final,
                   tt=512, rows_per_step=8)
    return y.reshape(B, S, D)
```

```python
import functools
import math

import jax
import jax.numpy as jnp
import numpy as np
from jax import lax
from jax.experimental import pallas as pl
from jax.experimental.pallas import tpu as pltpu

HEAD_DIM = 64
DIL_GROUPS = ((128, 1), (512, 4), (2048, 16))
DIL_HEADS = 4
DIL_BLOCK = 128
MOBA_HEADS = 8
MOBA_BLOCK = 256
MOBA_TOPK = 3
MOBA_BIAS_TILES = 8
MOBA_GROUP = 4
MOBA_MASK = -(2.0 ** 100)
REL_BUCKETS = 32
REL_MAX_DIST = 2048
CROSS_HEADS = 4
CROSS_HEAD_DIM = 128
PEER_HEADS = 8
PEER_N_KEYS = 128
PEER_TOPK = 16
RMS_EPS = 1e-6
NEG_INF = -1e30
DIL_GROUP_COLS = 3 * DIL_HEADS * HEAD_DIM
DIL_WIDTH = len(DIL_GROUPS) * DIL_GROUP_COLS
MOBA_WIDTH = 3 * MOBA_HEADS * HEAD_DIM
QKV_WIDTH = DIL_WIDTH + MOBA_WIDTH

LANES = 128
V7X_VMEM_BYTES = 64 * 1024 * 1024
VMEM_LIMIT = V7X_VMEM_BYTES * 3 // 4

BF16 = jnp.bfloat16
F32 = jnp.float32


def _params(*sem):
    return pltpu.CompilerParams(dimension_semantics=sem, vmem_limit_bytes=VMEM_LIMIT)


def _rms(x, g):
    return x * lax.rsqrt(jnp.mean(x * x, axis=-1, keepdims=True) + RMS_EPS) * g


def _nt_dot(a, b, precision=None):
    return lax.dot_general(a, b, (((1,), (1,)), ((), ())), preferred_element_type=F32, precision=precision)


def _norm_matmul_body(x_ref, g_ref, w_ref, *rest, sigmoid_bias, dil):
    rest = list(rest)
    b_ref = rest.pop(0) if sigmoid_bias else None
    o_ref, h_scr = rest[:2]
    rows = h_scr.shape[0] // dil

    @pl.when(pl.program_id(1) == 0)
    def _():
        if dil == 1:
            h_scr[...] = _rms(x_ref[...], g_ref[...]).astype(h_scr.dtype)
        else:
            cols_ref = rest[2]
            n_slab = cols_ref.shape[0]
            for c in range(n_slab):
                cols_ref[c] = x_ref[:, c * LANES:(c + 1) * LANES]
            for r in range(dil):
                x = jnp.concatenate([cols_ref[c, pl.ds(r, rows, stride=dil), :] for c in range(n_slab)], axis=-1)
                h_scr[r * rows:(r + 1) * rows, :] = _rms(x, g_ref[...]).astype(h_scr.dtype)

    acc = jnp.dot(h_scr[...], w_ref[...], preferred_element_type=F32)
    if sigmoid_bias:
        acc = jax.nn.sigmoid(acc + b_ref[...])
    if dil == 1:
        o_ref[...] = acc.astype(o_ref.dtype)
    else:
        for r in range(dil):
            o_ref[0, r] = acc[r * rows:(r + 1) * rows].astype(o_ref.dtype)


def _norm_matmul(x2d, g, w, *, out_dtype, tm, tn, bias=None, dil=1, seq=None, emit_normed=False, name):
    T, D = x2d.shape
    N = w.shape[1]
    tm = min(tm, T if seq is None else seq)
    assert T % tm == 0 and N % tn == 0
    in_specs = [
        pl.BlockSpec((tm, D), lambda i, j: (i, 0)),
        pl.BlockSpec((1, D), lambda i, j: (0, 0)),
        pl.BlockSpec((D, tn), lambda i, j: (0, j)),
    ]
    args = [x2d, g.reshape(1, D).astype(F32), w.astype(BF16)]
    if bias is not None:
        in_specs.append(pl.BlockSpec((1, tn), lambda i, j: (0, j)))
        args.append(bias.reshape(1, N).astype(F32))
    assert not (emit_normed and dil > 1)
    scratch = [] if emit_normed else [pltpu.VMEM((tm, D), BF16)]
    if dil == 1:
        out_spec = pl.BlockSpec((tm, tn), lambda i, j: (i, j))
        out_shape = jax.ShapeDtypeStruct((T, N), out_dtype)
    else:
        assert seq % tm == 0 and tm % (8 * dil) == 0 and D % LANES == 0
        per_seq = seq // tm
        out_spec = pl.BlockSpec((1, dil, tm // dil, tn), lambda i, j: (i // per_seq, 0, i % per_seq, j))
        out_shape = jax.ShapeDtypeStruct((T // seq, dil, seq // dil, N), out_dtype)
        scratch.append(pltpu.VMEM((D // LANES, tm, LANES), F32))
    if emit_normed:
        out_spec = [out_spec, pl.BlockSpec((tm, D), lambda i, j: (i, 0))]
        out_shape = [out_shape, jax.ShapeDtypeStruct((T, D), BF16)]
    return pl.pallas_call(
        functools.partial(_norm_matmul_body, sigmoid_bias=bias is not None, dil=dil),
        grid=(T // tm, N // tn),
        in_specs=in_specs,
        out_specs=out_spec,
        out_shape=out_shape,
        scratch_shapes=scratch,
        compiler_params=_params("parallel", "arbitrary"),
        name=name,
    )(*args)


def _t5_bucket_np(n):
    max_exact = REL_BUCKETS // 2
    nf = np.maximum(n, 1).astype(np.float32)
    large = max_exact + (np.log(nf / np.float32(max_exact)) / np.float32(math.log(REL_MAX_DIST / max_exact))
                         * np.float32(REL_BUCKETS - max_exact)).astype(np.int32)
    large = np.minimum(large, REL_BUCKETS - 1)
    return np.where(n < max_exact, n, large).astype(np.int32)


def _dil_bucket_map(d):
    blk = DIL_BLOCK
    rel = (blk + np.arange(blk))[:, None] - np.arange(2 * blk)[None, :]
    return _t5_bucket_np(np.clip(rel, 0, None) * d)


def _moba_bucket_map():
    blk = MOBA_BLOCK
    rel = np.arange(blk)[:, None] - np.arange(blk)[None, :]
    tiles = [_t5_bucket_np(np.clip(rel + dlt * blk, 0, None)) for dlt in range(MOBA_BIAS_TILES)]
    tiles[0] = np.where(rel >= 0, tiles[0], -1)
    far = _t5_bucket_np(np.arange((MOBA_BIAS_TILES - 2) * blk + 1, 64 * blk))
    assert (far == REL_BUCKETS - 1).all()
    return np.stack(tiles)


def _bias_lookup_body(idx_ref, tab_ref, o_ref):
    h = pl.program_id(0)
    idx = idx_ref[...]

    def bucket(b, acc):
        return jnp.where(idx == b, tab_ref[b, h], acc)

    o_ref[0] = lax.fori_loop(0, REL_BUCKETS, bucket, jnp.full(idx.shape, NEG_INF, F32))


def _bias_lookup(bucket_map, table, name):
    R, C = bucket_map.shape
    H = table.shape[1]
    rt = min(R, 256)
    assert R % rt == 0
    return pl.pallas_call(
        _bias_lookup_body,
        grid=(H, R // rt),
        in_specs=[pl.BlockSpec((rt, C), lambda h, i: (i, 0)), pl.BlockSpec(memory_space=pltpu.SMEM)],
        out_specs=pl.BlockSpec((1, rt, C), lambda h, i: (h, i, 0)),
        out_shape=jax.ShapeDtypeStruct((H, R, C), F32),
        compiler_params=_params("parallel", "parallel"),
        name=name,
    )(jnp.asarray(bucket_map), table.astype(F32))


def _dilated_body(q_ref, kp_ref, kc_ref, vp_ref, vc_ref, bias_ref, o_ref, *, qblocks):
    blk, hd = DIL_BLOCK, HEAD_DIM
    gw = DIL_HEADS * hd
    n = pl.program_id(2)
    k = jnp.concatenate([kp_ref[0, 0], kc_ref[0, 0]], axis=0)
    v = jnp.concatenate([vp_ref[0, 0], vc_ref[0, 0]], axis=0)
    lane = lax.broadcasted_iota(jnp.int32, (1, gw), 1)
    qi = lax.broadcasted_iota(jnp.int32, (blk, 2 * blk), 0)
    kj = lax.broadcasted_iota(jnp.int32, (blk, 2 * blk), 1)
    rel = blk + qi - kj
    band = (rel >= 0) & (rel <= blk)
    first_key = jnp.where(n > 0, 0, blk)
    for i in range(qblocks):
        rows = slice(i * blk, (i + 1) * blk)
        q = q_ref[0, 0, rows, :]
        kw = k[i * blk:(i + 2) * blk]
        vw = v[i * blk:(i + 2) * blk]
        mask = band & (kj >= first_key) if i == 0 else band
        head_lanes = [(lane >= h * hd) & (lane < (h + 1) * hd) for h in range(DIL_HEADS)]
        q_all = jnp.concatenate([jnp.where(hl, q, jnp.zeros_like(q)) for hl in head_lanes], axis=0)
        s = _nt_dot(q_all, kw) * (hd ** -0.5) + bias_ref[...]
        s = jnp.where(jnp.concatenate([mask] * DIL_HEADS, axis=0), s, NEG_INF)
        m = jnp.max(s, axis=-1, keepdims=True)
        p = jnp.exp(s - m)
        l = jnp.sum(p, axis=-1, keepdims=True)
        pv = jnp.dot(p.astype(BF16), vw, preferred_element_type=F32) / l
        lse_all = m + jnp.log(l)
        out = jnp.zeros((blk, gw), F32)
        lse = jnp.zeros((blk, gw), F32)
        for h, hl in enumerate(head_lanes):
            out = jnp.where(hl, pv[h * blk:(h + 1) * blk], out)
            lse = jnp.where(hl, lse_all[h * blk:(h + 1) * blk], lse)
        o_ref[0, 0, rows, :gw] = out
        o_ref[0, 0, rows, gw:] = lse


def _dilated_group(qkv, bias, name):
    B, d, L, _ = qkv.shape
    blk, gw = DIL_BLOCK, DIL_HEADS * HEAD_DIM
    assert L % blk == 0
    qblocks = min(4, L // blk)
    assert L % (qblocks * blk) == 0
    rows = qblocks * blk

    def spec(j, prev):
        if prev:
            return pl.BlockSpec((1, 1, blk, gw), lambda b, r, n: (b, r, jnp.maximum(n * qblocks - 1, 0), j))
        return pl.BlockSpec((1, 1, rows, gw), lambda b, r, n: (b, r, n, j))

    return pl.pallas_call(
        functools.partial(_dilated_body, qblocks=qblocks),
        grid=(B, d, L // rows),
        in_specs=[spec(0, False), spec(1, True), spec(1, False), spec(2, True), spec(2, False),
                  pl.BlockSpec((DIL_HEADS * blk, 2 * blk), lambda b, r, n: (0, 0))],
        out_specs=pl.BlockSpec((1, 1, rows, 2 * gw), lambda b, r, n: (b, r, n, 0)),
        out_shape=jax.ShapeDtypeStruct((B, d, L, 2 * gw), F32),
        compiler_params=_params("parallel", "parallel", "arbitrary"),
        name=name,
    )(qkv, qkv, qkv, qkv, qkv, bias.reshape(DIL_HEADS * blk, 2 * blk))


def _moba_body(q_ref, k_ref, v_ref, bias_ref, o_ref, qa_ref, ka_ref, *, S):
    blk, hd = MOBA_BLOCK, HEAD_DIM
    nblk = S // blk
    scale = hd ** -0.5
    lane = lax.broadcasted_iota(jnp.int32, (1, 2 * hd), 1)
    kmean = jnp.mean(k_ref[0].astype(F32).reshape(nblk, blk, 2 * hd), axis=1)
    grp = MOBA_GROUP * blk
    row_blk = lax.broadcasted_iota(jnp.int32, (S, 2 * hd), 0) // blk
    q2, k2 = q_ref[0], k_ref[0]

    ka_ref[:, :2 * hd] = k2
    ka_ref[:, 2 * hd:] = jnp.where(lane == row_blk, 1.0, 0.0).astype(BF16)
    blk_t = lax.broadcasted_iota(jnp.int32, (nblk, S), 0)
    q_blk_t = lax.broadcasted_iota(jnp.int32, (nblk, S), 1) // blk
    past = blk_t < q_blk_t
    blk_f = blk_t.astype(F32)
    for h in range(2):
        head_lanes = (lane >= h * hd) & (lane < (h + 1) * hd)
        qh = jnp.where(head_lanes, q2, jnp.zeros_like(q2))
        gs = _nt_dot(jnp.where(head_lanes, kmean, 0.0), qh.astype(F32), precision=lax.Precision.HIGHEST)
        gs = jnp.where(past, gs, NEG_INF)
        sel = jnp.zeros(gs.shape, F32)
        for _r in range(MOBA_TOPK):
            gmax = jnp.max(gs, axis=0, keepdims=True)
            first = jnp.min(jnp.where(gs == gmax, blk_f, float(nblk)), axis=0, keepdims=True)
            pick = blk_f == first
            sel = jnp.where(pick & past, 1.0, sel)
            gs = jnp.where(pick, -jnp.inf, gs)
        penalty_t = jnp.where((sel > 0.5) | (blk_t == q_blk_t), 0.0, MOBA_MASK)
        penalty_t = jnp.concatenate([penalty_t, jnp.zeros((2 * hd - nblk, S), F32)], axis=0)
        qa_ref[h, :, :2 * hd] = qh * scale
        qa_ref[h, :, 2 * hd:] = penalty_t.T.astype(BF16)

    def q_block(qb, _):
        row0 = pl.multiple_of(qb * blk, blk)
        n_grp = qb // MOBA_GROUP + 1
        qa = jnp.concatenate([qa_ref[h, pl.ds(row0, blk), :] for h in range(2)], axis=0)

        def key_group(i, carry):
            m, l, acc = carry
            g = n_grp - 1 - i
            col0 = pl.multiple_of(g * grp, grp)
            s = _nt_dot(qa, ka_ref[pl.ds(col0, grp), :])
            parts = []
            for jj in range(MOBA_GROUP):
                tile = jnp.clip(qb - (g * MOBA_GROUP + jj), 0, MOBA_BIAS_TILES - 1)
                bias = jnp.concatenate([bias_ref[0, h, tile] for h in range(2)], axis=0)
                parts.append(s[:, jj * blk:(jj + 1) * blk] + bias)
            m_new = jnp.maximum(m, jnp.max(functools.reduce(jnp.maximum, parts), axis=-1, keepdims=True))
            a = jnp.exp(m - m_new)
            ps = [jnp.exp(p - m_new) for p in parts]
            l = l * a + jnp.sum(functools.reduce(jnp.add, ps), axis=-1, keepdims=True)
            p_all = jnp.concatenate([p.astype(BF16) for p in ps], axis=-1)
            pv = jnp.dot(p_all, v_ref[0, pl.ds(col0, grp), :], preferred_element_type=F32)
            return m_new, l, acc * a + pv

        init = (jnp.full((2 * blk, 1), -jnp.inf, F32), jnp.zeros((2 * blk, 1), F32), jnp.zeros((2 * blk, 2 * hd), F32))
        _, l, acc = lax.fori_loop(0, n_grp, key_group, init)
        o = acc / l
        o_ref[0, pl.ds(row0, blk), :] = jnp.where(lane < hd, o[:blk], o[blk:]).astype(o_ref.dtype)
        return 0

    lax.fori_loop(0, nblk, q_block, 0)


def _moba(qkv, bias, first_col):
    B, S, C = qkv.shape
    pw = 2 * HEAD_DIM
    npair = MOBA_HEADS // 2
    assert S % (MOBA_GROUP * MOBA_BLOCK) == 0 and S // MOBA_BLOCK <= HEAD_DIM and first_col % pw == 0
    c0 = first_col // pw

    def spec(j):
        return pl.BlockSpec((1, S, pw), lambda hp, b: (b, 0, c0 + j * npair + hp))

    return pl.pallas_call(
        functools.partial(_moba_body, S=S),
        grid=(npair, B),
        in_specs=[spec(0), spec(1), spec(2),
                  pl.BlockSpec((1, 2, MOBA_BIAS_TILES, MOBA_BLOCK, MOBA_BLOCK), lambda hp, b: (hp, 0, 0, 0, 0))],
        out_specs=pl.BlockSpec((1, S, pw), lambda hp, b: (b, 0, hp)),
        out_shape=jax.ShapeDtypeStruct((B, S, MOBA_HEADS * HEAD_DIM), BF16),
        scratch_shapes=[pltpu.VMEM((2, S, 2 * pw), BF16), pltpu.VMEM((S, 2 * pw), BF16)],
        compiler_params=_params("parallel", "parallel"),
        name="moba",
    )(qkv, qkv, qkv, bias)


def _merge_body(d0_ref, d1_ref, d2_ref, om_ref, gate_ref, x_ref, wa_ref, wb_ref, wo_ref, x1_ref, t1_ref, t2_ref):
    D = x_ref.shape[-1]
    gw = DIL_HEADS * HEAD_DIM
    n_slab = t1_ref.shape[0]
    for src, dst in ((d1_ref, t1_ref), (d2_ref, t2_ref)):
        dil, rows = src.shape[1], src.shape[2]
        for r in range(dil):
            for c in range(n_slab):
                dst[c, pl.ds(r, rows, stride=dil), :] = src[0, r, :, c * LANES:(c + 1) * LANES]
    g0 = d0_ref[...]
    g1 = jnp.concatenate([t1_ref[c] for c in range(n_slab)], axis=-1)
    g2 = jnp.concatenate([t2_ref[c] for c in range(n_slab)], axis=-1)
    l0, l1, l2 = g0[:, gw:], g1[:, gw:], g2[:, gw:]
    mx = jnp.maximum(jnp.maximum(l0, l1), l2)
    e0, e1, e2 = jnp.exp(l0 - mx), jnp.exp(l1 - mx), jnp.exp(l2 - mx)
    o_dil = (e0 * g0[:, :gw] + e1 * g1[:, :gw] + e2 * g2[:, :gw]) / (e0 + e1 + e2)
    a = jnp.dot(o_dil.astype(BF16), wa_ref[...], preferred_element_type=F32)
    b = jnp.dot(om_ref[...], wb_ref[...], preferred_element_type=F32)
    merged = gate_ref[:, :D].astype(F32) * a + gate_ref[:, D:].astype(F32) * b
    x1_ref[...] = x_ref[...] + jnp.dot(merged.astype(BF16), wo_ref[...], preferred_element_type=F32)


def _merge(dil_out, o_moba, gates, x2d, w_a, w_b, w_o, seq, tm):
    T, D = x2d.shape
    tm = min(tm, seq)
    per_seq = seq // tm
    assert seq % tm == 0 and dil_out[0].shape[1] == 1

    def rows(width):
        return pl.BlockSpec((tm, width), lambda i: (i, 0))

    def whole(w):
        return pl.BlockSpec(w.shape, lambda i: (0, 0))

    def residue_major(a):
        d, width = a.shape[1], a.shape[3]
        assert tm % (8 * d) == 0
        return pl.BlockSpec((1, d, tm // d, width), lambda i: (i // per_seq, 0, i % per_seq, 0))

    ws = [w_a.astype(BF16), w_b.astype(BF16), w_o.astype(BF16)]
    width = dil_out[0].shape[-1]
    return pl.pallas_call(
        _merge_body,
        grid=(T // tm,),
        in_specs=[rows(width), residue_major(dil_out[1]), residue_major(dil_out[2]),
                  rows(o_moba.shape[-1]), rows(2 * D), rows(D)] + [whole(w) for w in ws],
        out_specs=rows(D),
        out_shape=jax.ShapeDtypeStruct((T, D), F32),
        scratch_shapes=[pltpu.VMEM((width // LANES, tm, LANES), F32)] * 2,
        compiler_params=_params("parallel"),
        name="merge",
    )(dil_out[0].reshape(T, width), dil_out[1], dil_out[2], o_moba, gates, x2d, *ws)


def _cross_body(q_ref, kv_ref, x_ref, wo_ref, x2_ref):
    hd = CROSS_HEAD_DIM
    width = CROSS_HEADS * hd
    heads = []
    for h in range(CROSS_HEADS):
        sl = slice(h * hd, (h + 1) * hd)
        s = _nt_dot(q_ref[0, :, sl], kv_ref[0, :, sl]) * (hd ** -0.5)
        m = jnp.max(s, axis=-1, keepdims=True)
        p = jnp.exp(s - m)
        l = jnp.sum(p, axis=-1, keepdims=True)
        v = kv_ref[0, :, width + h * hd: width + (h + 1) * hd]
        heads.append((jnp.dot(p.astype(BF16), v, preferred_element_type=F32) / l).astype(BF16))
    o = jnp.concatenate(heads, axis=-1)
    x2_ref[0] = x_ref[0] + jnp.dot(o, wo_ref[...], preferred_element_type=F32)


def _cross(q, kv, x, w_o, tm):
    B, S, D = x.shape
    M = kv.shape[1]
    tm = min(tm, S)
    w_o = w_o.astype(BF16)
    return pl.pallas_call(
        _cross_body,
        grid=(B, S // tm),
        in_specs=[pl.BlockSpec((1, tm, q.shape[-1]), lambda b, i: (b, i, 0)),
                  pl.BlockSpec((1, M, kv.shape[-1]), lambda b, i: (b, 0, 0)),
                  pl.BlockSpec((1, tm, D), lambda b, i: (b, i, 0)),
                  pl.BlockSpec(w_o.shape, lambda b, i: (0, 0))],
        out_specs=pl.BlockSpec((1, tm, D), lambda b, i: (b, i, 0)),
        out_shape=jax.ShapeDtypeStruct((B, S, D), F32),
        compiler_params=_params("parallel", "parallel"),
        name="cross",
    )(q, kv, x, w_o)


def _merge_exchange_pairs(n):
    pairs = []
    p = 1
    while p < n:
        k = p
        while k >= 1:
            for j in range(k % p, n - k, 2 * k):
                for i in range(min(k, n - j - k)):
                    if (i + j) // (2 * p) == (i + j + k) // (2 * p):
                        pairs.append((i + j, i + j + k))
            k //= 2
        p *= 2
    return pairs


def _top16(s):
    n, sub = 16, 8
    assert s.shape[0] % sub == 0 and sub * n >= s.shape[0] >= 2 * sub
    v = [s[k * sub:(k + 1) * sub] for k in range(s.shape[0] // sub)] + [None] * (n - s.shape[0] // sub)

    def exchange(i, j):
        if v[j] is None:
            return
        if v[i] is None:
            v[i], v[j] = v[j], None
        else:
            v[i], v[j] = jnp.maximum(v[i], v[j]), jnp.minimum(v[i], v[j])

    for i, j in _merge_exchange_pairs(n):
        exchange(i, j)
    for shift in (4, 2, 1):
        w = [None if x is None else pltpu.roll(x, shift, 0) for x in v]
        for k in range(n):
            other = w[n - 1 - k]
            if other is not None:
                v[k] = other if v[k] is None else jnp.maximum(v[k], other)
        d = n // 2
        while d >= 1:
            for i in range(n):
                if i & d == 0:
                    exchange(i, i + d)
            d //= 2
    return v


def _top_sorted(s):
    v = _top16(s)
    nxt = jnp.max(jnp.where(s < v[-1][0:1], s, -jnp.inf), axis=0, keepdims=True)
    return jnp.concatenate([x[0:1] for x in v] + [nxt], axis=0)


def _pair_sum_candidates(a, b):
    assert PEER_TOPK == 16
    row = lax.broadcasted_iota(jnp.int32, (PEER_TOPK, 1), 0)
    blocks = [a[0:1] + b]
    blocks += [a[i:i + 1] + b[0:8] for i in (1, 2, 3)]
    blocks += [jnp.where(row >= 4, b[0:1] + a, -jnp.inf)]
    blocks += [jnp.where(row[0:8] >= 4, b[j:j + 1] + a[0:8], -jnp.inf) for j in (1, 2)]
    return jnp.concatenate(blocks, axis=0)


def _peer_route_body(q_ref, keys_ref, thr_ref, e1_ref, s2_ref, e2_ref):
    nk, K = PEER_N_KEYS, PEER_TOPK
    kd = keys_ref.shape[-1]
    for h in range(PEER_HEADS):
        sc = []
        for p in range(2):
            c0 = (2 * h + p) * kd
            sc.append(_nt_dot(keys_ref[p], q_ref[:, c0:c0 + kd], precision=lax.Precision.HIGHEST))
        a = _top_sorted(sc[0])
        b = _top_sorted(sc[1])
        cand = _pair_sum_candidates(a[:K], b[:K])
        tau = _top16(cand)[K - 1][0:1]
        below = jnp.max(jnp.where(cand < tau, cand, -jnp.inf), axis=0, keepdims=True)
        for corner in (a[0:1] + b[K:K + 1], a[K:K + 1] + b[0:1]):
            below = jnp.maximum(below, jnp.where(corner < tau, corner, -jnp.inf))
        below = jnp.where(below == -jnp.inf, tau, below)
        cut = 0.5 * (tau + below)
        top = a[0:1] + b[0:1]
        z = jnp.sum(jnp.where(cand >= tau, jnp.exp(cand - top), 0.0), axis=0, keepdims=True)
        thr = cut - sc[0]
        e1 = jnp.exp(sc[0] - a[0:1]) * (0.5 / z)
        for c in range(thr_ref.shape[1]):
            thr_ref[h, c] = thr[:, c * LANES:(c + 1) * LANES]
            e1_ref[h, c] = e1[:, c * LANES:(c + 1) * LANES]
        s2_ref[h] = sc[1]
        e2_ref[h] = jnp.exp(sc[1] - b[0:1])


def _peer_route(q, sub_keys, tt):
    T = q.shape[0]
    tt = min(tt, T)
    nk = PEER_N_KEYS
    assert tt % LANES == 0
    big = jax.ShapeDtypeStruct((PEER_HEADS, nk, T), F32)
    big_spec = pl.BlockSpec((PEER_HEADS, nk, tt), lambda i: (0, 0, i))
    slab = jax.ShapeDtypeStruct((PEER_HEADS, T // LANES, nk, LANES), F32)
    slab_spec = pl.BlockSpec((PEER_HEADS, tt // LANES, nk, LANES), lambda i: (0, i, 0, 0))
    return pl.pallas_call(
        _peer_route_body,
        grid=(T // tt,),
        in_specs=[pl.BlockSpec((tt, q.shape[1]), lambda i: (i, 0)),
                  pl.BlockSpec(sub_keys.shape, lambda i: (0, 0, 0))],
        out_specs=[slab_spec, slab_spec, big_spec, big_spec],
        out_shape=[slab, slab, big, big],
        compiler_params=_params("parallel"),
        name="peer_route",
    )(q, sub_keys.astype(F32))


def _peer_main_body(h_ref, u_ref, vt_ref, thr_ref, e1_ref, s2_ref, e2_ref, x_ref, g_ref,
                    o_ref, acc_ref, pre_ref, ga_ref, ht_ref, *, rows_per_step):
    nk = PEER_N_KEYS
    lanes = 128
    j = pl.program_id(1)
    tt = h_ref.shape[0]

    @pl.when(j == 0)
    def _():
        acc_ref[...] = jnp.zeros_like(acc_ref)
        ht_ref[...] = h_ref[...].astype(F32).T.astype(ht_ref.dtype)

    i1_base = pl.multiple_of(j * rows_per_step, rows_per_step)
    group = 4
    sub = 64
    n_sub = nk // sub

    pre_ref[...] = jnp.dot(u_ref[...], ht_ref[...], preferred_element_type=F32)

    def gate_tile(ti, carry):
        cs = pl.ds(pl.multiple_of((ti // n_sub) * lanes, lanes), lanes)
        k0 = pl.multiple_of((ti % n_sub) * sub, sub)
        for r0 in range(0, rows_per_step, group):
            gates = [None] * group
            for h in range(PEER_HEADS):
                s2 = s2_ref[h, pl.ds(k0, sub), cs]
                e2 = e2_ref[h, pl.ds(k0, sub), cs]
                for g in range(group):
                    r = r0 + g
                    thr = thr_ref[h, ti // n_sub, pl.ds(i1_base + r, 8, stride=0), :]
                    e1 = e1_ref[h, ti // n_sub, pl.ds(i1_base + r, 8, stride=0), :]
                    w = (jnp.where(s2.reshape(sub // 8, 8, lanes) >= thr, e2.reshape(sub // 8, 8, lanes), 0.0)
                         * e1).reshape(sub, lanes)
                    gates[g] = w if gates[g] is None else gates[g] + w
            for g in range(group):
                rs = pl.ds(pl.multiple_of((r0 + g) * nk + k0, sub), sub)
                pre = pre_ref[rs, cs]
                act = pre * (1.0 + lax.erf(pre * (2.0 ** -0.5)))
                ga_ref[rs, cs] = (gates[g] * act).astype(ga_ref.dtype)
        return carry

    lax.fori_loop(0, (tt // lanes) * n_sub, gate_tile, 0)
    acc_ref[...] += jnp.dot(vt_ref[0], ga_ref[...], preferred_element_type=F32)

    @pl.when(j == pl.num_programs(1) - 1)
    def _():
        o_ref[...] = _rms(x_ref[...] + acc_ref[...].T, g_ref[...])


def _peer_main(hn, u, v, thr, e1, s2, e2, x2d, g_final, tt, rows_per_step):
    T, D = x2d.shape
    tt = min(tt, T)
    nk = PEER_N_KEYS
    eb = rows_per_step * nk
    n_exp = u.shape[0]
    assert n_exp == nk * nk and nk % rows_per_step == 0
    u = u.astype(BF16)
    vt = v.astype(BF16).reshape(n_exp // eb, eb, D).transpose(0, 2, 1)
    score_spec = pl.BlockSpec((PEER_HEADS, nk, tt), lambda i, j: (0, 0, i))
    slab_spec = pl.BlockSpec((PEER_HEADS, tt // LANES, nk, LANES), lambda i, j: (0, i, 0, 0))
    return pl.pallas_call(
        functools.partial(_peer_main_body, rows_per_step=rows_per_step),
        grid=(T // tt, n_exp // eb),
        in_specs=[pl.BlockSpec((tt, D), lambda i, j: (i, 0)),
                  pl.BlockSpec((eb, D), lambda i, j: (j, 0)),
                  pl.BlockSpec((1, D, eb), lambda i, j: (j, 0, 0)),
                  slab_spec, slab_spec, score_spec, score_spec,
                  pl.BlockSpec((tt, D), lambda i, j: (i, 0)),
                  pl.BlockSpec((1, D), lambda i, j: (0, 0))],
        out_specs=pl.BlockSpec((tt, D), lambda i, j: (i, 0)),
        out_shape=jax.ShapeDtypeStruct((T, D), F32),
        scratch_shapes=[pltpu.VMEM((D, tt), F32), pltpu.VMEM((eb, tt), F32), pltpu.VMEM((eb, tt), BF16),
                        pltpu.VMEM((D, tt), BF16)],
        compiler_params=_params("parallel", "arbitrary"),
        name="peer_main",
    )(hn, u, vt, thr, e1, s2, e2, x2d, g_final.reshape(1, D).astype(F32))


def kernel(x, mem, rel_bias, g_mix, w_in, b_gate, w_branch_a, w_branch_b, w_out, g_cross, g_mem, w_q_cross,
           w_kv_cross, w_o_cross, g_ffn, w_q_peer, peer_sub_keys, peer_u, peer_v, g_final):
    B, S, D = x.shape
    T = B * S
    depth = w_in.shape[0]
    ndil = len(DIL_GROUPS) * DIL_HEADS
    x2d = x.reshape(T, D)

    dil_bias = [_bias_lookup(_dil_bucket_map(d), rel_bias[:, gi * DIL_HEADS:(gi + 1) * DIL_HEADS], f"dil_bias_g{gi}")
                for gi, (_, d) in enumerate(DIL_GROUPS)]
    mmap = _moba_bucket_map()
    moba_bias = _bias_lookup(mmap.reshape(-1, MOBA_BLOCK), rel_bias[:, ndil:], "moba_bias")
    moba_bias = moba_bias.reshape(MOBA_HEADS // 2, 2, MOBA_BIAS_TILES, MOBA_BLOCK, MOBA_BLOCK)

    assert depth == 1, "the final norm is fused into the last PEER sweep; one layer supported"
    l = 0
    gc = DIL_GROUP_COLS
    assert DIL_GROUPS[0][1] == 1
    w_tok = jnp.concatenate([w_in[l][:, :gc], w_in[l][:, DIL_WIDTH:QKV_WIDTH]], axis=1)
    tok = _norm_matmul(x2d, g_mix[l], w_tok, out_dtype=BF16, tm=1024, tn=gc, name="proj_tok")
    dil_qkv = [tok.reshape(B, 1, S, -1)]
    for gi, (_, d) in enumerate(DIL_GROUPS[1:], start=1):
        dil_qkv.append(_norm_matmul(x2d, g_mix[l], w_in[l][:, gi * gc:(gi + 1) * gc], out_dtype=BF16, tm=1024, tn=gc,
                                    dil=d, seq=S, name=f"proj_dil_g{gi}"))
    gates = _norm_matmul(x2d, g_mix[l], w_in[l][:, QKV_WIDTH:], out_dtype=BF16, tm=1024, tn=1024,
                         bias=b_gate[l], name="proj_gates")
    dil_out = [_dilated_group(a, dil_bias[gi], f"dilated_g{gi}") for gi, a in enumerate(dil_qkv)]
    o_moba = _moba(tok.reshape(B, S, -1), moba_bias, first_col=gc).reshape(T, MOBA_HEADS * HEAD_DIM)
    x2d = _merge(dil_out, o_moba, gates, x2d, w_branch_a[l], w_branch_b[l], w_out[l], seq=S, tm=512)
    M = mem.shape[1]
    q_c = _norm_matmul(x2d, g_cross[l], w_q_cross[l], out_dtype=BF16, tm=1024, tn=512, name="cross_q")
    kv = _norm_matmul(mem.reshape(B * M, D), g_mem[l], w_kv_cross[l], out_dtype=BF16, tm=1024, tn=1024, name="cross_kv")
    x2d = _cross(q_c.reshape(B, S, -1), kv.reshape(B, M, -1), x2d.reshape(B, S, D), w_o_cross[l], tm=1024).reshape(T, D)
    q_p, hn = _norm_matmul(x2d, g_ffn[l], w_q_peer[l], out_dtype=F32, tm=1024, tn=1024, emit_normed=True, name="peer_q")
    thr, e1, s2, e2 = _peer_route(q_p, peer_sub_keys[l], tt=512)
    y = _peer_main(hn, peer_u[l], peer_v[l], thr, e1, s2, e2, x2d, g_final,
                   tt=512, rows_per_step=8)
    return y.reshape(B, S, D)
```

```python
import functools
import math

import jax
import jax.numpy as jnp
import numpy as np
from jax import lax
from jax.experimental import pallas as pl
from jax.experimental.pallas import tpu as pltpu

HEAD_DIM = 64
DIL_GROUPS = ((128, 1), (512, 4), (2048, 16))
DIL_HEADS = 4
DIL_BLOCK = 128
MOBA_HEADS = 8
MOBA_BLOCK = 256
MOBA_TOPK = 3
MOBA_BIAS_TILES = 8
MOBA_GROUP = 4
MOBA_SPLIT = 2
MOBA_MASK = -(2.0 ** 100)
REL_BUCKETS = 32
REL_MAX_DIST = 2048
CROSS_HEADS = 4
CROSS_HEAD_DIM = 128
PEER_HEADS = 8
PEER_N_KEYS = 128
PEER_TOPK = 16
RMS_EPS = 1e-6
NEG_INF = -1e30
DIL_GROUP_COLS = 3 * DIL_HEADS * HEAD_DIM
DIL_WIDTH = len(DIL_GROUPS) * DIL_GROUP_COLS
MOBA_WIDTH = 3 * MOBA_HEADS * HEAD_DIM
QKV_WIDTH = DIL_WIDTH + MOBA_WIDTH

LANES = 128
V7X_VMEM_BYTES = 64 * 1024 * 1024
VMEM_LIMIT = V7X_VMEM_BYTES * 3 // 4

BF16 = jnp.bfloat16
F32 = jnp.float32


def _params(*sem):
    return pltpu.CompilerParams(dimension_semantics=sem, vmem_limit_bytes=VMEM_LIMIT)


def _rms(x, g):
    return x * lax.rsqrt(jnp.mean(x * x, axis=-1, keepdims=True) + RMS_EPS) * g


def _nt_dot(a, b, precision=None):
    return lax.dot_general(a, b, (((1,), (1,)), ((), ())), preferred_element_type=F32, precision=precision)


def _norm_matmul_body(x_ref, g_ref, w_ref, *rest, sigmoid_bias, dil):
    rest = list(rest)
    b_ref = rest.pop(0) if sigmoid_bias else None
    o_ref, h_scr = rest[:2]
    rows = h_scr.shape[0] // dil

    @pl.when(pl.program_id(1) == 0)
    def _():
        if dil == 1:
            h_scr[...] = _rms(x_ref[...], g_ref[...]).astype(h_scr.dtype)
        else:
            cols_ref = rest[2]
            n_slab = cols_ref.shape[0]
            for c in range(n_slab):
                cols_ref[c] = x_ref[:, c * LANES:(c + 1) * LANES]
            for r in range(dil):
                x = jnp.concatenate([cols_ref[c, pl.ds(r, rows, stride=dil), :] for c in range(n_slab)], axis=-1)
                h_scr[r * rows:(r + 1) * rows, :] = _rms(x, g_ref[...]).astype(h_scr.dtype)

    acc = jnp.dot(h_scr[...], w_ref[...], preferred_element_type=F32)
    if sigmoid_bias:
        acc = jax.nn.sigmoid(acc + b_ref[...])
    if dil == 1:
        o_ref[...] = acc.astype(o_ref.dtype)
    else:
        for r in range(dil):
            o_ref[0, r] = acc[r * rows:(r + 1) * rows].astype(o_ref.dtype)


def _norm_matmul(x2d, g, w, *, out_dtype, tm, tn, bias=None, dil=1, seq=None, emit_normed=False, name):
    T, D = x2d.shape
    N = w.shape[1]
    tm = min(tm, T if seq is None else seq)
    assert T % tm == 0 and N % tn == 0
    in_specs = [
        pl.BlockSpec((tm, D), lambda i, j: (i, 0)),
        pl.BlockSpec((1, D), lambda i, j: (0, 0)),
        pl.BlockSpec((D, tn), lambda i, j: (0, j)),
    ]
    args = [x2d, g.reshape(1, D).astype(F32), w.astype(BF16)]
    if bias is not None:
        in_specs.append(pl.BlockSpec((1, tn), lambda i, j: (0, j)))
        args.append(bias.reshape(1, N).astype(F32))
    assert not (emit_normed and dil > 1)
    scratch = [] if emit_normed else [pltpu.VMEM((tm, D), BF16)]
    if dil == 1:
        out_spec = pl.BlockSpec((tm, tn), lambda i, j: (i, j))
        out_shape = jax.ShapeDtypeStruct((T, N), out_dtype)
    else:
        assert seq % tm == 0 and tm % (8 * dil) == 0 and D % LANES == 0
        per_seq = seq // tm
        out_spec = pl.BlockSpec((1, dil, tm // dil, tn), lambda i, j: (i // per_seq, 0, i % per_seq, j))
        out_shape = jax.ShapeDtypeStruct((T // seq, dil, seq // dil, N), out_dtype)
        scratch.append(pltpu.VMEM((D // LANES, tm, LANES), F32))
    if emit_normed:
        out_spec = [out_spec, pl.BlockSpec((tm, D), lambda i, j: (i, 0))]
        out_shape = [out_shape, jax.ShapeDtypeStruct((T, D), BF16)]
    return pl.pallas_call(
        functools.partial(_norm_matmul_body, sigmoid_bias=bias is not None, dil=dil),
        grid=(T // tm, N // tn),
        in_specs=in_specs,
        out_specs=out_spec,
        out_shape=out_shape,
        scratch_shapes=scratch,
        compiler_params=_params("parallel", "arbitrary"),
        name=name,
    )(*args)


def _t5_bucket_np(n):
    max_exact = REL_BUCKETS // 2
    nf = np.maximum(n, 1).astype(np.float32)
    large = max_exact + (np.log(nf / np.float32(max_exact)) / np.float32(math.log(REL_MAX_DIST / max_exact))
                         * np.float32(REL_BUCKETS - max_exact)).astype(np.int32)
    large = np.minimum(large, REL_BUCKETS - 1)
    return np.where(n < max_exact, n, large).astype(np.int32)


def _dil_bucket_map(d):
    blk = DIL_BLOCK
    rel = (blk + np.arange(blk))[:, None] - np.arange(2 * blk)[None, :]
    return _t5_bucket_np(np.clip(rel, 0, None) * d)


def _moba_bucket_map():
    blk = MOBA_BLOCK
    rel = np.arange(blk)[:, None] - np.arange(blk)[None, :]
    tiles = [_t5_bucket_np(np.clip(rel + dlt * blk, 0, None)) for dlt in range(MOBA_BIAS_TILES)]
    tiles[0] = np.where(rel >= 0, tiles[0], -1)
    far = _t5_bucket_np(np.arange((MOBA_BIAS_TILES - 2) * blk + 1, 64 * blk))
    assert (far == REL_BUCKETS - 1).all()
    return np.stack(tiles)


def _bias_lookup_body(idx_ref, range_ref, tab_ref, o_ref):
    h, i = pl.program_id(0), pl.program_id(1)
    idx = idx_ref[...]

    def bucket(b, acc):
        return jnp.where(idx == b, tab_ref[b, h], acc)

    o_ref[0] = lax.fori_loop(range_ref[i, 0], range_ref[i, 1], bucket, jnp.full(idx.shape, NEG_INF, F32))


def _bias_lookup(bucket_map, table, name):
    R, C = bucket_map.shape
    H = table.shape[1]
    rt = min(R, 256)
    assert R % rt == 0
    tiles = bucket_map.reshape(R // rt, rt * C)
    ranges = np.stack([np.where(tiles >= 0, tiles, REL_BUCKETS).min(axis=1), tiles.max(axis=1) + 1], axis=1)
    return pl.pallas_call(
        _bias_lookup_body,
        grid=(H, R // rt),
        in_specs=[pl.BlockSpec((rt, C), lambda h, i: (i, 0)), pl.BlockSpec(memory_space=pltpu.SMEM),
                  pl.BlockSpec(memory_space=pltpu.SMEM)],
        out_specs=pl.BlockSpec((1, rt, C), lambda h, i: (h, i, 0)),
        out_shape=jax.ShapeDtypeStruct((H, R, C), F32),
        compiler_params=_params("parallel", "parallel"),
        name=name,
    )(jnp.asarray(bucket_map), jnp.asarray(ranges.astype(np.int32)), table.astype(F32))


def _dilated_body(q_ref, kp_ref, kc_ref, vp_ref, vc_ref, bias_ref, o_ref, *, qblocks):
    blk, hd = DIL_BLOCK, HEAD_DIM
    gw = DIL_HEADS * hd
    n = pl.program_id(2)
    k = jnp.concatenate([kp_ref[0, 0], kc_ref[0, 0]], axis=0)
    v = jnp.concatenate([vp_ref[0, 0], vc_ref[0, 0]], axis=0)
    lane = lax.broadcasted_iota(jnp.int32, (1, gw), 1)
    qi = lax.broadcasted_iota(jnp.int32, (blk, 2 * blk), 0)
    kj = lax.broadcasted_iota(jnp.int32, (blk, 2 * blk), 1)
    rel = blk + qi - kj
    band = (rel >= 0) & (rel <= blk)
    first_key = jnp.where(n > 0, 0, blk)
    for i in range(qblocks):
        rows = slice(i * blk, (i + 1) * blk)
        q = q_ref[0, 0, rows, :]
        kw = k[i * blk:(i + 2) * blk]
        vw = v[i * blk:(i + 2) * blk]
        mask = band & (kj >= first_key) if i == 0 else band
        head_lanes = [(lane >= h * hd) & (lane < (h + 1) * hd) for h in range(DIL_HEADS)]
        q_all = jnp.concatenate([jnp.where(hl, q, jnp.zeros_like(q)) for hl in head_lanes], axis=0)
        s = _nt_dot(q_all, kw) * (hd ** -0.5) + bias_ref[...]
        s = jnp.where(jnp.concatenate([mask] * DIL_HEADS, axis=0), s, NEG_INF)
        m = jnp.max(s, axis=-1, keepdims=True)
        p = jnp.exp(s - m)
        l = jnp.sum(p, axis=-1, keepdims=True)
        pv = jnp.dot(p.astype(BF16), vw, preferred_element_type=F32) / l
        lse_all = m + jnp.log(l)
        out = jnp.zeros((blk, gw), F32)
        lse = jnp.zeros((blk, gw), F32)
        for h, hl in enumerate(head_lanes):
            out = jnp.where(hl, pv[h * blk:(h + 1) * blk], out)
            lse = jnp.where(hl, lse_all[h * blk:(h + 1) * blk], lse)
        o_ref[0, 0, rows, :gw] = out
        o_ref[0, 0, rows, gw:] = lse


def _dilated_group(qkv, bias, name):
    B, d, L, _ = qkv.shape
    blk, gw = DIL_BLOCK, DIL_HEADS * HEAD_DIM
    assert L % blk == 0
    qblocks = min(4, L // blk)
    assert L % (qblocks * blk) == 0
    rows = qblocks * blk

    def spec(j, prev):
        if prev:
            return pl.BlockSpec((1, 1, blk, gw), lambda b, r, n: (b, r, jnp.maximum(n * qblocks - 1, 0), j))
        return pl.BlockSpec((1, 1, rows, gw), lambda b, r, n: (b, r, n, j))

    return pl.pallas_call(
        functools.partial(_dilated_body, qblocks=qblocks),
        grid=(B, d, L // rows),
        in_specs=[spec(0, False), spec(1, True), spec(1, False), spec(2, True), spec(2, False),
                  pl.BlockSpec((DIL_HEADS * blk, 2 * blk), lambda b, r, n: (0, 0))],
        out_specs=pl.BlockSpec((1, 1, rows, 2 * gw), lambda b, r, n: (b, r, n, 0)),
        out_shape=jax.ShapeDtypeStruct((B, d, L, 2 * gw), F32),
        compiler_params=_params("parallel", "parallel", "arbitrary"),
        name=name,
    )(qkv, qkv, qkv, qkv, qkv, bias.reshape(DIL_HEADS * blk, 2 * blk))


def _moba_body(q_ref, k_ref, v_ref, bias_ref, o_ref, qa_ref, ka_ref, *, S):
    blk, hd = MOBA_BLOCK, HEAD_DIM
    nblk = S // blk
    scale = hd ** -0.5
    lane = lax.broadcasted_iota(jnp.int32, (1, 2 * hd), 1)
    kmean = jnp.mean(k_ref[0].astype(F32).reshape(nblk, blk, 2 * hd), axis=1)
    grp = MOBA_GROUP * blk
    row_blk = lax.broadcasted_iota(jnp.int32, (S, 2 * hd), 0) // blk
    q2, k2 = q_ref[0], k_ref[0]

    ka_ref[:, :2 * hd] = k2
    ka_ref[:, 2 * hd:] = jnp.where(lane == row_blk, 1.0, 0.0).astype(BF16)
    blk_t = lax.broadcasted_iota(jnp.int32, (nblk, S), 0)
    q_blk_t = lax.broadcasted_iota(jnp.int32, (nblk, S), 1) // blk
    past = blk_t < q_blk_t
    blk_f = blk_t.astype(F32)
    for h in range(2):
        head_lanes = (lane >= h * hd) & (lane < (h + 1) * hd)
        qh = jnp.where(head_lanes, q2, jnp.zeros_like(q2))
        gs = _nt_dot(jnp.where(head_lanes, kmean, 0.0), qh.astype(F32), precision=lax.Precision.HIGHEST)
        gs = jnp.where(past, gs, NEG_INF)
        sel = jnp.zeros(gs.shape, F32)
        for _r in range(MOBA_TOPK):
            gmax = jnp.max(gs, axis=0, keepdims=True)
            first = jnp.min(jnp.where(gs == gmax, blk_f, float(nblk)), axis=0, keepdims=True)
            pick = blk_f == first
            sel = jnp.where(pick & past, 1.0, sel)
            gs = jnp.where(pick, -jnp.inf, gs)
        penalty_t = jnp.where((sel > 0.5) | (blk_t == q_blk_t), 0.0, MOBA_MASK)
        penalty_t = jnp.concatenate([penalty_t, jnp.zeros((2 * hd - nblk, S), F32)], axis=0)
        qa_ref[h, :, :2 * hd] = qh * scale
        qa_ref[h, :, 2 * hd:] = penalty_t.T.astype(BF16)

    def q_block(qb, _):
        row0 = pl.multiple_of(qb * blk, blk)
        n_grp = qb // MOBA_GROUP + 1
        qa = jnp.concatenate([qa_ref[h, pl.ds(row0, blk), :] for h in range(2)], axis=0)

        def key_group(i, carry):
            m, l, acc = carry
            g = n_grp - 1 - i
            col0 = pl.multiple_of(g * grp, grp)
            per = MOBA_GROUP // MOBA_SPLIT
            half = per * blk
            cols = [pl.ds(pl.multiple_of(col0 + i * half, half), half) for i in range(MOBA_SPLIT)]
            scores = [_nt_dot(qa, ka_ref[c, :]) for c in cols]
            for i in range(MOBA_SPLIT):
                parts = []
                for jj in range(per):
                    n = g * MOBA_GROUP + i * per + jj
                    tile = jnp.clip(qb - n, 0, MOBA_BIAS_TILES - 1)
                    bias = jnp.concatenate([bias_ref[0, h, tile] for h in range(2)], axis=0)
                    parts.append(scores[i][:, jj * blk:(jj + 1) * blk] + bias)
                m_new = jnp.maximum(m, jnp.max(functools.reduce(jnp.maximum, parts), axis=-1, keepdims=True))
                a = jnp.exp(m - m_new)
                ps = [jnp.exp(p - m_new) for p in parts]
                l = l * a + jnp.sum(functools.reduce(jnp.add, ps), axis=-1, keepdims=True)
                p_all = jnp.concatenate([p.astype(BF16) for p in ps], axis=-1)
                pv = jnp.dot(p_all, v_ref[0, cols[i], :], preferred_element_type=F32)
                m, acc = m_new, acc * a + pv
            return m, l, acc

        init = (jnp.full((2 * blk, 1), -jnp.inf, F32), jnp.zeros((2 * blk, 1), F32), jnp.zeros((2 * blk, 2 * hd), F32))
        _, l, acc = lax.fori_loop(0, n_grp, key_group, init)
        o = acc / l
        o_ref[0, pl.ds(row0, blk), :] = jnp.where(lane < hd, o[:blk], o[blk:]).astype(o_ref.dtype)
        return 0

    lax.fori_loop(0, nblk, q_block, 0)


def _moba(qkv, bias, first_col):
    B, S, C = qkv.shape
    pw = 2 * HEAD_DIM
    npair = MOBA_HEADS // 2
    assert S % (MOBA_GROUP * MOBA_BLOCK) == 0 and S // MOBA_BLOCK <= HEAD_DIM and first_col % pw == 0
    c0 = first_col // pw

    def spec(j):
        return pl.BlockSpec((1, S, pw), lambda hp, b: (b, 0, c0 + j * npair + hp))

    return pl.pallas_call(
        functools.partial(_moba_body, S=S),
        grid=(npair, B),
        in_specs=[spec(0), spec(1), spec(2),
                  pl.BlockSpec((1, 2, MOBA_BIAS_TILES, MOBA_BLOCK, MOBA_BLOCK), lambda hp, b: (hp, 0, 0, 0, 0))],
        out_specs=pl.BlockSpec((1, S, pw), lambda hp, b: (b, 0, hp)),
        out_shape=jax.ShapeDtypeStruct((B, S, MOBA_HEADS * HEAD_DIM), BF16),
        scratch_shapes=[pltpu.VMEM((2, S, 2 * pw), BF16), pltpu.VMEM((S, 2 * pw), BF16)],
        compiler_params=_params("parallel", "parallel"),
        name="moba",
    )(qkv, qkv, qkv, bias)


def _merge_body(d0_ref, d1_ref, d2_ref, om_ref, gate_ref, x_ref, wa_ref, wb_ref, wo_ref, x1_ref, t1_ref, t2_ref):
    D = x_ref.shape[-1]
    gw = DIL_HEADS * HEAD_DIM
    n_slab = t1_ref.shape[0]
    for src, dst in ((d1_ref, t1_ref), (d2_ref, t2_ref)):
        dil, rows = src.shape[1], src.shape[2]
        for r in range(dil):
            for c in range(n_slab):
                dst[c, pl.ds(r, rows, stride=dil), :] = src[0, r, :, c * LANES:(c + 1) * LANES]
    g0 = d0_ref[...]
    g1 = jnp.concatenate([t1_ref[c] for c in range(n_slab)], axis=-1)
    g2 = jnp.concatenate([t2_ref[c] for c in range(n_slab)], axis=-1)
    l0, l1, l2 = g0[:, gw:], g1[:, gw:], g2[:, gw:]
    mx = jnp.maximum(jnp.maximum(l0, l1), l2)
    e0, e1, e2 = jnp.exp(l0 - mx), jnp.exp(l1 - mx), jnp.exp(l2 - mx)
    o_dil = (e0 * g0[:, :gw] + e1 * g1[:, :gw] + e2 * g2[:, :gw]) / (e0 + e1 + e2)
    a = jnp.dot(o_dil.astype(BF16), wa_ref[...], preferred_element_type=F32)
    b = jnp.dot(om_ref[...], wb_ref[...], preferred_element_type=F32)
    merged = gate_ref[:, :D].astype(F32) * a + gate_ref[:, D:].astype(F32) * b
    x1_ref[...] = x_ref[...] + jnp.dot(merged.astype(BF16), wo_ref[...], preferred_element_type=F32)


def _merge(dil_out, o_moba, gates, x2d, w_a, w_b, w_o, seq, tm):
    T, D = x2d.shape
    tm = min(tm, seq)
    per_seq = seq // tm
    assert seq % tm == 0 and dil_out[0].shape[1] == 1

    def rows(width):
        return pl.BlockSpec((tm, width), lambda i: (i, 0))

    def whole(w):
        return pl.BlockSpec(w.shape, lambda i: (0, 0))

    def residue_major(a):
        d, width = a.shape[1], a.shape[3]
        assert tm % (8 * d) == 0
        return pl.BlockSpec((1, d, tm // d, width), lambda i: (i // per_seq, 0, i % per_seq, 0))

    ws = [w_a.astype(BF16), w_b.astype(BF16), w_o.astype(BF16)]
    width = dil_out[0].shape[-1]
    return pl.pallas_call(
        _merge_body,
        grid=(T // tm,),
        in_specs=[rows(width), residue_major(dil_out[1]), residue_major(dil_out[2]),
                  rows(o_moba.shape[-1]), rows(2 * D), rows(D)] + [whole(w) for w in ws],
        out_specs=rows(D),
        out_shape=jax.ShapeDtypeStruct((T, D), F32),
        scratch_shapes=[pltpu.VMEM((width // LANES, tm, LANES), F32)] * 2,
        compiler_params=_params("parallel"),
        name="merge",
    )(dil_out[0].reshape(T, width), dil_out[1], dil_out[2], o_moba, gates, x2d, *ws)


def _cross_body(q_ref, kv_ref, x_ref, wo_ref, x2_ref):
    hd = CROSS_HEAD_DIM
    width = CROSS_HEADS * hd
    heads = []
    for h in range(CROSS_HEADS):
        sl = slice(h * hd, (h + 1) * hd)
        s = _nt_dot(q_ref[0, :, sl], kv_ref[0, :, sl]) * (hd ** -0.5)
        m = jnp.max(s, axis=-1, keepdims=True)
        p = jnp.exp(s - m)
        l = jnp.sum(p, axis=-1, keepdims=True)
        v = kv_ref[0, :, width + h * hd: width + (h + 1) * hd]
        heads.append((jnp.dot(p.astype(BF16), v, preferred_element_type=F32) / l).astype(BF16))
    o = jnp.concatenate(heads, axis=-1)
    x2_ref[0] = x_ref[0] + jnp.dot(o, wo_ref[...], preferred_element_type=F32)


def _cross(q, kv, x, w_o, tm):
    B, S, D = x.shape
    M = kv.shape[1]
    tm = min(tm, S)
    w_o = w_o.astype(BF16)
    return pl.pallas_call(
        _cross_body,
        grid=(B, S // tm),
        in_specs=[pl.BlockSpec((1, tm, q.shape[-1]), lambda b, i: (b, i, 0)),
                  pl.BlockSpec((1, M, kv.shape[-1]), lambda b, i: (b, 0, 0)),
                  pl.BlockSpec((1, tm, D), lambda b, i: (b, i, 0)),
                  pl.BlockSpec(w_o.shape, lambda b, i: (0, 0))],
        out_specs=pl.BlockSpec((1, tm, D), lambda b, i: (b, i, 0)),
        out_shape=jax.ShapeDtypeStruct((B, S, D), F32),
        compiler_params=_params("parallel", "parallel"),
        name="cross",
    )(q, kv, x, w_o)


def _merge_exchange_pairs(n):
    pairs = []
    p = 1
    while p < n:
        k = p
        while k >= 1:
            for j in range(k % p, n - k, 2 * k):
                for i in range(min(k, n - j - k)):
                    if (i + j) // (2 * p) == (i + j + k) // (2 * p):
                        pairs.append((i + j, i + j + k))
            k //= 2
        p *= 2
    return pairs


def _top16(s):
    n, sub = 16, 8
    assert s.shape[0] % sub == 0 and sub * n >= s.shape[0] >= 2 * sub
    v = [s[k * sub:(k + 1) * sub] for k in range(s.shape[0] // sub)] + [None] * (n - s.shape[0] // sub)

    def exchange(i, j):
        if v[j] is None:
            return
        if v[i] is None:
            v[i], v[j] = v[j], None
        else:
            v[i], v[j] = jnp.maximum(v[i], v[j]), jnp.minimum(v[i], v[j])

    for i, j in _merge_exchange_pairs(n):
        exchange(i, j)
    for shift in (4, 2, 1):
        w = [None if x is None else pltpu.roll(x, shift, 0) for x in v]
        for k in range(n):
            other = w[n - 1 - k]
            if other is not None:
                v[k] = other if v[k] is None else jnp.maximum(v[k], other)
        d = n // 2
        while d >= 1:
            for i in range(n):
                if i & d == 0:
                    exchange(i, i + d)
            d //= 2
    return v


def _top_sorted(s):
    v = _top16(s)
    nxt = jnp.max(jnp.where(s < v[-1][0:1], s, -jnp.inf), axis=0, keepdims=True)
    return jnp.concatenate([x[0:1] for x in v] + [nxt], axis=0)


def _pair_sum_candidates(a, b):
    assert PEER_TOPK == 16
    row = lax.broadcasted_iota(jnp.int32, (PEER_TOPK, 1), 0)
    blocks = [a[0:1] + b]
    blocks += [a[i:i + 1] + b[0:8] for i in (1, 2, 3)]
    blocks += [jnp.where(row >= 4, b[0:1] + a, -jnp.inf)]
    blocks += [jnp.where(row[0:8] >= 4, b[j:j + 1] + a[0:8], -jnp.inf) for j in (1, 2)]
    return jnp.concatenate(blocks, axis=0)


def _peer_route_body(q_ref, keys_ref, thr_ref, e1_ref, s2_ref, e2_ref):
    nk, K = PEER_N_KEYS, PEER_TOPK
    kd = keys_ref.shape[-1]
    for h in range(PEER_HEADS):
        sc = []
        for p in range(2):
            c0 = (2 * h + p) * kd
            sc.append(_nt_dot(keys_ref[p], q_ref[:, c0:c0 + kd], precision=lax.Precision.HIGHEST))
        a = _top_sorted(sc[0])
        b = _top_sorted(sc[1])
        cand = _pair_sum_candidates(a[:K], b[:K])
        tau = _top16(cand)[K - 1][0:1]
        below = jnp.max(jnp.where(cand < tau, cand, -jnp.inf), axis=0, keepdims=True)
        for corner in (a[0:1] + b[K:K + 1], a[K:K + 1] + b[0:1]):
            below = jnp.maximum(below, jnp.where(corner < tau, corner, -jnp.inf))
        below = jnp.where(below == -jnp.inf, tau, below)
        cut = 0.5 * (tau + below)
        top = a[0:1] + b[0:1]
        z = jnp.sum(jnp.where(cand >= tau, jnp.exp(cand - top), 0.0), axis=0, keepdims=True)
        thr = cut - sc[0]
        e1 = jnp.exp(sc[0] - a[0:1]) * (0.5 / z)
        for c in range(thr_ref.shape[1]):
            thr_ref[h, c] = thr[:, c * LANES:(c + 1) * LANES]
            e1_ref[h, c] = e1[:, c * LANES:(c + 1) * LANES]
        s2_ref[h] = sc[1]
        e2_ref[h] = jnp.exp(sc[1] - b[0:1])


def _peer_route(q, sub_keys, tt):
    T = q.shape[0]
    tt = min(tt, T)
    nk = PEER_N_KEYS
    assert tt % LANES == 0
    big = jax.ShapeDtypeStruct((PEER_HEADS, nk, T), F32)
    big_spec = pl.BlockSpec((PEER_HEADS, nk, tt), lambda i: (0, 0, i))
    slab = jax.ShapeDtypeStruct((PEER_HEADS, T // LANES, nk, LANES), F32)
    slab_spec = pl.BlockSpec((PEER_HEADS, tt // LANES, nk, LANES), lambda i: (0, i, 0, 0))
    return pl.pallas_call(
        _peer_route_body,
        grid=(T // tt,),
        in_specs=[pl.BlockSpec((tt, q.shape[1]), lambda i: (i, 0)),
                  pl.BlockSpec(sub_keys.shape, lambda i: (0, 0, 0))],
        out_specs=[slab_spec, slab_spec, big_spec, big_spec],
        out_shape=[slab, slab, big, big],
        compiler_params=_params("parallel"),
        name="peer_route",
    )(q, sub_keys.astype(F32))


def _peer_main_body(h_ref, u_ref, vt_ref, thr_ref, e1_ref, s2_ref, e2_ref, x_ref, g_ref,
                    o_ref, acc_ref, pre_ref, ga_ref, ht_ref, *, rows_per_step):
    nk = PEER_N_KEYS
    lanes = 128
    j = pl.program_id(1)
    tt = h_ref.shape[0]

    @pl.when(j == 0)
    def _():
        acc_ref[...] = jnp.zeros_like(acc_ref)
        ht_ref[...] = h_ref[...].astype(F32).T.astype(ht_ref.dtype)

    i1_base = pl.multiple_of(j * rows_per_step, rows_per_step)
    group = 4
    sub = 64
    n_sub = nk // sub

    pre_ref[...] = jnp.dot(u_ref[...], ht_ref[...], preferred_element_type=F32)

    def gate_tile(ti, carry):
        cs = pl.ds(pl.multiple_of((ti // n_sub) * lanes, lanes), lanes)
        k0 = pl.multiple_of((ti % n_sub) * sub, sub)
        for r0 in range(0, rows_per_step, group):
            gates = [None] * group
            for h in range(PEER_HEADS):
                s2 = s2_ref[h, pl.ds(k0, sub), cs]
                e2 = e2_ref[h, pl.ds(k0, sub), cs]
                for g in range(group):
                    r = r0 + g
                    thr = thr_ref[h, ti // n_sub, pl.ds(i1_base + r, 8, stride=0), :]
                    e1 = e1_ref[h, ti // n_sub, pl.ds(i1_base + r, 8, stride=0), :]
                    w = (jnp.where(s2.reshape(sub // 8, 8, lanes) >= thr, e2.reshape(sub // 8, 8, lanes), 0.0)
                         * e1).reshape(sub, lanes)
                    gates[g] = w if gates[g] is None else gates[g] + w
            for g in range(group):
                rs = pl.ds(pl.multiple_of((r0 + g) * nk + k0, sub), sub)
                pre = pre_ref[rs, cs]
                act = pre * (1.0 + lax.erf(pre * (2.0 ** -0.5)))
                ga_ref[rs, cs] = (gates[g] * act).astype(ga_ref.dtype)
        return carry

    lax.fori_loop(0, (tt // lanes) * n_sub, gate_tile, 0)
    acc_ref[...] += jnp.dot(vt_ref[0], ga_ref[...], preferred_element_type=F32)

    @pl.when(j == pl.num_programs(1) - 1)
    def _():
        o_ref[...] = _rms(x_ref[...] + acc_ref[...].T, g_ref[...])


def _peer_main(hn, u, v, thr, e1, s2, e2, x2d, g_final, tt, rows_per_step):
    T, D = x2d.shape
    tt = min(tt, T)
    nk = PEER_N_KEYS
    eb = rows_per_step * nk
    n_exp = u.shape[0]
    assert n_exp == nk * nk and nk % rows_per_step == 0
    u = u.astype(BF16)
    vt = v.astype(BF16).reshape(n_exp // eb, eb, D).transpose(0, 2, 1)
    score_spec = pl.BlockSpec((PEER_HEADS, nk, tt), lambda i, j: (0, 0, i))
    slab_spec = pl.BlockSpec((PEER_HEADS, tt // LANES, nk, LANES), lambda i, j: (0, i, 0, 0))
    return pl.pallas_call(
        functools.partial(_peer_main_body, rows_per_step=rows_per_step),
        grid=(T // tt, n_exp // eb),
        in_specs=[pl.BlockSpec((tt, D), lambda i, j: (i, 0)),
                  pl.BlockSpec((eb, D), lambda i, j: (j, 0)),
                  pl.BlockSpec((1, D, eb), lambda i, j: (j, 0, 0)),
                  slab_spec, slab_spec, score_spec, score_spec,
                  pl.BlockSpec((tt, D), lambda i, j: (i, 0)),
                  pl.BlockSpec((1, D), lambda i, j: (0, 0))],
        out_specs=pl.BlockSpec((tt, D), lambda i, j: (i, 0)),
        out_shape=jax.ShapeDtypeStruct((T, D), F32),
        scratch_shapes=[pltpu.VMEM((D, tt), F32), pltpu.VMEM((eb, tt), F32), pltpu.VMEM((eb, tt), BF16),
                        pltpu.VMEM((D, tt), BF16)],
        compiler_params=_params("parallel", "arbitrary"),
        name="peer_main",
    )(hn, u, vt, thr, e1, s2, e2, x2d, g_final.reshape(1, D).astype(F32))


def kernel(x, mem, rel_bias, g_mix, w_in, b_gate, w_branch_a, w_branch_b, w_out, g_cross, g_mem, w_q_cross,
           w_kv_cross, w_o_cross, g_ffn, w_q_peer, peer_sub_keys, peer_u, peer_v, g_final):
    B, S, D = x.shape
    T = B * S
    depth = w_in.shape[0]
    ndil = len(DIL_GROUPS) * DIL_HEADS
    x2d = x.reshape(T, D)

    dil_bias = [_bias_lookup(_dil_bucket_map(d), rel_bias[:, gi * DIL_HEADS:(gi + 1) * DIL_HEADS], f"dil_bias_g{gi}")
                for gi, (_, d) in enumerate(DIL_GROUPS)]
    mmap = _moba_bucket_map()
    moba_bias = _bias_lookup(mmap.reshape(-1, MOBA_BLOCK), rel_bias[:, ndil:], "moba_bias")
    moba_bias = moba_bias.reshape(MOBA_HEADS // 2, 2, MOBA_BIAS_TILES, MOBA_BLOCK, MOBA_BLOCK)

    assert depth == 1, "the final norm is fused into the last PEER sweep; one layer supported"
    l = 0
    gc = DIL_GROUP_COLS
    assert DIL_GROUPS[0][1] == 1
    w_tok = jnp.concatenate([w_in[l][:, :gc], w_in[l][:, DIL_WIDTH:QKV_WIDTH]], axis=1)
    tok = _norm_matmul(x2d, g_mix[l], w_tok, out_dtype=BF16, tm=1024, tn=gc, name="proj_tok")
    dil_qkv = [tok.reshape(B, 1, S, -1)]
    for gi, (_, d) in enumerate(DIL_GROUPS[1:], start=1):
        dil_qkv.append(_norm_matmul(x2d, g_mix[l], w_in[l][:, gi * gc:(gi + 1) * gc], out_dtype=BF16, tm=1024, tn=gc,
                                    dil=d, seq=S, name=f"proj_dil_g{gi}"))
    gates = _norm_matmul(x2d, g_mix[l], w_in[l][:, QKV_WIDTH:], out_dtype=BF16, tm=1024, tn=1024,
                         bias=b_gate[l], name="proj_gates")
    dil_out = [_dilated_group(a, dil_bias[gi], f"dilated_g{gi}") for gi, a in enumerate(dil_qkv)]
    o_moba = _moba(tok.reshape(B, S, -1), moba_bias, first_col=gc).reshape(T, MOBA_HEADS * HEAD_DIM)
    x2d = _merge(dil_out, o_moba, gates, x2d, w_branch_a[l], w_branch_b[l], w_out[l], seq=S, tm=512)
    M = mem.shape[1]
    q_c = _norm_matmul(x2d, g_cross[l], w_q_cross[l], out_dtype=BF16, tm=1024, tn=512, name="cross_q")
    kv = _norm_matmul(mem.reshape(B * M, D), g_mem[l], w_kv_cross[l], out_dtype=BF16, tm=1024, tn=1024, name="cross_kv")
    x2d = _cross(q_c.reshape(B, S, -1), kv.reshape(B, M, -1), x2d.reshape(B, S, D), w_o_cross[l], tm=1024).reshape(T, D)
    q_p, hn = _norm_matmul(x2d, g_ffn[l], w_q_peer[l], out_dtype=F32, tm=1024, tn=1024, emit_normed=True, name="peer_q")
    thr, e1, s2, e2 = _peer_route(q_p, peer_sub_keys[l], tt=512)
    y = _peer_main(hn, peer_u[l], peer_v[l], thr, e1, s2, e2, x2d, g_final,
                   tt=512, rows_per_step=8)
    return y.reshape(B, S, D)
```

```python
import functools
import math

import jax
import jax.numpy as jnp
import numpy as np
from jax import lax
from jax.experimental import pallas as pl
from jax.experimental.pallas import tpu as pltpu

HEAD_DIM = 64
DIL_GROUPS = ((128, 1), (512, 4), (2048, 16))
DIL_HEADS = 4
DIL_BLOCK = 128
MOBA_HEADS = 8
MOBA_BLOCK = 256
MOBA_TOPK = 3
MOBA_BIAS_TILES = 8
MOBA_GROUP = 4
MOBA_SPLIT = 2
MOBA_MASK = -(2.0 ** 100)
REL_BUCKETS = 32
REL_MAX_DIST = 2048
CROSS_HEADS = 4
CROSS_HEAD_DIM = 128
PEER_HEADS = 8
PEER_N_KEYS = 128
PEER_TOPK = 16
RMS_EPS = 1e-6
NEG_INF = -1e30
DIL_GROUP_COLS = 3 * DIL_HEADS * HEAD_DIM
DIL_WIDTH = len(DIL_GROUPS) * DIL_GROUP_COLS
MOBA_WIDTH = 3 * MOBA_HEADS * HEAD_DIM
QKV_WIDTH = DIL_WIDTH + MOBA_WIDTH

LANES = 128
SUBLANES = 8
V7X_VMEM_BYTES = 64 * 1024 * 1024
VMEM_LIMIT = V7X_VMEM_BYTES * 3 // 4

BF16 = jnp.bfloat16
F32 = jnp.float32


def _params(*sem):
    return pltpu.CompilerParams(dimension_semantics=sem, vmem_limit_bytes=VMEM_LIMIT)


def _rms(x, g):
    return x * lax.rsqrt(jnp.mean(x * x, axis=-1, keepdims=True) + RMS_EPS) * g


def _nt_dot(a, b, precision=None):
    return lax.dot_general(a, b, (((1,), (1,)), ((), ())), preferred_element_type=F32, precision=precision)


def _norm_matmul_body(x_ref, g_ref, w_ref, *rest, sigmoid_bias, dil):
    rest = list(rest)
    b_ref = rest.pop(0) if sigmoid_bias else None
    o_ref, h_scr = rest[:2]
    rows = h_scr.shape[0] // dil

    @pl.when(pl.program_id(1) == 0)
    def _():
        if dil == 1:
            h_scr[...] = _rms(x_ref[...], g_ref[...]).astype(h_scr.dtype)
        else:
            cols_ref = rest[2]
            n_slab = cols_ref.shape[0]
            for c in range(n_slab):
                cols_ref[c] = x_ref[:, c * LANES:(c + 1) * LANES]
            for r in range(dil):
                x = jnp.concatenate([cols_ref[c, pl.ds(r, rows, stride=dil), :] for c in range(n_slab)], axis=-1)
                h_scr[r * rows:(r + 1) * rows, :] = _rms(x, g_ref[...]).astype(h_scr.dtype)

    acc = jnp.dot(h_scr[...], w_ref[...], preferred_element_type=F32)
    if sigmoid_bias:
        acc = jax.nn.sigmoid(acc + b_ref[...])
    if dil == 1:
        o_ref[...] = acc.astype(o_ref.dtype)
    else:
        for r in range(dil):
            o_ref[0, r] = acc[r * rows:(r + 1) * rows].astype(o_ref.dtype)


def _norm_matmul(x2d, g, w, *, out_dtype, tm, tn, bias=None, dil=1, seq=None, emit_normed=False, name):
    T, D = x2d.shape
    N = w.shape[1]
    tm = min(tm, T if seq is None else seq)
    assert T % tm == 0 and N % tn == 0
    in_specs = [
        pl.BlockSpec((tm, D), lambda i, j: (i, 0)),
        pl.BlockSpec((1, D), lambda i, j: (0, 0)),
        pl.BlockSpec((D, tn), lambda i, j: (0, j)),
    ]
    args = [x2d, g.reshape(1, D).astype(F32), w.astype(BF16)]
    if bias is not None:
        in_specs.append(pl.BlockSpec((1, tn), lambda i, j: (0, j)))
        args.append(bias.reshape(1, N).astype(F32))
    assert not (emit_normed and dil > 1)
    scratch = [] if emit_normed else [pltpu.VMEM((tm, D), BF16)]
    if dil == 1:
        out_spec = pl.BlockSpec((tm, tn), lambda i, j: (i, j))
        out_shape = jax.ShapeDtypeStruct((T, N), out_dtype)
    else:
        assert seq % tm == 0 and tm % (8 * dil) == 0 and D % LANES == 0
        per_seq = seq // tm
        out_spec = pl.BlockSpec((1, dil, tm // dil, tn), lambda i, j: (i // per_seq, 0, i % per_seq, j))
        out_shape = jax.ShapeDtypeStruct((T // seq, dil, seq // dil, N), out_dtype)
        scratch.append(pltpu.VMEM((D // LANES, tm, LANES), F32))
    if emit_normed:
        out_spec = [out_spec, pl.BlockSpec((tm, D), lambda i, j: (i, 0))]
        out_shape = [out_shape, jax.ShapeDtypeStruct((T, D), BF16)]
    return pl.pallas_call(
        functools.partial(_norm_matmul_body, sigmoid_bias=bias is not None, dil=dil),
        grid=(T // tm, N // tn),
        in_specs=in_specs,
        out_specs=out_spec,
        out_shape=out_shape,
        scratch_shapes=scratch,
        compiler_params=_params("parallel", "arbitrary"),
        name=name,
    )(*args)


def _t5_bucket_np(n):
    max_exact = REL_BUCKETS // 2
    nf = np.maximum(n, 1).astype(np.float32)
    large = max_exact + (np.log(nf / np.float32(max_exact)) / np.float32(math.log(REL_MAX_DIST / max_exact))
                         * np.float32(REL_BUCKETS - max_exact)).astype(np.int32)
    large = np.minimum(large, REL_BUCKETS - 1)
    return np.where(n < max_exact, n, large).astype(np.int32)


def _dil_bucket_map(d):
    blk = DIL_BLOCK
    rel = (blk + np.arange(blk))[:, None] - np.arange(2 * blk)[None, :]
    return _t5_bucket_np(np.clip(rel, 0, None) * d)


def _moba_bucket_map():
    blk = MOBA_BLOCK
    rel = np.arange(blk)[:, None] - np.arange(blk)[None, :]
    tiles = [_t5_bucket_np(np.clip(rel + dlt * blk, 0, None)) for dlt in range(MOBA_BIAS_TILES)]
    tiles[0] = np.where(rel >= 0, tiles[0], -1)
    far = _t5_bucket_np(np.arange((MOBA_BIAS_TILES - 2) * blk + 1, 64 * blk))
    assert (far == REL_BUCKETS - 1).all()
    return np.stack(tiles)


def _bias_lookup_body(idx_ref, range_ref, tab_ref, o_ref):
    h, i = pl.program_id(0), pl.program_id(1)
    idx = idx_ref[...]

    def bucket(b, acc):
        return jnp.where(idx == b, tab_ref[b, h], acc)

    o_ref[0] = lax.fori_loop(range_ref[i, 0], range_ref[i, 1], bucket, jnp.full(idx.shape, NEG_INF, F32))


def _bias_lookup(bucket_map, table, name):
    R, C = bucket_map.shape
    H = table.shape[1]
    rt = min(R, 256)
    assert R % rt == 0
    tiles = bucket_map.reshape(R // rt, rt * C)
    ranges = np.stack([np.where(tiles >= 0, tiles, REL_BUCKETS).min(axis=1), tiles.max(axis=1) + 1], axis=1)
    return pl.pallas_call(
        _bias_lookup_body,
        grid=(H, R // rt),
        in_specs=[pl.BlockSpec((rt, C), lambda h, i: (i, 0)), pl.BlockSpec(memory_space=pltpu.SMEM),
                  pl.BlockSpec(memory_space=pltpu.SMEM)],
        out_specs=pl.BlockSpec((1, rt, C), lambda h, i: (h, i, 0)),
        out_shape=jax.ShapeDtypeStruct((H, R, C), F32),
        compiler_params=_params("parallel", "parallel"),
        name=name,
    )(jnp.asarray(bucket_map), jnp.asarray(ranges.astype(np.int32)), table.astype(F32))


def _dilated_body(q_ref, kp_ref, kc_ref, vp_ref, vc_ref, bias_ref, o_ref, *, qblocks):
    blk, hd = DIL_BLOCK, HEAD_DIM
    gw = DIL_HEADS * hd
    n = pl.program_id(2)
    k = jnp.concatenate([kp_ref[0, 0], kc_ref[0, 0]], axis=0)
    v = jnp.concatenate([vp_ref[0, 0], vc_ref[0, 0]], axis=0)
    lane = lax.broadcasted_iota(jnp.int32, (1, gw), 1)
    qi = lax.broadcasted_iota(jnp.int32, (blk, 2 * blk), 0)
    kj = lax.broadcasted_iota(jnp.int32, (blk, 2 * blk), 1)
    rel = blk + qi - kj
    band = (rel >= 0) & (rel <= blk)
    first_key = jnp.where(n > 0, 0, blk)
    for i in range(qblocks):
        rows = slice(i * blk, (i + 1) * blk)
        q = q_ref[0, 0, rows, :]
        kw = k[i * blk:(i + 2) * blk]
        vw = v[i * blk:(i + 2) * blk]
        mask = band & (kj >= first_key) if i == 0 else band
        head_lanes = [(lane >= h * hd) & (lane < (h + 1) * hd) for h in range(DIL_HEADS)]
        q_all = jnp.concatenate([jnp.where(hl, q, jnp.zeros_like(q)) for hl in head_lanes], axis=0)
        s = _nt_dot(q_all, kw) * (hd ** -0.5) + bias_ref[...]
        s = jnp.where(jnp.concatenate([mask] * DIL_HEADS, axis=0), s, NEG_INF)
        m = jnp.max(s, axis=-1, keepdims=True)
        p = jnp.exp(s - m)
        l = jnp.sum(p, axis=-1, keepdims=True)
        pv = jnp.dot(p.astype(BF16), vw, preferred_element_type=F32) / l
        lse_all = m + jnp.log(l)
        out = jnp.zeros((blk, gw), F32)
        lse = jnp.zeros((blk, gw), F32)
        for h, hl in enumerate(head_lanes):
            out = jnp.where(hl, pv[h * blk:(h + 1) * blk], out)
            lse = jnp.where(hl, lse_all[h * blk:(h + 1) * blk], lse)
        o_ref[0, 0, rows, :gw] = out
        o_ref[0, 0, rows, gw:] = lse


def _dilated_group(qkv, bias, name):
    B, d, L, _ = qkv.shape
    blk, gw = DIL_BLOCK, DIL_HEADS * HEAD_DIM
    assert L % blk == 0
    qblocks = min(4, L // blk)
    assert L % (qblocks * blk) == 0
    rows = qblocks * blk

    def spec(j, prev):
        if prev:
            return pl.BlockSpec((1, 1, blk, gw), lambda b, r, n: (b, r, jnp.maximum(n * qblocks - 1, 0), j))
        return pl.BlockSpec((1, 1, rows, gw), lambda b, r, n: (b, r, n, j))

    return pl.pallas_call(
        functools.partial(_dilated_body, qblocks=qblocks),
        grid=(B, d, L // rows),
        in_specs=[spec(0, False), spec(1, True), spec(1, False), spec(2, True), spec(2, False),
                  pl.BlockSpec((DIL_HEADS * blk, 2 * blk), lambda b, r, n: (0, 0))],
        out_specs=pl.BlockSpec((1, 1, rows, 2 * gw), lambda b, r, n: (b, r, n, 0)),
        out_shape=jax.ShapeDtypeStruct((B, d, L, 2 * gw), F32),
        compiler_params=_params("parallel", "parallel", "arbitrary"),
        name=name,
    )(qkv, qkv, qkv, qkv, qkv, bias.reshape(DIL_HEADS * blk, 2 * blk))


def _moba_body(q_ref, k_ref, v_ref, bias_ref, o_ref, qa_ref, ka_ref, *, S):
    blk, hd = MOBA_BLOCK, HEAD_DIM
    nblk = S // blk
    scale = hd ** -0.5
    lane = lax.broadcasted_iota(jnp.int32, (1, 2 * hd), 1)
    kmean = jnp.mean(k_ref[0].astype(F32).reshape(nblk, blk, 2 * hd), axis=1)
    grp = MOBA_GROUP * blk
    row_blk = lax.broadcasted_iota(jnp.int32, (S, 2 * hd), 0) // blk
    q2, k2 = q_ref[0], k_ref[0]

    ka_ref[:, :2 * hd] = k2
    ka_ref[:, 2 * hd:] = jnp.where(lane == row_blk, 1.0, 0.0).astype(BF16)
    blk_t = lax.broadcasted_iota(jnp.int32, (nblk, S), 0)
    q_blk_t = lax.broadcasted_iota(jnp.int32, (nblk, S), 1) // blk
    past = blk_t < q_blk_t
    blk_f = blk_t.astype(F32)
    for h in range(2):
        head_lanes = (lane >= h * hd) & (lane < (h + 1) * hd)
        qh = jnp.where(head_lanes, q2, jnp.zeros_like(q2))
        gs = _nt_dot(jnp.where(head_lanes, kmean, 0.0), qh.astype(F32), precision=lax.Precision.HIGHEST)
        gs = jnp.where(past, gs, NEG_INF)
        sel = jnp.zeros(gs.shape, F32)
        for _r in range(MOBA_TOPK):
            gmax = jnp.max(gs, axis=0, keepdims=True)
            first = jnp.min(jnp.where(gs == gmax, blk_f, float(nblk)), axis=0, keepdims=True)
            pick = blk_f == first
            sel = jnp.where(pick & past, 1.0, sel)
            gs = jnp.where(pick, -jnp.inf, gs)
        penalty_t = jnp.where((sel > 0.5) | (blk_t == q_blk_t), 0.0, MOBA_MASK)
        penalty_t = jnp.concatenate([penalty_t, jnp.zeros((2 * hd - nblk, S), F32)], axis=0)
        qa_ref[h, :, :2 * hd] = qh * scale
        qa_ref[h, :, 2 * hd:] = penalty_t.T.astype(BF16)

    def q_block(qb, _):
        row0 = pl.multiple_of(qb * blk, blk)
        n_grp = qb // MOBA_GROUP + 1
        qa = jnp.concatenate([qa_ref[h, pl.ds(row0, blk), :] for h in range(2)], axis=0)

        def key_group(i, carry):
            m, l, acc = carry
            g = n_grp - 1 - i
            col0 = pl.multiple_of(g * grp, grp)
            per = MOBA_GROUP // MOBA_SPLIT
            half = per * blk
            cols = [pl.ds(pl.multiple_of(col0 + i * half, half), half) for i in range(MOBA_SPLIT)]
            scores = [_nt_dot(qa, ka_ref[c, :]) for c in cols]
            for i in range(MOBA_SPLIT):
                parts = []
                for jj in range(per):
                    n = g * MOBA_GROUP + i * per + jj
                    tile = jnp.clip(qb - n, 0, MOBA_BIAS_TILES - 1)
                    bias = jnp.concatenate([bias_ref[0, h, tile] for h in range(2)], axis=0)
                    parts.append(scores[i][:, jj * blk:(jj + 1) * blk] + bias)
                m_new = jnp.maximum(m, jnp.max(functools.reduce(jnp.maximum, parts), axis=-1, keepdims=True))
                a = jnp.exp(m - m_new)
                ps = [jnp.exp(p - m_new) for p in parts]
                l = l * a + jnp.sum(functools.reduce(jnp.add, ps), axis=-1, keepdims=True)
                p_all = jnp.concatenate([p.astype(BF16) for p in ps], axis=-1)
                pv = jnp.dot(p_all, v_ref[0, cols[i], :], preferred_element_type=F32)
                m, acc = m_new, acc * a + pv
            return m, l, acc

        init = (jnp.full((2 * blk, 1), -jnp.inf, F32), jnp.zeros((2 * blk, 1), F32), jnp.zeros((2 * blk, 2 * hd), F32))
        _, l, acc = lax.fori_loop(0, n_grp, key_group, init)
        o = acc / l
        o_ref[0, pl.ds(row0, blk), :] = jnp.where(lane < hd, o[:blk], o[blk:]).astype(o_ref.dtype)
        return 0

    lax.fori_loop(0, nblk, q_block, 0)


def _moba(qkv, bias, first_col):
    B, S, C = qkv.shape
    pw = 2 * HEAD_DIM
    npair = MOBA_HEADS // 2
    assert S % (MOBA_GROUP * MOBA_BLOCK) == 0 and S // MOBA_BLOCK <= HEAD_DIM and first_col % pw == 0
    c0 = first_col // pw

    def spec(j):
        return pl.BlockSpec((1, S, pw), lambda hp, b: (b, 0, c0 + j * npair + hp))

    return pl.pallas_call(
        functools.partial(_moba_body, S=S),
        grid=(npair, B),
        in_specs=[spec(0), spec(1), spec(2),
                  pl.BlockSpec((1, 2, MOBA_BIAS_TILES, MOBA_BLOCK, MOBA_BLOCK), lambda hp, b: (hp, 0, 0, 0, 0))],
        out_specs=pl.BlockSpec((1, S, pw), lambda hp, b: (b, 0, hp)),
        out_shape=jax.ShapeDtypeStruct((B, S, MOBA_HEADS * HEAD_DIM), BF16),
        scratch_shapes=[pltpu.VMEM((2, S, 2 * pw), BF16), pltpu.VMEM((S, 2 * pw), BF16)],
        compiler_params=_params("parallel", "parallel"),
        name="moba",
    )(qkv, qkv, qkv, bias)


def _merge_body(d0_ref, d1_ref, d2_ref, om_ref, gate_ref, x_ref, wa_ref, wb_ref, wo_ref, x1_ref, t1_ref, t2_ref):
    D = x_ref.shape[-1]
    gw = DIL_HEADS * HEAD_DIM
    n_slab = t1_ref.shape[0]
    for src, dst in ((d1_ref, t1_ref), (d2_ref, t2_ref)):
        dil, rows = src.shape[1], src.shape[2]
        for r in range(dil):
            for c in range(n_slab):
                dst[c, pl.ds(r, rows, stride=dil), :] = src[0, r, :, c * LANES:(c + 1) * LANES]
    g0 = d0_ref[...]
    g1 = jnp.concatenate([t1_ref[c] for c in range(n_slab)], axis=-1)
    g2 = jnp.concatenate([t2_ref[c] for c in range(n_slab)], axis=-1)
    l0, l1, l2 = g0[:, gw:], g1[:, gw:], g2[:, gw:]
    mx = jnp.maximum(jnp.maximum(l0, l1), l2)
    e0, e1, e2 = jnp.exp(l0 - mx), jnp.exp(l1 - mx), jnp.exp(l2 - mx)
    o_dil = (e0 * g0[:, :gw] + e1 * g1[:, :gw] + e2 * g2[:, :gw]) / (e0 + e1 + e2)
    a = jnp.dot(o_dil.astype(BF16), wa_ref[...], preferred_element_type=F32)
    b = jnp.dot(om_ref[...], wb_ref[...], preferred_element_type=F32)
    merged = gate_ref[:, :D].astype(F32) * a + gate_ref[:, D:].astype(F32) * b
    x1_ref[...] = x_ref[...] + jnp.dot(merged.astype(BF16), wo_ref[...], preferred_element_type=F32)


def _merge(dil_out, o_moba, gates, x2d, w_a, w_b, w_o, seq, tm):
    T, D = x2d.shape
    tm = min(tm, seq)
    per_seq = seq // tm
    assert seq % tm == 0 and dil_out[0].shape[1] == 1

    def rows(width):
        return pl.BlockSpec((tm, width), lambda i: (i, 0))

    def whole(w):
        return pl.BlockSpec(w.shape, lambda i: (0, 0))

    def residue_major(a):
        d, width = a.shape[1], a.shape[3]
        assert tm % (8 * d) == 0
        return pl.BlockSpec((1, d, tm // d, width), lambda i: (i // per_seq, 0, i % per_seq, 0))

    ws = [w_a.astype(BF16), w_b.astype(BF16), w_o.astype(BF16)]
    width = dil_out[0].shape[-1]
    return pl.pallas_call(
        _merge_body,
        grid=(T // tm,),
        in_specs=[rows(width), residue_major(dil_out[1]), residue_major(dil_out[2]),
                  rows(o_moba.shape[-1]), rows(2 * D), rows(D)] + [whole(w) for w in ws],
        out_specs=rows(D),
        out_shape=jax.ShapeDtypeStruct((T, D), F32),
        scratch_shapes=[pltpu.VMEM((width // LANES, tm, LANES), F32)] * 2,
        compiler_params=_params("parallel"),
        name="merge",
    )(dil_out[0].reshape(T, width), dil_out[1], dil_out[2], o_moba, gates, x2d, *ws)


def _cross_body(q_ref, kv_ref, x_ref, wo_ref, x2_ref):
    hd = CROSS_HEAD_DIM
    width = CROSS_HEADS * hd
    heads = []
    for h in range(CROSS_HEADS):
        sl = slice(h * hd, (h + 1) * hd)
        s = _nt_dot(q_ref[0, :, sl], kv_ref[0, :, sl]) * (hd ** -0.5)
        m = jnp.max(s, axis=-1, keepdims=True)
        p = jnp.exp(s - m)
        l = jnp.sum(p, axis=-1, keepdims=True)
        v = kv_ref[0, :, width + h * hd: width + (h + 1) * hd]
        heads.append((jnp.dot(p.astype(BF16), v, preferred_element_type=F32) / l).astype(BF16))
    o = jnp.concatenate(heads, axis=-1)
    x2_ref[0] = x_ref[0] + jnp.dot(o, wo_ref[...], preferred_element_type=F32)


def _cross(q, kv, x, w_o, tm):
    B, S, D = x.shape
    M = kv.shape[1]
    tm = min(tm, S)
    w_o = w_o.astype(BF16)
    return pl.pallas_call(
        _cross_body,
        grid=(B, S // tm),
        in_specs=[pl.BlockSpec((1, tm, q.shape[-1]), lambda b, i: (b, i, 0)),
                  pl.BlockSpec((1, M, kv.shape[-1]), lambda b, i: (b, 0, 0)),
                  pl.BlockSpec((1, tm, D), lambda b, i: (b, i, 0)),
                  pl.BlockSpec(w_o.shape, lambda b, i: (0, 0))],
        out_specs=pl.BlockSpec((1, tm, D), lambda b, i: (b, i, 0)),
        out_shape=jax.ShapeDtypeStruct((B, S, D), F32),
        compiler_params=_params("parallel", "parallel"),
        name="cross",
    )(q, kv, x, w_o)


def _merge_exchange_pairs(n):
    pairs = []
    p = 1
    while p < n:
        k = p
        while k >= 1:
            for j in range(k % p, n - k, 2 * k):
                for i in range(min(k, n - j - k)):
                    if (i + j) // (2 * p) == (i + j + k) // (2 * p):
                        pairs.append((i + j, i + j + k))
            k //= 2
        p *= 2
    return pairs


def _top16(s):
    n, sub = PEER_TOPK, SUBLANES
    assert s.shape[0] % sub == 0 and sub * n >= s.shape[0] >= 2 * sub
    v = [s[k * sub:(k + 1) * sub] for k in range(s.shape[0] // sub)] + [None] * (n - s.shape[0] // sub)

    def exchange(i, j):
        if v[j] is None:
            return
        if v[i] is None:
            v[i], v[j] = v[j], None
        else:
            v[i], v[j] = jnp.maximum(v[i], v[j]), jnp.minimum(v[i], v[j])

    for i, j in _merge_exchange_pairs(n):
        exchange(i, j)
    for shift in (sub // 2, sub // 4, sub // 8):
        w = [None if x is None else pltpu.roll(x, shift, 0) for x in v]
        for k in range(n):
            other = w[n - 1 - k]
            if other is not None:
                v[k] = other if v[k] is None else jnp.maximum(v[k], other)
        d = n // 2
        while d >= 1:
            for i in range(n):
                if i & d == 0:
                    exchange(i, i + d)
            d //= 2
    return v


def _top_sorted(s):
    v = _top16(s)
    nxt = jnp.max(jnp.where(s < v[-1][0:1], s, -jnp.inf), axis=0, keepdims=True)
    return jnp.concatenate([x[0:1] for x in v] + [nxt], axis=0)


def _pair_sum_candidates(a, b):
    assert PEER_TOPK == 16
    row = lax.broadcasted_iota(jnp.int32, (PEER_TOPK, 1), 0)
    blocks = [a[0:1] + b]
    blocks += [a[i:i + 1] + b[0:8] for i in (1, 2, 3)]
    blocks += [jnp.where(row >= 4, b[0:1] + a, -jnp.inf)]
    blocks += [jnp.where(row[0:8] >= 4, b[j:j + 1] + a[0:8], -jnp.inf) for j in (1, 2)]
    return jnp.concatenate(blocks, axis=0)


def _peer_route_body(q_ref, keys_ref, thr_ref, e1_ref, s2_ref, e2_ref):
    nk, K = PEER_N_KEYS, PEER_TOPK
    kd = keys_ref.shape[-1]
    for h in range(PEER_HEADS):
        sc = []
        for p in range(2):
            c0 = (2 * h + p) * kd
            sc.append(_nt_dot(keys_ref[p], q_ref[:, c0:c0 + kd], precision=lax.Precision.HIGHEST))
        a = _top_sorted(sc[0])
        b = _top_sorted(sc[1])
        cand = _pair_sum_candidates(a[:K], b[:K])
        tau = _top16(cand)[K - 1][0:1]
        below = jnp.max(jnp.where(cand < tau, cand, -jnp.inf), axis=0, keepdims=True)
        for corner in (a[0:1] + b[K:K + 1], a[K:K + 1] + b[0:1]):
            below = jnp.maximum(below, jnp.where(corner < tau, corner, -jnp.inf))
        below = jnp.where(below == -jnp.inf, tau, below)
        cut = 0.5 * (tau + below)
        top = a[0:1] + b[0:1]
        z = jnp.sum(jnp.where(cand >= tau, jnp.exp(cand - top), 0.0), axis=0, keepdims=True)
        thr = cut - sc[0]
        e1 = jnp.exp(sc[0] - a[0:1]) * (0.5 / z)
        for c in range(thr_ref.shape[1]):
            thr_ref[h, c] = thr[:, c * LANES:(c + 1) * LANES]
            e1_ref[h, c] = e1[:, c * LANES:(c + 1) * LANES]
        s2_ref[h] = sc[1]
        e2_ref[h] = jnp.exp(sc[1] - b[0:1])


def _peer_route(q, sub_keys, tt):
    T = q.shape[0]
    tt = min(tt, T)
    nk = PEER_N_KEYS
    assert tt % LANES == 0
    big = jax.ShapeDtypeStruct((PEER_HEADS, nk, T), F32)
    big_spec = pl.BlockSpec((PEER_HEADS, nk, tt), lambda i: (0, 0, i))
    slab = jax.ShapeDtypeStruct((PEER_HEADS, T // LANES, nk, LANES), F32)
    slab_spec = pl.BlockSpec((PEER_HEADS, tt // LANES, nk, LANES), lambda i: (0, i, 0, 0))
    return pl.pallas_call(
        _peer_route_body,
        grid=(T // tt,),
        in_specs=[pl.BlockSpec((tt, q.shape[1]), lambda i: (i, 0)),
                  pl.BlockSpec(sub_keys.shape, lambda i: (0, 0, 0))],
        out_specs=[slab_spec, slab_spec, big_spec, big_spec],
        out_shape=[slab, slab, big, big],
        compiler_params=_params("parallel"),
        name="peer_route",
    )(q, sub_keys.astype(F32))


def _peer_main_body(h_ref, u_ref, vt_ref, thr_ref, e1_ref, s2_ref, e2_ref, x_ref, g_ref,
                    o_ref, acc_ref, pre_ref, ga_ref, ht_ref, *, rows_per_step):
    nk = PEER_N_KEYS
    lanes = LANES
    j = pl.program_id(1)
    tt = h_ref.shape[0]

    @pl.when(j == 0)
    def _():
        acc_ref[...] = jnp.zeros_like(acc_ref)
        ht_ref[...] = h_ref[...].astype(F32).T.astype(ht_ref.dtype)

    i1_base = pl.multiple_of(j * rows_per_step, rows_per_step)
    group = 4
    sub = 64
    n_sub = nk // sub

    pre_ref[...] = jnp.dot(u_ref[...], ht_ref[...], preferred_element_type=F32)

    def gate_tile(ti, carry):
        cs = pl.ds(pl.multiple_of((ti // n_sub) * lanes, lanes), lanes)
        k0 = pl.multiple_of((ti % n_sub) * sub, sub)
        for r0 in range(0, rows_per_step, group):
            gates = [None] * group
            for h in range(PEER_HEADS):
                s2 = s2_ref[h, pl.ds(k0, sub), cs]
                e2 = e2_ref[h, pl.ds(k0, sub), cs]
                for g in range(group):
                    r = r0 + g
                    row = pl.ds(i1_base + r, SUBLANES, stride=0)
                    thr = thr_ref[h, ti // n_sub, row, :]
                    e1 = e1_ref[h, ti // n_sub, row, :]
                    tiles = (sub // SUBLANES, SUBLANES, lanes)
                    w = (jnp.where(s2.reshape(tiles) >= thr, e2.reshape(tiles), 0.0) * e1).reshape(sub, lanes)
                    gates[g] = w if gates[g] is None else gates[g] + w
            for g in range(group):
                rs = pl.ds(pl.multiple_of((r0 + g) * nk + k0, sub), sub)
                pre = pre_ref[rs, cs]
                act = pre * (1.0 + lax.erf(pre * (2.0 ** -0.5)))
                ga_ref[rs, cs] = (gates[g] * act).astype(ga_ref.dtype)
        return carry

    lax.fori_loop(0, (tt // lanes) * n_sub, gate_tile, 0)
    acc_ref[...] += jnp.dot(vt_ref[0], ga_ref[...], preferred_element_type=F32)

    @pl.when(j == pl.num_programs(1) - 1)
    def _():
        o_ref[...] = _rms(x_ref[...] + acc_ref[...].T, g_ref[...])


def _peer_main(hn, u, v, thr, e1, s2, e2, x2d, g_final, tt, rows_per_step):
    T, D = x2d.shape
    tt = min(tt, T)
    nk = PEER_N_KEYS
    eb = rows_per_step * nk
    n_exp = u.shape[0]
    assert n_exp == nk * nk and nk % rows_per_step == 0
    u = u.astype(BF16)
    vt = v.astype(BF16).reshape(n_exp // eb, eb, D).transpose(0, 2, 1)
    score_spec = pl.BlockSpec((PEER_HEADS, nk, tt), lambda i, j: (0, 0, i))
    slab_spec = pl.BlockSpec((PEER_HEADS, tt // LANES, nk, LANES), lambda i, j: (0, i, 0, 0))
    return pl.pallas_call(
        functools.partial(_peer_main_body, rows_per_step=rows_per_step),
        grid=(T // tt, n_exp // eb),
        in_specs=[pl.BlockSpec((tt, D), lambda i, j: (i, 0)),
                  pl.BlockSpec((eb, D), lambda i, j: (j, 0)),
                  pl.BlockSpec((1, D, eb), lambda i, j: (j, 0, 0)),
                  slab_spec, slab_spec, score_spec, score_spec,
                  pl.BlockSpec((tt, D), lambda i, j: (i, 0)),
                  pl.BlockSpec((1, D), lambda i, j: (0, 0))],
        out_specs=pl.BlockSpec((tt, D), lambda i, j: (i, 0)),
        out_shape=jax.ShapeDtypeStruct((T, D), F32),
        scratch_shapes=[pltpu.VMEM((D, tt), F32), pltpu.VMEM((eb, tt), F32), pltpu.VMEM((eb, tt), BF16),
                        pltpu.VMEM((D, tt), BF16)],
        compiler_params=_params("parallel", "arbitrary"),
        name="peer_main",
    )(hn, u, vt, thr, e1, s2, e2, x2d, g_final.reshape(1, D).astype(F32))


def kernel(x, mem, rel_bias, g_mix, w_in, b_gate, w_branch_a, w_branch_b, w_out, g_cross, g_mem, w_q_cross,
           w_kv_cross, w_o_cross, g_ffn, w_q_peer, peer_sub_keys, peer_u, peer_v, g_final):
    B, S, D = x.shape
    T = B * S
    depth = w_in.shape[0]
    ndil = len(DIL_GROUPS) * DIL_HEADS
    x2d = x.reshape(T, D)

    dil_bias = [_bias_lookup(_dil_bucket_map(d), rel_bias[:, gi * DIL_HEADS:(gi + 1) * DIL_HEADS], f"dil_bias_g{gi}")
                for gi, (_, d) in enumerate(DIL_GROUPS)]
    mmap = _moba_bucket_map()
    moba_bias = _bias_lookup(mmap.reshape(-1, MOBA_BLOCK), rel_bias[:, ndil:], "moba_bias")
    moba_bias = moba_bias.reshape(MOBA_HEADS // 2, 2, MOBA_BIAS_TILES, MOBA_BLOCK, MOBA_BLOCK)

    assert depth == 1, "the final norm is fused into the last PEER sweep; one layer supported"
    l = 0
    gc = DIL_GROUP_COLS
    assert DIL_GROUPS[0][1] == 1
    w_tok = jnp.concatenate([w_in[l][:, :gc], w_in[l][:, DIL_WIDTH:QKV_WIDTH]], axis=1)
    tok = _norm_matmul(x2d, g_mix[l], w_tok, out_dtype=BF16, tm=1024, tn=gc, name="proj_tok")
    dil_qkv = [tok.reshape(B, 1, S, -1)]
    for gi, (_, d) in enumerate(DIL_GROUPS[1:], start=1):
        dil_qkv.append(_norm_matmul(x2d, g_mix[l], w_in[l][:, gi * gc:(gi + 1) * gc], out_dtype=BF16, tm=1024, tn=gc,
                                    dil=d, seq=S, name=f"proj_dil_g{gi}"))
    gates = _norm_matmul(x2d, g_mix[l], w_in[l][:, QKV_WIDTH:], out_dtype=BF16, tm=1024, tn=1024,
                         bias=b_gate[l], name="proj_gates")
    dil_out = [_dilated_group(a, dil_bias[gi], f"dilated_g{gi}") for gi, a in enumerate(dil_qkv)]
    o_moba = _moba(tok.reshape(B, S, -1), moba_bias, first_col=gc).reshape(T, MOBA_HEADS * HEAD_DIM)
    x2d = _merge(dil_out, o_moba, gates, x2d, w_branch_a[l], w_branch_b[l], w_out[l], seq=S, tm=512)
    M = mem.shape[1]
    q_c = _norm_matmul(x2d, g_cross[l], w_q_cross[l], out_dtype=BF16, tm=1024, tn=512, name="cross_q")
    kv = _norm_matmul(mem.reshape(B * M, D), g_mem[l], w_kv_cross[l], out_dtype=BF16, tm=1024, tn=1024, name="cross_kv")
    x2d = _cross(q_c.reshape(B, S, -1), kv.reshape(B, M, -1), x2d.reshape(B, S, D), w_o_cross[l], tm=1024).reshape(T, D)
    q_p, hn = _norm_matmul(x2d, g_ffn[l], w_q_peer[l], out_dtype=F32, tm=1024, tn=1024, emit_normed=True, name="peer_q")
    thr, e1, s2, e2 = _peer_route(q_p, peer_sub_keys[l], tt=512)
    y = _peer_main(hn, peer_u[l], peer_v[l], thr, e1, s2, e2, x2d, g_final,
                   tt=512, rows_per_step=8)
    return y.reshape(B, S, D)
```

```python
import functools
import math

import jax
import jax.numpy as jnp
import numpy as np
from jax import lax
from jax.experimental import pallas as pl
from jax.experimental.pallas import tpu as pltpu

HEAD_DIM = 64
DIL_GROUPS = ((128, 1), (512, 4), (2048, 16))
DIL_HEADS = 4
DIL_BLOCK = 128
MOBA_HEADS = 8
MOBA_BLOCK = 256
MOBA_TOPK = 3
MOBA_BIAS_TILES = 8
MOBA_GROUP = 4
MOBA_SPLIT = 2
MOBA_MASK = -(2.0 ** 100)
REL_BUCKETS = 32
REL_MAX_DIST = 2048
CROSS_HEADS = 4
CROSS_HEAD_DIM = 128
PEER_HEADS = 8
PEER_N_KEYS = 128
PEER_TOPK = 16
RMS_EPS = 1e-6
NEG_INF = -1e30
DIL_GROUP_COLS = 3 * DIL_HEADS * HEAD_DIM
DIL_WIDTH = len(DIL_GROUPS) * DIL_GROUP_COLS
MOBA_WIDTH = 3 * MOBA_HEADS * HEAD_DIM
QKV_WIDTH = DIL_WIDTH + MOBA_WIDTH

LANES = 128
SUBLANES = 8
V7X_VMEM_BYTES = 64 * 1024 * 1024
VMEM_LIMIT = V7X_VMEM_BYTES * 3 // 4

BF16 = jnp.bfloat16
F32 = jnp.float32


def _params(*sem):
    return pltpu.CompilerParams(dimension_semantics=sem, vmem_limit_bytes=VMEM_LIMIT)


def _rms(x, g):
    return x * lax.rsqrt(jnp.mean(x * x, axis=-1, keepdims=True) + RMS_EPS) * g


def _nt_dot(a, b, precision=None):
    return lax.dot_general(a, b, (((1,), (1,)), ((), ())), preferred_element_type=F32, precision=precision)


def _norm_matmul_body(x_ref, g_ref, w_ref, *rest, sigmoid_bias, dil, emit_normed):
    rest = list(rest)
    b_ref = rest.pop(0) if sigmoid_bias else None
    o_ref = rest.pop(0)
    hn_ref = rest.pop(0) if emit_normed else None
    rows = x_ref.shape[0] // dil

    if dil == 1:
        h = _rms(x_ref[...], g_ref[...]).astype(BF16)
    else:
        cols_ref = rest.pop(0)
        n_slab = cols_ref.shape[0]
        for c in range(n_slab):
            cols_ref[c] = x_ref[:, c * LANES:(c + 1) * LANES]
        h = jnp.concatenate(
            [_rms(jnp.concatenate([cols_ref[c, pl.ds(r, rows, stride=dil), :] for c in range(n_slab)], axis=-1),
                  g_ref[...]).astype(BF16) for r in range(dil)], axis=0)
    if emit_normed:
        hn_ref[...] = h

    acc = jnp.dot(h, w_ref[...], preferred_element_type=F32)
    if sigmoid_bias:
        acc = jax.nn.sigmoid(acc + b_ref[...])
    if dil == 1:
        o_ref[...] = acc.astype(o_ref.dtype)
    else:
        for r in range(dil):
            o_ref[0, r] = acc[r * rows:(r + 1) * rows].astype(o_ref.dtype)


def _norm_matmul(x2d, g, w, *, out_dtype, tm, tn, bias=None, dil=1, seq=None, emit_normed=False, name):
    T, D = x2d.shape
    N = w.shape[1]
    tm = min(tm, T if seq is None else seq)
    assert T % tm == 0 and N % tn == 0
    in_specs = [
        pl.BlockSpec((tm, D), lambda i, j: (i, 0)),
        pl.BlockSpec((1, D), lambda i, j: (0, 0)),
        pl.BlockSpec((D, tn), lambda i, j: (0, j)),
    ]
    args = [x2d, g.reshape(1, D).astype(F32), w.astype(BF16)]
    if bias is not None:
        in_specs.append(pl.BlockSpec((1, tn), lambda i, j: (0, j)))
        args.append(bias.reshape(1, N).astype(F32))
    assert not (emit_normed and dil > 1)
    scratch = []
    if dil == 1:
        out_spec = pl.BlockSpec((tm, tn), lambda i, j: (i, j))
        out_shape = jax.ShapeDtypeStruct((T, N), out_dtype)
    else:
        assert seq % tm == 0 and tm % (2 * SUBLANES * dil) == 0 and D % LANES == 0
        per_seq = seq // tm
        out_spec = pl.BlockSpec((1, dil, tm // dil, tn), lambda i, j: (i // per_seq, 0, i % per_seq, j))
        out_shape = jax.ShapeDtypeStruct((T // seq, dil, seq // dil, N), out_dtype)
        scratch.append(pltpu.VMEM((D // LANES, tm, LANES), F32))
    if emit_normed:
        out_spec = [out_spec, pl.BlockSpec((tm, D), lambda i, j: (i, 0))]
        out_shape = [out_shape, jax.ShapeDtypeStruct((T, D), BF16)]
    return pl.pallas_call(
        functools.partial(_norm_matmul_body, sigmoid_bias=bias is not None, dil=dil, emit_normed=emit_normed),
        grid=(T // tm, N // tn),
        in_specs=in_specs,
        out_specs=out_spec,
        out_shape=out_shape,
        scratch_shapes=scratch,
        compiler_params=_params("parallel", "arbitrary"),
        name=name,
    )(*args)


def _t5_bucket_np(n):
    max_exact = REL_BUCKETS // 2
    nf = np.maximum(n, 1).astype(np.float32)
    large = max_exact + (np.log(nf / np.float32(max_exact)) / np.float32(math.log(REL_MAX_DIST / max_exact))
                         * np.float32(REL_BUCKETS - max_exact)).astype(np.int32)
    large = np.minimum(large, REL_BUCKETS - 1)
    return np.where(n < max_exact, n, large).astype(np.int32)


def _dil_bucket_map(d):
    blk = DIL_BLOCK
    rel = (blk + np.arange(blk))[:, None] - np.arange(2 * blk)[None, :]
    return _t5_bucket_np(np.clip(rel, 0, None) * d)


def _moba_bucket_map():
    blk = MOBA_BLOCK
    rel = np.arange(blk)[:, None] - np.arange(blk)[None, :]
    tiles = [_t5_bucket_np(np.clip(rel + dlt * blk, 0, None)) for dlt in range(MOBA_BIAS_TILES)]
    tiles[0] = np.where(rel >= 0, tiles[0], -1)
    far = _t5_bucket_np(np.arange((MOBA_BIAS_TILES - 2) * blk + 1, 64 * blk))
    assert (far == REL_BUCKETS - 1).all()
    return np.stack(tiles)


def _bias_lookup_body(idx_ref, range_ref, tab_ref, o_ref):
    h, i = pl.program_id(0), pl.program_id(1)
    idx = idx_ref[...]

    def bucket(b, acc):
        return jnp.where(idx == b, tab_ref[b, h], acc)

    o_ref[0] = lax.fori_loop(range_ref[i, 0], range_ref[i, 1], bucket, jnp.full(idx.shape, NEG_INF, F32))


def _bias_lookup(bucket_map, table, name):
    R, C = bucket_map.shape
    H = table.shape[1]
    rt = min(R, 256)
    assert R % rt == 0
    tiles = bucket_map.reshape(R // rt, rt * C)
    ranges = np.stack([np.where(tiles >= 0, tiles, REL_BUCKETS).min(axis=1), tiles.max(axis=1) + 1], axis=1)
    return pl.pallas_call(
        _bias_lookup_body,
        grid=(H, R // rt),
        in_specs=[pl.BlockSpec((rt, C), lambda h, i: (i, 0)), pl.BlockSpec(memory_space=pltpu.SMEM),
                  pl.BlockSpec(memory_space=pltpu.SMEM)],
        out_specs=pl.BlockSpec((1, rt, C), lambda h, i: (h, i, 0)),
        out_shape=jax.ShapeDtypeStruct((H, R, C), F32),
        compiler_params=_params("parallel", "parallel"),
        name=name,
    )(jnp.asarray(bucket_map), jnp.asarray(ranges.astype(np.int32)), table.astype(F32))


def _dilated_body(q_ref, kp_ref, kc_ref, vp_ref, vc_ref, bias_ref, o_ref, *, qblocks):
    blk, hd = DIL_BLOCK, HEAD_DIM
    gw = DIL_HEADS * hd
    n = pl.program_id(2)
    k = jnp.concatenate([kp_ref[0, 0], kc_ref[0, 0]], axis=0)
    v = jnp.concatenate([vp_ref[0, 0], vc_ref[0, 0]], axis=0)
    lane = lax.broadcasted_iota(jnp.int32, (1, gw), 1)
    qi = lax.broadcasted_iota(jnp.int32, (blk, 2 * blk), 0)
    kj = lax.broadcasted_iota(jnp.int32, (blk, 2 * blk), 1)
    rel = blk + qi - kj
    band = (rel >= 0) & (rel <= blk)
    first_key = jnp.where(n > 0, 0, blk)
    for i in range(qblocks):
        rows = slice(i * blk, (i + 1) * blk)
        q = q_ref[0, 0, rows, :]
        kw = k[i * blk:(i + 2) * blk]
        vw = v[i * blk:(i + 2) * blk]
        mask = band & (kj >= first_key) if i == 0 else band
        head_lanes = [(lane >= h * hd) & (lane < (h + 1) * hd) for h in range(DIL_HEADS)]
        q_all = jnp.concatenate([jnp.where(hl, q, jnp.zeros_like(q)) for hl in head_lanes], axis=0)
        s = _nt_dot(q_all, kw) * (hd ** -0.5) + bias_ref[...]
        s = jnp.where(jnp.concatenate([mask] * DIL_HEADS, axis=0), s, NEG_INF)
        m = jnp.max(s, axis=-1, keepdims=True)
        p = jnp.exp(s - m)
        l = jnp.sum(p, axis=-1, keepdims=True)
        pv = jnp.dot(p.astype(BF16), vw, preferred_element_type=F32) / l
        lse_all = m + jnp.log(l)
        out = jnp.zeros((blk, gw), F32)
        lse = jnp.zeros((blk, gw), F32)
        for h, hl in enumerate(head_lanes):
            out = jnp.where(hl, pv[h * blk:(h + 1) * blk], out)
            lse = jnp.where(hl, lse_all[h * blk:(h + 1) * blk], lse)
        o_ref[0, 0, rows, :gw] = out
        o_ref[0, 0, rows, gw:] = lse


def _dilated_group(qkv, bias, name):
    B, d, L, _ = qkv.shape
    blk, gw = DIL_BLOCK, DIL_HEADS * HEAD_DIM
    assert L % blk == 0
    qblocks = min(4, L // blk)
    assert L % (qblocks * blk) == 0
    rows = qblocks * blk

    def spec(j, prev):
        if prev:
            return pl.BlockSpec((1, 1, blk, gw), lambda b, r, n: (b, r, jnp.maximum(n * qblocks - 1, 0), j))
        return pl.BlockSpec((1, 1, rows, gw), lambda b, r, n: (b, r, n, j))

    return pl.pallas_call(
        functools.partial(_dilated_body, qblocks=qblocks),
        grid=(B, d, L // rows),
        in_specs=[spec(0, False), spec(1, True), spec(1, False), spec(2, True), spec(2, False),
                  pl.BlockSpec((DIL_HEADS * blk, 2 * blk), lambda b, r, n: (0, 0))],
        out_specs=pl.BlockSpec((1, 1, rows, 2 * gw), lambda b, r, n: (b, r, n, 0)),
        out_shape=jax.ShapeDtypeStruct((B, d, L, 2 * gw), F32),
        compiler_params=_params("parallel", "parallel", "arbitrary"),
        name=name,
    )(qkv, qkv, qkv, qkv, qkv, bias.reshape(DIL_HEADS * blk, 2 * blk))


def _moba_body(q_ref, k_ref, v_ref, bias_ref, o_ref, qa_ref, ka_ref, *, S):
    blk, hd = MOBA_BLOCK, HEAD_DIM
    nblk = S // blk
    scale = hd ** -0.5
    lane = lax.broadcasted_iota(jnp.int32, (1, 2 * hd), 1)
    kmean = jnp.mean(k_ref[0].astype(F32).reshape(nblk, blk, 2 * hd), axis=1)
    grp = MOBA_GROUP * blk
    row_blk = lax.broadcasted_iota(jnp.int32, (S, 2 * hd), 0) // blk
    q2, k2 = q_ref[0], k_ref[0]

    ka_ref[:, :2 * hd] = k2
    ka_ref[:, 2 * hd:] = jnp.where(lane == row_blk, 1.0, 0.0).astype(BF16)
    blk_t = lax.broadcasted_iota(jnp.int32, (nblk, S), 0)
    q_blk_t = lax.broadcasted_iota(jnp.int32, (nblk, S), 1) // blk
    past = blk_t < q_blk_t
    blk_f = blk_t.astype(F32)
    for h in range(2):
        head_lanes = (lane >= h * hd) & (lane < (h + 1) * hd)
        qh = jnp.where(head_lanes, q2, jnp.zeros_like(q2))
        gs = _nt_dot(jnp.where(head_lanes, kmean, 0.0), qh.astype(F32), precision=lax.Precision.HIGHEST)
        gs = jnp.where(past, gs, NEG_INF)
        sel = jnp.zeros(gs.shape, F32)
        for _r in range(MOBA_TOPK):
            gmax = jnp.max(gs, axis=0, keepdims=True)
            first = jnp.min(jnp.where(gs == gmax, blk_f, float(nblk)), axis=0, keepdims=True)
            pick = blk_f == first
            sel = jnp.where(pick & past, 1.0, sel)
            gs = jnp.where(pick, -jnp.inf, gs)
        penalty_t = jnp.where((sel > 0.5) | (blk_t == q_blk_t), 0.0, MOBA_MASK)
        penalty_t = jnp.concatenate([penalty_t, jnp.zeros((2 * hd - nblk, S), F32)], axis=0)
        qa_ref[h, :, :2 * hd] = qh * scale
        qa_ref[h, :, 2 * hd:] = penalty_t.T.astype(BF16)

    def q_block(qb, _):
        row0 = pl.multiple_of(qb * blk, blk)
        n_grp = qb // MOBA_GROUP + 1
        qa = jnp.concatenate([qa_ref[h, pl.ds(row0, blk), :] for h in range(2)], axis=0)

        def key_group(i, carry):
            m, l, acc = carry
            g = n_grp - 1 - i
            col0 = pl.multiple_of(g * grp, grp)
            per = MOBA_GROUP // MOBA_SPLIT
            half = per * blk
            cols = [pl.ds(pl.multiple_of(col0 + i * half, half), half) for i in range(MOBA_SPLIT)]
            scores = [_nt_dot(qa, ka_ref[c, :]) for c in cols]
            for i in range(MOBA_SPLIT):
                parts = []
                for jj in range(per):
                    n = g * MOBA_GROUP + i * per + jj
                    tile = jnp.clip(qb - n, 0, MOBA_BIAS_TILES - 1)
                    bias = jnp.concatenate([bias_ref[0, h, tile] for h in range(2)], axis=0)
                    parts.append(scores[i][:, jj * blk:(jj + 1) * blk] + bias)
                m_new = jnp.maximum(m, jnp.max(functools.reduce(jnp.maximum, parts), axis=-1, keepdims=True))
                a = jnp.exp(m - m_new)
                ps = [jnp.exp(p - m_new) for p in parts]
                l = l * a + jnp.sum(functools.reduce(jnp.add, ps), axis=-1, keepdims=True)
                p_all = jnp.concatenate([p.astype(BF16) for p in ps], axis=-1)
                pv = jnp.dot(p_all, v_ref[0, cols[i], :], preferred_element_type=F32)
                m, acc = m_new, acc * a + pv
            return m, l, acc

        init = (jnp.full((2 * blk, 1), -jnp.inf, F32), jnp.zeros((2 * blk, 1), F32), jnp.zeros((2 * blk, 2 * hd), F32))
        _, l, acc = lax.fori_loop(0, n_grp, key_group, init)
        o = acc / l
        o_ref[0, pl.ds(row0, blk), :] = jnp.where(lane < hd, o[:blk], o[blk:]).astype(o_ref.dtype)
        return 0

    lax.fori_loop(0, nblk, q_block, 0)


def _moba(qkv, bias, first_col):
    B, S, C = qkv.shape
    pw = 2 * HEAD_DIM
    npair = MOBA_HEADS // 2
    assert S % (MOBA_GROUP * MOBA_BLOCK) == 0 and S // MOBA_BLOCK <= HEAD_DIM and first_col % pw == 0
    c0 = first_col // pw

    def spec(j):
        return pl.BlockSpec((1, S, pw), lambda hp, b: (b, 0, c0 + j * npair + hp))

    return pl.pallas_call(
        functools.partial(_moba_body, S=S),
        grid=(npair, B),
        in_specs=[spec(0), spec(1), spec(2),
                  pl.BlockSpec((1, 2, MOBA_BIAS_TILES, MOBA_BLOCK, MOBA_BLOCK), lambda hp, b: (hp, 0, 0, 0, 0))],
        out_specs=pl.BlockSpec((1, S, pw), lambda hp, b: (b, 0, hp)),
        out_shape=jax.ShapeDtypeStruct((B, S, MOBA_HEADS * HEAD_DIM), BF16),
        scratch_shapes=[pltpu.VMEM((2, S, 2 * pw), BF16), pltpu.VMEM((S, 2 * pw), BF16)],
        compiler_params=_params("parallel", "parallel"),
        name="moba",
    )(qkv, qkv, qkv, bias)


def _merge_body(d0_ref, d1_ref, d2_ref, om_ref, gate_ref, x_ref, wa_ref, wb_ref, wo_ref, x1_ref, t1_ref, t2_ref):
    D = x_ref.shape[-1]
    gw = DIL_HEADS * HEAD_DIM
    n_slab = t1_ref.shape[0]
    for src, dst in ((d1_ref, t1_ref), (d2_ref, t2_ref)):
        dil, rows = src.shape[1], src.shape[2]
        for r in range(dil):
            for c in range(n_slab):
                dst[c, pl.ds(r, rows, stride=dil), :] = src[0, r, :, c * LANES:(c + 1) * LANES]
    g0 = d0_ref[...]
    g1 = jnp.concatenate([t1_ref[c] for c in range(n_slab)], axis=-1)
    g2 = jnp.concatenate([t2_ref[c] for c in range(n_slab)], axis=-1)
    l0, l1, l2 = g0[:, gw:], g1[:, gw:], g2[:, gw:]
    mx = jnp.maximum(jnp.maximum(l0, l1), l2)
    e0, e1, e2 = jnp.exp(l0 - mx), jnp.exp(l1 - mx), jnp.exp(l2 - mx)
    o_dil = (e0 * g0[:, :gw] + e1 * g1[:, :gw] + e2 * g2[:, :gw]) / (e0 + e1 + e2)
    a = jnp.dot(o_dil.astype(BF16), wa_ref[...], preferred_element_type=F32)
    b = jnp.dot(om_ref[...], wb_ref[...], preferred_element_type=F32)
    merged = gate_ref[:, :D].astype(F32) * a + gate_ref[:, D:].astype(F32) * b
    x1_ref[...] = x_ref[...] + jnp.dot(merged.astype(BF16), wo_ref[...], preferred_element_type=F32)


def _merge(dil_out, o_moba, gates, x2d, w_a, w_b, w_o, seq, tm):
    T, D = x2d.shape
    tm = min(tm, seq)
    per_seq = seq // tm
    assert seq % tm == 0 and dil_out[0].shape[1] == 1

    def rows(width):
        return pl.BlockSpec((tm, width), lambda i: (i, 0))

    def whole(w):
        return pl.BlockSpec(w.shape, lambda i: (0, 0))

    def residue_major(a):
        d, width = a.shape[1], a.shape[3]
        assert tm % (8 * d) == 0
        return pl.BlockSpec((1, d, tm // d, width), lambda i: (i // per_seq, 0, i % per_seq, 0))

    ws = [w_a.astype(BF16), w_b.astype(BF16), w_o.astype(BF16)]
    width = dil_out[0].shape[-1]
    return pl.pallas_call(
        _merge_body,
        grid=(T // tm,),
        in_specs=[rows(width), residue_major(dil_out[1]), residue_major(dil_out[2]),
                  rows(o_moba.shape[-1]), rows(2 * D), rows(D)] + [whole(w) for w in ws],
        out_specs=rows(D),
        out_shape=jax.ShapeDtypeStruct((T, D), F32),
        scratch_shapes=[pltpu.VMEM((width // LANES, tm, LANES), F32)] * 2,
        compiler_params=_params("parallel"),
        name="merge",
    )(dil_out[0].reshape(T, width), dil_out[1], dil_out[2], o_moba, gates, x2d, *ws)


def _cross_body(q_ref, kv_ref, x_ref, wo_ref, x2_ref):
    hd = CROSS_HEAD_DIM
    width = CROSS_HEADS * hd
    heads = []
    for h in range(CROSS_HEADS):
        sl = slice(h * hd, (h + 1) * hd)
        s = _nt_dot(q_ref[0, :, sl], kv_ref[0, :, sl]) * (hd ** -0.5)
        m = jnp.max(s, axis=-1, keepdims=True)
        p = jnp.exp(s - m)
        l = jnp.sum(p, axis=-1, keepdims=True)
        v = kv_ref[0, :, width + h * hd: width + (h + 1) * hd]
        heads.append((jnp.dot(p.astype(BF16), v, preferred_element_type=F32) / l).astype(BF16))
    o = jnp.concatenate(heads, axis=-1)
    x2_ref[0] = x_ref[0] + jnp.dot(o, wo_ref[...], preferred_element_type=F32)


def _cross(q, kv, x, w_o, tm):
    B, S, D = x.shape
    M = kv.shape[1]
    tm = min(tm, S)
    w_o = w_o.astype(BF16)
    return pl.pallas_call(
        _cross_body,
        grid=(B, S // tm),
        in_specs=[pl.BlockSpec((1, tm, q.shape[-1]), lambda b, i: (b, i, 0)),
                  pl.BlockSpec((1, M, kv.shape[-1]), lambda b, i: (b, 0, 0)),
                  pl.BlockSpec((1, tm, D), lambda b, i: (b, i, 0)),
                  pl.BlockSpec(w_o.shape, lambda b, i: (0, 0))],
        out_specs=pl.BlockSpec((1, tm, D), lambda b, i: (b, i, 0)),
        out_shape=jax.ShapeDtypeStruct((B, S, D), F32),
        compiler_params=_params("parallel", "parallel"),
        name="cross",
    )(q, kv, x, w_o)


def _merge_exchange_pairs(n):
    pairs = []
    p = 1
    while p < n:
        k = p
        while k >= 1:
            for j in range(k % p, n - k, 2 * k):
                for i in range(min(k, n - j - k)):
                    if (i + j) // (2 * p) == (i + j + k) // (2 * p):
                        pairs.append((i + j, i + j + k))
            k //= 2
        p *= 2
    return pairs


def _top16(s):
    n, sub = PEER_TOPK, SUBLANES
    assert s.shape[0] % sub == 0 and sub * n >= s.shape[0] >= 2 * sub
    v = [s[k * sub:(k + 1) * sub] for k in range(s.shape[0] // sub)] + [None] * (n - s.shape[0] // sub)

    def exchange(i, j):
        if v[j] is None:
            return
        if v[i] is None:
            v[i], v[j] = v[j], None
        else:
            v[i], v[j] = jnp.maximum(v[i], v[j]), jnp.minimum(v[i], v[j])

    for i, j in _merge_exchange_pairs(n):
        exchange(i, j)
    for shift in (sub // 2, sub // 4, sub // 8):
        w = [None if x is None else pltpu.roll(x, shift, 0) for x in v]
        for k in range(n):
            other = w[n - 1 - k]
            if other is not None:
                v[k] = other if v[k] is None else jnp.maximum(v[k], other)
        d = n // 2
        while d >= 1:
            for i in range(n):
                if i & d == 0:
                    exchange(i, i + d)
            d //= 2
    return v


def _top_sorted(s):
    v = _top16(s)
    nxt = jnp.max(jnp.where(s < v[-1][0:1], s, -jnp.inf), axis=0, keepdims=True)
    return jnp.concatenate([x[0:1] for x in v] + [nxt], axis=0)


def _pair_sum_candidates(a, b):
    assert PEER_TOPK == 16
    row = lax.broadcasted_iota(jnp.int32, (PEER_TOPK, 1), 0)
    blocks = [a[0:1] + b]
    blocks += [a[i:i + 1] + b[0:8] for i in (1, 2, 3)]
    blocks += [jnp.where(row >= 4, b[0:1] + a, -jnp.inf)]
    blocks += [jnp.where(row[0:8] >= 4, b[j:j + 1] + a[0:8], -jnp.inf) for j in (1, 2)]
    return jnp.concatenate(blocks, axis=0)


def _peer_route_body(q_ref, keys_ref, thr_ref, e1_ref, s2_ref, e2_ref):
    nk, K = PEER_N_KEYS, PEER_TOPK
    kd = keys_ref.shape[-1]
    for h in range(PEER_HEADS):
        sc = []
        for p in range(2):
            c0 = (2 * h + p) * kd
            sc.append(_nt_dot(keys_ref[p], q_ref[:, c0:c0 + kd], precision=lax.Precision.HIGHEST))
        a = _top_sorted(sc[0])
        b = _top_sorted(sc[1])
        cand = _pair_sum_candidates(a[:K], b[:K])
        tau = _top16(cand)[K - 1][0:1]
        below = jnp.max(jnp.where(cand < tau, cand, -jnp.inf), axis=0, keepdims=True)
        for corner in (a[0:1] + b[K:K + 1], a[K:K + 1] + b[0:1]):
            below = jnp.maximum(below, jnp.where(corner < tau, corner, -jnp.inf))
        below = jnp.where(below == -jnp.inf, tau, below)
        cut = 0.5 * (tau + below)
        top = a[0:1] + b[0:1]
        z = jnp.sum(jnp.where(cand >= tau, jnp.exp(cand - top), 0.0), axis=0, keepdims=True)
        thr = cut - sc[0]
        e1 = jnp.exp(sc[0] - a[0:1]) * (0.5 / z)
        for c in range(thr_ref.shape[1]):
            thr_ref[h, c] = thr[:, c * LANES:(c + 1) * LANES]
            e1_ref[h, c] = e1[:, c * LANES:(c + 1) * LANES]
        s2_ref[h] = sc[1]
        e2_ref[h] = jnp.exp(sc[1] - b[0:1])


def _peer_route(q, sub_keys, tt):
    T = q.shape[0]
    tt = min(tt, T)
    nk = PEER_N_KEYS
    assert tt % LANES == 0
    big = jax.ShapeDtypeStruct((PEER_HEADS, nk, T), F32)
    big_spec = pl.BlockSpec((PEER_HEADS, nk, tt), lambda i: (0, 0, i))
    slab = jax.ShapeDtypeStruct((PEER_HEADS, T // LANES, nk, LANES), F32)
    slab_spec = pl.BlockSpec((PEER_HEADS, tt // LANES, nk, LANES), lambda i: (0, i, 0, 0))
    return pl.pallas_call(
        _peer_route_body,
        grid=(T // tt,),
        in_specs=[pl.BlockSpec((tt, q.shape[1]), lambda i: (i, 0)),
                  pl.BlockSpec(sub_keys.shape, lambda i: (0, 0, 0))],
        out_specs=[slab_spec, slab_spec, big_spec, big_spec],
        out_shape=[slab, slab, big, big],
        compiler_params=_params("parallel"),
        name="peer_route",
    )(q, sub_keys.astype(F32))


def _peer_main_body(h_ref, u_ref, vt_ref, thr_ref, e1_ref, s2_ref, e2_ref, x_ref, g_ref,
                    o_ref, acc_ref, gate_ref, ga_ref, ht_ref, *, rows_per_step):
    nk = PEER_N_KEYS
    lanes = LANES
    j = pl.program_id(1)
    tt = h_ref.shape[0]

    @pl.when(j == 0)
    def _():
        acc_ref[...] = jnp.zeros_like(acc_ref)
        ht_ref[...] = h_ref[...].astype(F32).T.astype(ht_ref.dtype)

    i1_base = pl.multiple_of(j * rows_per_step, rows_per_step)
    group = 4
    sub = 64
    n_sub = nk // sub

    def gate_tile(ti, carry):
        cs = pl.ds(pl.multiple_of((ti // n_sub) * lanes, lanes), lanes)
        k0 = pl.multiple_of((ti % n_sub) * sub, sub)
        for r0 in range(0, rows_per_step, group):
            gates = [None] * group
            for h in range(PEER_HEADS):
                s2 = s2_ref[h, pl.ds(k0, sub), cs]
                e2 = e2_ref[h, pl.ds(k0, sub), cs]
                for g in range(group):
                    r = r0 + g
                    row = pl.ds(i1_base + r, SUBLANES, stride=0)
                    thr = thr_ref[h, ti // n_sub, row, :]
                    e1 = e1_ref[h, ti // n_sub, row, :]
                    tiles = (sub // SUBLANES, SUBLANES, lanes)
                    w = (jnp.where(s2.reshape(tiles) >= thr, e2.reshape(tiles), 0.0) * e1).reshape(sub, lanes)
                    gates[g] = w if gates[g] is None else gates[g] + w
            for g in range(group):
                rs = pl.ds(pl.multiple_of((r0 + g) * nk + k0, sub), sub)
                gate_ref[rs, cs] = gates[g]
        return carry

    lax.fori_loop(0, (tt // lanes) * n_sub, gate_tile, 0)
    pre = jnp.dot(u_ref[...], ht_ref[...], preferred_element_type=F32)
    ga_ref[...] = (gate_ref[...] * (pre * (1.0 + lax.erf(pre * (2.0 ** -0.5))))).astype(ga_ref.dtype)
    acc_ref[...] += jnp.dot(vt_ref[0], ga_ref[...], preferred_element_type=F32)

    @pl.when(j == pl.num_programs(1) - 1)
    def _():
        o_ref[...] = _rms(x_ref[...] + acc_ref[...].T, g_ref[...])


def _peer_main(hn, u, v, thr, e1, s2, e2, x2d, g_final, tt, rows_per_step):
    T, D = x2d.shape
    tt = min(tt, T)
    nk = PEER_N_KEYS
    eb = rows_per_step * nk
    n_exp = u.shape[0]
    assert n_exp == nk * nk and nk % rows_per_step == 0
    u = u.astype(BF16)
    vt = v.astype(BF16).reshape(n_exp // eb, eb, D).transpose(0, 2, 1)
    score_spec = pl.BlockSpec((PEER_HEADS, nk, tt), lambda i, j: (0, 0, i))
    slab_spec = pl.BlockSpec((PEER_HEADS, tt // LANES, nk, LANES), lambda i, j: (0, i, 0, 0))
    return pl.pallas_call(
        functools.partial(_peer_main_body, rows_per_step=rows_per_step),
        grid=(T // tt, n_exp // eb),
        in_specs=[pl.BlockSpec((tt, D), lambda i, j: (i, 0)),
                  pl.BlockSpec((eb, D), lambda i, j: (j, 0)),
                  pl.BlockSpec((1, D, eb), lambda i, j: (j, 0, 0)),
                  slab_spec, slab_spec, score_spec, score_spec,
                  pl.BlockSpec((tt, D), lambda i, j: (i, 0)),
                  pl.BlockSpec((1, D), lambda i, j: (0, 0))],
        out_specs=pl.BlockSpec((tt, D), lambda i, j: (i, 0)),
        out_shape=jax.ShapeDtypeStruct((T, D), F32),
        scratch_shapes=[pltpu.VMEM((D, tt), F32), pltpu.VMEM((eb, tt), F32), pltpu.VMEM((eb, tt), BF16),
                        pltpu.VMEM((D, tt), BF16)],
        compiler_params=_params("parallel", "arbitrary"),
        name="peer_main",
    )(hn, u, vt, thr, e1, s2, e2, x2d, g_final.reshape(1, D).astype(F32))


def kernel(x, mem, rel_bias, g_mix, w_in, b_gate, w_branch_a, w_branch_b, w_out, g_cross, g_mem, w_q_cross,
           w_kv_cross, w_o_cross, g_ffn, w_q_peer, peer_sub_keys, peer_u, peer_v, g_final):
    B, S, D = x.shape
    T = B * S
    depth = w_in.shape[0]
    ndil = len(DIL_GROUPS) * DIL_HEADS
    x2d = x.reshape(T, D)

    dil_bias = [_bias_lookup(_dil_bucket_map(d), rel_bias[:, gi * DIL_HEADS:(gi + 1) * DIL_HEADS], f"dil_bias_g{gi}")
                for gi, (_, d) in enumerate(DIL_GROUPS)]
    mmap = _moba_bucket_map()
    moba_bias = _bias_lookup(mmap.reshape(-1, MOBA_BLOCK), rel_bias[:, ndil:], "moba_bias")
    moba_bias = moba_bias.reshape(MOBA_HEADS // 2, 2, MOBA_BIAS_TILES, MOBA_BLOCK, MOBA_BLOCK)

    assert depth == 1, "the final norm is fused into the last PEER sweep; one layer supported"
    l = 0
    gc = DIL_GROUP_COLS
    assert DIL_GROUPS[0][1] == 1
    w_tok = jnp.concatenate([w_in[l][:, :gc], w_in[l][:, DIL_WIDTH:QKV_WIDTH]], axis=1)
    tok = _norm_matmul(x2d, g_mix[l], w_tok, out_dtype=BF16, tm=1024, tn=gc, name="proj_tok")
    dil_qkv = [tok.reshape(B, 1, S, -1)]
    for gi, (_, d) in enumerate(DIL_GROUPS[1:], start=1):
        dil_qkv.append(_norm_matmul(x2d, g_mix[l], w_in[l][:, gi * gc:(gi + 1) * gc], out_dtype=BF16, tm=1024, tn=gc,
                                    dil=d, seq=S, name=f"proj_dil_g{gi}"))
    gates = _norm_matmul(x2d, g_mix[l], w_in[l][:, QKV_WIDTH:], out_dtype=BF16, tm=1024, tn=1024,
                         bias=b_gate[l], name="proj_gates")
    dil_out = [_dilated_group(a, dil_bias[gi], f"dilated_g{gi}") for gi, a in enumerate(dil_qkv)]
    o_moba = _moba(tok.reshape(B, S, -1), moba_bias, first_col=gc).reshape(T, MOBA_HEADS * HEAD_DIM)
    x2d = _merge(dil_out, o_moba, gates, x2d, w_branch_a[l], w_branch_b[l], w_out[l], seq=S, tm=512)
    M = mem.shape[1]
    q_c = _norm_matmul(x2d, g_cross[l], w_q_cross[l], out_dtype=BF16, tm=1024, tn=512, name="cross_q")
    kv = _norm_matmul(mem.reshape(B * M, D), g_mem[l], w_kv_cross[l], out_dtype=BF16, tm=1024, tn=1024, name="cross_kv")
    x2d = _cross(q_c.reshape(B, S, -1), kv.reshape(B, M, -1), x2d.reshape(B, S, D), w_o_cross[l], tm=1024).reshape(T, D)
    q_p, hn = _norm_matmul(x2d, g_ffn[l], w_q_peer[l], out_dtype=F32, tm=1024, tn=1024, emit_normed=True, name="peer_q")
    thr, e1, s2, e2 = _peer_route(q_p, peer_sub_keys[l], tt=512)
    y = _peer_main(hn, peer_u[l], peer_v[l], thr, e1, s2, e2, x2d, g_final,
                   tt=512, rows_per_step=8)
    return y.reshape(B, S, D)
```

```python
import functools
import math

import jax
import jax.numpy as jnp
import numpy as np
from jax import lax
from jax.experimental import pallas as pl
from jax.experimental.pallas import tpu as pltpu

HEAD_DIM = 64
DIL_GROUPS = ((128, 1), (512, 4), (2048, 16))
DIL_HEADS = 4
DIL_BLOCK = 128
MOBA_HEADS = 8
MOBA_BLOCK = 256
MOBA_TOPK = 3
MOBA_BIAS_TILES = 8
MOBA_GROUP = 4
MOBA_SPLIT = 2
MOBA_MASK = -(2.0 ** 100)
REL_BUCKETS = 32
REL_MAX_DIST = 2048
CROSS_HEADS = 4
CROSS_HEAD_DIM = 128
PEER_HEADS = 8
PEER_N_KEYS = 128
PEER_TOPK = 16
RMS_EPS = 1e-6
NEG_INF = -1e30
DIL_GROUP_COLS = 3 * DIL_HEADS * HEAD_DIM
DIL_WIDTH = len(DIL_GROUPS) * DIL_GROUP_COLS
MOBA_WIDTH = 3 * MOBA_HEADS * HEAD_DIM
QKV_WIDTH = DIL_WIDTH + MOBA_WIDTH

LANES = 128
SUBLANES = 8
V7X_VMEM_BYTES = 64 * 1024 * 1024
VMEM_LIMIT = V7X_VMEM_BYTES * 3 // 4

BF16 = jnp.bfloat16
F32 = jnp.float32


def _params(*sem):
    return pltpu.CompilerParams(dimension_semantics=sem, vmem_limit_bytes=VMEM_LIMIT)


def _rms(x, g):
    return x * lax.rsqrt(jnp.mean(x * x, axis=-1, keepdims=True) + RMS_EPS) * g


def _nt_dot(a, b, precision=None):
    return lax.dot_general(a, b, (((1,), (1,)), ((), ())), preferred_element_type=F32, precision=precision)


def _norm_matmul_body(x_ref, g_ref, w_ref, *rest, sigmoid_bias, dil, emit_normed):
    rest = list(rest)
    b_ref = rest.pop(0) if sigmoid_bias else None
    o_ref = rest.pop(0)
    hn_ref = rest.pop(0) if emit_normed else None
    rows = x_ref.shape[0] // dil

    if dil == 1:
        h = _rms(x_ref[...], g_ref[...]).astype(BF16)
    else:
        cols_ref = rest.pop(0)
        n_slab = cols_ref.shape[0]
        for c in range(n_slab):
            cols_ref[c] = x_ref[:, c * LANES:(c + 1) * LANES]
        h = jnp.concatenate(
            [_rms(jnp.concatenate([cols_ref[c, pl.ds(r, rows, stride=dil), :] for c in range(n_slab)], axis=-1),
                  g_ref[...]).astype(BF16) for r in range(dil)], axis=0)
    if emit_normed:
        hn_ref[...] = h

    acc = jnp.dot(h, w_ref[...], preferred_element_type=F32)
    if sigmoid_bias:
        acc = jax.nn.sigmoid(acc + b_ref[...])
    if dil == 1:
        o_ref[...] = acc.astype(o_ref.dtype)
    else:
        for r in range(dil):
            o_ref[0, r] = acc[r * rows:(r + 1) * rows].astype(o_ref.dtype)


def _norm_matmul(x2d, g, w, *, out_dtype, tm, tn, bias=None, dil=1, seq=None, emit_normed=False, name):
    T, D = x2d.shape
    N = w.shape[1]
    tm = min(tm, T if seq is None else seq)
    assert T % tm == 0 and N % tn == 0
    in_specs = [
        pl.BlockSpec((tm, D), lambda i, j: (i, 0)),
        pl.BlockSpec((1, D), lambda i, j: (0, 0)),
        pl.BlockSpec((D, tn), lambda i, j: (0, j)),
    ]
    args = [x2d, g.reshape(1, D).astype(F32), w.astype(BF16)]
    if bias is not None:
        in_specs.append(pl.BlockSpec((1, tn), lambda i, j: (0, j)))
        args.append(bias.reshape(1, N).astype(F32))
    assert not (emit_normed and dil > 1)
    scratch = []
    if dil == 1:
        out_spec = pl.BlockSpec((tm, tn), lambda i, j: (i, j))
        out_shape = jax.ShapeDtypeStruct((T, N), out_dtype)
    else:
        assert seq % tm == 0 and tm % (2 * SUBLANES * dil) == 0 and D % LANES == 0
        per_seq = seq // tm
        out_spec = pl.BlockSpec((1, dil, tm // dil, tn), lambda i, j: (i // per_seq, 0, i % per_seq, j))
        out_shape = jax.ShapeDtypeStruct((T // seq, dil, seq // dil, N), out_dtype)
        scratch.append(pltpu.VMEM((D // LANES, tm, LANES), F32))
    if emit_normed:
        out_spec = [out_spec, pl.BlockSpec((tm, D), lambda i, j: (i, 0))]
        out_shape = [out_shape, jax.ShapeDtypeStruct((T, D), BF16)]
    return pl.pallas_call(
        functools.partial(_norm_matmul_body, sigmoid_bias=bias is not None, dil=dil, emit_normed=emit_normed),
        grid=(T // tm, N // tn),
        in_specs=in_specs,
        out_specs=out_spec,
        out_shape=out_shape,
        scratch_shapes=scratch,
        compiler_params=_params("parallel", "arbitrary"),
        name=name,
    )(*args)


def _t5_bucket_np(n):
    max_exact = REL_BUCKETS // 2
    nf = np.maximum(n, 1).astype(np.float32)
    large = max_exact + (np.log(nf / np.float32(max_exact)) / np.float32(math.log(REL_MAX_DIST / max_exact))
                         * np.float32(REL_BUCKETS - max_exact)).astype(np.int32)
    large = np.minimum(large, REL_BUCKETS - 1)
    return np.where(n < max_exact, n, large).astype(np.int32)


def _dil_bucket_map(d):
    blk = DIL_BLOCK
    rel = (blk + np.arange(blk))[:, None] - np.arange(2 * blk)[None, :]
    return _t5_bucket_np(np.clip(rel, 0, None) * d)


def _moba_bucket_map():
    blk = MOBA_BLOCK
    rel = np.arange(blk)[:, None] - np.arange(blk)[None, :]
    tiles = [_t5_bucket_np(np.clip(rel + dlt * blk, 0, None)) for dlt in range(MOBA_BIAS_TILES)]
    tiles[0] = np.where(rel >= 0, tiles[0], -1)
    far = _t5_bucket_np(np.arange((MOBA_BIAS_TILES - 2) * blk + 1, 64 * blk))
    assert (far == REL_BUCKETS - 1).all()
    return np.stack(tiles)


def _bias_lookup_body(idx_ref, range_ref, tab_ref, o_ref):
    h, i = pl.program_id(0), pl.program_id(1)
    idx = idx_ref[...]

    def bucket(b, acc):
        return jnp.where(idx == b, tab_ref[b, h], acc)

    o_ref[0] = lax.fori_loop(range_ref[i, 0], range_ref[i, 1], bucket, jnp.full(idx.shape, NEG_INF, F32))


def _bias_lookup(bucket_map, table, name):
    R, C = bucket_map.shape
    H = table.shape[1]
    rt = min(R, 256)
    assert R % rt == 0
    tiles = bucket_map.reshape(R // rt, rt * C)
    ranges = np.stack([np.where(tiles >= 0, tiles, REL_BUCKETS).min(axis=1), tiles.max(axis=1) + 1], axis=1)
    return pl.pallas_call(
        _bias_lookup_body,
        grid=(H, R // rt),
        in_specs=[pl.BlockSpec((rt, C), lambda h, i: (i, 0)), pl.BlockSpec(memory_space=pltpu.SMEM),
                  pl.BlockSpec(memory_space=pltpu.SMEM)],
        out_specs=pl.BlockSpec((1, rt, C), lambda h, i: (h, i, 0)),
        out_shape=jax.ShapeDtypeStruct((H, R, C), F32),
        compiler_params=_params("parallel", "parallel"),
        name=name,
    )(jnp.asarray(bucket_map), jnp.asarray(ranges.astype(np.int32)), table.astype(F32))


def _dilated_body(q_ref, kp_ref, kc_ref, vp_ref, vc_ref, bias_ref, o_ref, *, qblocks):
    blk, hd = DIL_BLOCK, HEAD_DIM
    gw = DIL_HEADS * hd
    n = pl.program_id(2)
    k = jnp.concatenate([kp_ref[0, 0], kc_ref[0, 0]], axis=0)
    v = jnp.concatenate([vp_ref[0, 0], vc_ref[0, 0]], axis=0)
    lane = lax.broadcasted_iota(jnp.int32, (1, gw), 1)
    qi = lax.broadcasted_iota(jnp.int32, (blk, 2 * blk), 0)
    kj = lax.broadcasted_iota(jnp.int32, (blk, 2 * blk), 1)
    rel = blk + qi - kj
    band = (rel >= 0) & (rel <= blk)
    first_key = jnp.where(n > 0, 0, blk)
    for i in range(qblocks):
        rows = slice(i * blk, (i + 1) * blk)
        q = q_ref[0, 0, rows, :]
        kw = k[i * blk:(i + 2) * blk]
        vw = v[i * blk:(i + 2) * blk]
        mask = band & (kj >= first_key) if i == 0 else band
        head_lanes = [(lane >= h * hd) & (lane < (h + 1) * hd) for h in range(DIL_HEADS)]
        q_all = jnp.concatenate([jnp.where(hl, q, jnp.zeros_like(q)) for hl in head_lanes], axis=0)
        s = _nt_dot(q_all, kw) * (hd ** -0.5) + bias_ref[...]
        s = jnp.where(jnp.concatenate([mask] * DIL_HEADS, axis=0), s, NEG_INF)
        m = jnp.max(s, axis=-1, keepdims=True)
        p = jnp.exp(s - m)
        l = jnp.sum(p, axis=-1, keepdims=True)
        pv = jnp.dot(p.astype(BF16), vw, preferred_element_type=F32) / l
        lse_all = m + jnp.log(l)
        out = jnp.zeros((blk, gw), F32)
        lse = jnp.zeros((blk, gw), F32)
        for h, hl in enumerate(head_lanes):
            out = jnp.where(hl, pv[h * blk:(h + 1) * blk], out)
            lse = jnp.where(hl, lse_all[h * blk:(h + 1) * blk], lse)
        o_ref[0, 0, rows, :gw] = out
        o_ref[0, 0, rows, gw:] = lse


def _dilated_group(qkv, bias, name):
    B, d, L, _ = qkv.shape
    blk, gw = DIL_BLOCK, DIL_HEADS * HEAD_DIM
    assert L % blk == 0
    qblocks = min(4, L // blk)
    assert L % (qblocks * blk) == 0
    rows = qblocks * blk

    def spec(j, prev):
        if prev:
            return pl.BlockSpec((1, 1, blk, gw), lambda b, r, n: (b, r, jnp.maximum(n * qblocks - 1, 0), j))
        return pl.BlockSpec((1, 1, rows, gw), lambda b, r, n: (b, r, n, j))

    return pl.pallas_call(
        functools.partial(_dilated_body, qblocks=qblocks),
        grid=(B, d, L // rows),
        in_specs=[spec(0, False), spec(1, True), spec(1, False), spec(2, True), spec(2, False),
                  pl.BlockSpec((DIL_HEADS * blk, 2 * blk), lambda b, r, n: (0, 0))],
        out_specs=pl.BlockSpec((1, 1, rows, 2 * gw), lambda b, r, n: (b, r, n, 0)),
        out_shape=jax.ShapeDtypeStruct((B, d, L, 2 * gw), F32),
        compiler_params=_params("parallel", "parallel", "arbitrary"),
        name=name,
    )(qkv, qkv, qkv, qkv, qkv, bias.reshape(DIL_HEADS * blk, 2 * blk))


def _moba_body(q_ref, k_ref, v_ref, bias_ref, o_ref, qa_ref, ka_ref, *, S):
    blk, hd = MOBA_BLOCK, HEAD_DIM
    nblk = S // blk
    scale = hd ** -0.5
    lane = lax.broadcasted_iota(jnp.int32, (1, 2 * hd), 1)
    kmean = jnp.mean(k_ref[0].astype(F32).reshape(nblk, blk, 2 * hd), axis=1)
    grp = MOBA_GROUP * blk
    row_blk = lax.broadcasted_iota(jnp.int32, (S, 2 * hd), 0) // blk
    q2, k2 = q_ref[0], k_ref[0]

    ka_ref[:, :2 * hd] = k2
    ka_ref[:, 2 * hd:] = jnp.where(lane == row_blk, 1.0, 0.0).astype(BF16)
    blk_t = lax.broadcasted_iota(jnp.int32, (nblk, S), 0)
    q_blk_t = lax.broadcasted_iota(jnp.int32, (nblk, S), 1) // blk
    past = blk_t < q_blk_t
    blk_f = blk_t.astype(F32)
    for h in range(2):
        head_lanes = (lane >= h * hd) & (lane < (h + 1) * hd)
        qh = jnp.where(head_lanes, q2, jnp.zeros_like(q2))
        gs = _nt_dot(jnp.where(head_lanes, kmean, 0.0), qh.astype(F32), precision=lax.Precision.HIGHEST)
        gs = jnp.where(past, gs, NEG_INF)
        sel = jnp.zeros(gs.shape, F32)
        for _r in range(MOBA_TOPK):
            gmax = jnp.max(gs, axis=0, keepdims=True)
            first = jnp.min(jnp.where(gs == gmax, blk_f, float(nblk)), axis=0, keepdims=True)
            pick = blk_f == first
            sel = jnp.where(pick & past, 1.0, sel)
            gs = jnp.where(pick, -jnp.inf, gs)
        penalty_t = jnp.where((sel > 0.5) | (blk_t == q_blk_t), 0.0, MOBA_MASK)
        penalty_t = jnp.concatenate([penalty_t, jnp.zeros((2 * hd - nblk, S), F32)], axis=0)
        qa_ref[h, :, :2 * hd] = qh * scale
        qa_ref[h, :, 2 * hd:] = penalty_t.T.astype(BF16)

    def q_block(qb, _):
        row0 = pl.multiple_of(qb * blk, blk)
        n_grp = qb // MOBA_GROUP + 1
        qa = jnp.concatenate([qa_ref[h, pl.ds(row0, blk), :] for h in range(2)], axis=0)

        def key_group(i, carry):
            m, l, acc = carry
            g = n_grp - 1 - i
            col0 = pl.multiple_of(g * grp, grp)
            per = MOBA_GROUP // MOBA_SPLIT
            half = per * blk
            cols = [pl.ds(pl.multiple_of(col0 + i * half, half), half) for i in range(MOBA_SPLIT)]
            scores = [_nt_dot(qa, ka_ref[c, :]) for c in cols]
            for i in range(MOBA_SPLIT):
                parts = []
                for jj in range(per):
                    n = g * MOBA_GROUP + i * per + jj
                    tile = jnp.clip(qb - n, 0, MOBA_BIAS_TILES - 1)
                    bias = jnp.concatenate([bias_ref[0, h, tile] for h in range(2)], axis=0)
                    parts.append(scores[i][:, jj * blk:(jj + 1) * blk] + bias)
                m_new = jnp.maximum(m, jnp.max(functools.reduce(jnp.maximum, parts), axis=-1, keepdims=True))
                a = jnp.exp(m - m_new)
                ps = [jnp.exp(p - m_new) for p in parts]
                l = l * a + jnp.sum(functools.reduce(jnp.add, ps), axis=-1, keepdims=True)
                p_all = jnp.concatenate([p.astype(BF16) for p in ps], axis=-1)
                pv = jnp.dot(p_all, v_ref[0, cols[i], :], preferred_element_type=F32)
                m, acc = m_new, acc * a + pv
            return m, l, acc

        init = (jnp.full((2 * blk, 1), -jnp.inf, F32), jnp.zeros((2 * blk, 1), F32), jnp.zeros((2 * blk, 2 * hd), F32))
        _, l, acc = lax.fori_loop(0, n_grp, key_group, init)
        o = acc / l
        o_ref[0, pl.ds(row0, blk), :] = jnp.where(lane < hd, o[:blk], o[blk:]).astype(o_ref.dtype)
        return 0

    lax.fori_loop(0, nblk, q_block, 0)


def _moba(qkv, bias, first_col):
    B, S, C = qkv.shape
    pw = 2 * HEAD_DIM
    npair = MOBA_HEADS // 2
    assert S % (MOBA_GROUP * MOBA_BLOCK) == 0 and S // MOBA_BLOCK <= HEAD_DIM and first_col % pw == 0
    c0 = first_col // pw

    def spec(j):
        return pl.BlockSpec((1, S, pw), lambda hp, b: (b, 0, c0 + j * npair + hp))

    return pl.pallas_call(
        functools.partial(_moba_body, S=S),
        grid=(npair, B),
        in_specs=[spec(0), spec(1), spec(2),
                  pl.BlockSpec((1, 2, MOBA_BIAS_TILES, MOBA_BLOCK, MOBA_BLOCK), lambda hp, b: (hp, 0, 0, 0, 0))],
        out_specs=pl.BlockSpec((1, S, pw), lambda hp, b: (b, 0, hp)),
        out_shape=jax.ShapeDtypeStruct((B, S, MOBA_HEADS * HEAD_DIM), BF16),
        scratch_shapes=[pltpu.VMEM((2, S, 2 * pw), BF16), pltpu.VMEM((S, 2 * pw), BF16)],
        compiler_params=_params("parallel", "parallel"),
        name="moba",
    )(qkv, qkv, qkv, bias)


def _merge_body(d0_ref, d1_ref, d2_ref, om_ref, gate_ref, x_ref, wa_ref, wb_ref, wo_ref, x1_ref, t1_ref, t2_ref):
    D = x_ref.shape[-1]
    gw = DIL_HEADS * HEAD_DIM
    n_slab = t1_ref.shape[0]
    for src, dst in ((d1_ref, t1_ref), (d2_ref, t2_ref)):
        dil, rows = src.shape[1], src.shape[2]
        for r in range(dil):
            for c in range(n_slab):
                dst[c, pl.ds(r, rows, stride=dil), :] = src[0, r, :, c * LANES:(c + 1) * LANES]
    g0 = d0_ref[...]
    g1 = jnp.concatenate([t1_ref[c] for c in range(n_slab)], axis=-1)
    g2 = jnp.concatenate([t2_ref[c] for c in range(n_slab)], axis=-1)
    l0, l1, l2 = g0[:, gw:], g1[:, gw:], g2[:, gw:]
    mx = jnp.maximum(jnp.maximum(l0, l1), l2)
    e0, e1, e2 = jnp.exp(l0 - mx), jnp.exp(l1 - mx), jnp.exp(l2 - mx)
    o_dil = (e0 * g0[:, :gw] + e1 * g1[:, :gw] + e2 * g2[:, :gw]) / (e0 + e1 + e2)
    a = jnp.dot(o_dil.astype(BF16), wa_ref[...], preferred_element_type=F32)
    b = jnp.dot(om_ref[...], wb_ref[...], preferred_element_type=F32)
    merged = gate_ref[:, :D].astype(F32) * a + gate_ref[:, D:].astype(F32) * b
    x1_ref[...] = x_ref[...] + jnp.dot(merged.astype(BF16), wo_ref[...], preferred_element_type=F32)


def _merge(dil_out, o_moba, gates, x2d, w_a, w_b, w_o, seq, tm):
    T, D = x2d.shape
    tm = min(tm, seq)
    per_seq = seq // tm
    assert seq % tm == 0 and dil_out[0].shape[1] == 1

    def rows(width):
        return pl.BlockSpec((tm, width), lambda i: (i, 0))

    def whole(w):
        return pl.BlockSpec(w.shape, lambda i: (0, 0))

    def residue_major(a):
        d, width = a.shape[1], a.shape[3]
        assert tm % (8 * d) == 0
        return pl.BlockSpec((1, d, tm // d, width), lambda i: (i // per_seq, 0, i % per_seq, 0))

    ws = [w_a.astype(BF16), w_b.astype(BF16), w_o.astype(BF16)]
    width = dil_out[0].shape[-1]
    return pl.pallas_call(
        _merge_body,
        grid=(T // tm,),
        in_specs=[rows(width), residue_major(dil_out[1]), residue_major(dil_out[2]),
                  rows(o_moba.shape[-1]), rows(2 * D), rows(D)] + [whole(w) for w in ws],
        out_specs=rows(D),
        out_shape=jax.ShapeDtypeStruct((T, D), F32),
        scratch_shapes=[pltpu.VMEM((width // LANES, tm, LANES), F32)] * 2,
        compiler_params=_params("parallel"),
        name="merge",
    )(dil_out[0].reshape(T, width), dil_out[1], dil_out[2], o_moba, gates, x2d, *ws)


def _cross_body(q_ref, kv_ref, x_ref, wo_ref, x2_ref):
    hd = CROSS_HEAD_DIM
    width = CROSS_HEADS * hd
    heads = []
    for h in range(CROSS_HEADS):
        sl = slice(h * hd, (h + 1) * hd)
        s = _nt_dot(q_ref[0, :, sl], kv_ref[0, :, sl]) * (hd ** -0.5)
        m = jnp.max(s, axis=-1, keepdims=True)
        p = jnp.exp(s - m)
        l = jnp.sum(p, axis=-1, keepdims=True)
        v = kv_ref[0, :, width + h * hd: width + (h + 1) * hd]
        heads.append((jnp.dot(p.astype(BF16), v, preferred_element_type=F32) / l).astype(BF16))
    o = jnp.concatenate(heads, axis=-1)
    x2_ref[0] = x_ref[0] + jnp.dot(o, wo_ref[...], preferred_element_type=F32)


def _cross(q, kv, x, w_o, tm):
    B, S, D = x.shape
    M = kv.shape[1]
    tm = min(tm, S)
    w_o = w_o.astype(BF16)
    return pl.pallas_call(
        _cross_body,
        grid=(B, S // tm),
        in_specs=[pl.BlockSpec((1, tm, q.shape[-1]), lambda b, i: (b, i, 0)),
                  pl.BlockSpec((1, M, kv.shape[-1]), lambda b, i: (b, 0, 0)),
                  pl.BlockSpec((1, tm, D), lambda b, i: (b, i, 0)),
                  pl.BlockSpec(w_o.shape, lambda b, i: (0, 0))],
        out_specs=pl.BlockSpec((1, tm, D), lambda b, i: (b, i, 0)),
        out_shape=jax.ShapeDtypeStruct((B, S, D), F32),
        compiler_params=_params("parallel", "parallel"),
        name="cross",
    )(q, kv, x, w_o)


def _merge_exchange_pairs(n):
    pairs = []
    p = 1
    while p < n:
        k = p
        while k >= 1:
            for j in range(k % p, n - k, 2 * k):
                for i in range(min(k, n - j - k)):
                    if (i + j) // (2 * p) == (i + j + k) // (2 * p):
                        pairs.append((i + j, i + j + k))
            k //= 2
        p *= 2
    return pairs


def _top16(s):
    n, sub = PEER_TOPK, SUBLANES
    assert s.shape[0] % sub == 0 and sub * n >= s.shape[0] >= 2 * sub
    v = [s[k * sub:(k + 1) * sub] for k in range(s.shape[0] // sub)] + [None] * (n - s.shape[0] // sub)

    def exchange(i, j):
        if v[j] is None:
            return
        if v[i] is None:
            v[i], v[j] = v[j], None
        else:
            v[i], v[j] = jnp.maximum(v[i], v[j]), jnp.minimum(v[i], v[j])

    for i, j in _merge_exchange_pairs(n):
        exchange(i, j)
    for shift in (sub // 2, sub // 4, sub // 8):
        w = [None if x is None else pltpu.roll(x, shift, 0) for x in v]
        for k in range(n):
            other = w[n - 1 - k]
            if other is not None:
                v[k] = other if v[k] is None else jnp.maximum(v[k], other)
        d = n // 2
        while d >= 1:
            for i in range(n):
                if i & d == 0:
                    exchange(i, i + d)
            d //= 2
    return v


def _top_sorted(s):
    v = _top16(s)
    nxt = jnp.max(jnp.where(s < v[-1][0:1], s, -jnp.inf), axis=0, keepdims=True)
    return jnp.concatenate([x[0:1] for x in v] + [nxt], axis=0)


def _pair_sum_candidates(a, b):
    assert PEER_TOPK == 16
    row = lax.broadcasted_iota(jnp.int32, (PEER_TOPK, 1), 0)
    blocks = [a[0:1] + b]
    blocks += [a[i:i + 1] + b[0:8] for i in (1, 2, 3)]
    blocks += [jnp.where(row >= 4, b[0:1] + a, -jnp.inf)]
    blocks += [jnp.where(row[0:8] >= 4, b[j:j + 1] + a[0:8], -jnp.inf) for j in (1, 2)]
    return jnp.concatenate(blocks, axis=0)


def _peer_route_body(q_ref, keys_ref, thr_ref, e1_ref, s2_ref, e2_ref):
    nk, K = PEER_N_KEYS, PEER_TOPK
    kd = keys_ref.shape[-1]
    for h in range(PEER_HEADS):
        sc = []
        for p in range(2):
            c0 = (2 * h + p) * kd
            sc.append(_nt_dot(keys_ref[p], q_ref[:, c0:c0 + kd], precision=lax.Precision.HIGHEST))
        a = _top_sorted(sc[0])
        b = _top_sorted(sc[1])
        cand = _pair_sum_candidates(a[:K], b[:K])
        tau = _top16(cand)[K - 1][0:1]
        below = jnp.max(jnp.where(cand < tau, cand, -jnp.inf), axis=0, keepdims=True)
        for corner in (a[0:1] + b[K:K + 1], a[K:K + 1] + b[0:1]):
            below = jnp.maximum(below, jnp.where(corner < tau, corner, -jnp.inf))
        below = jnp.where(below == -jnp.inf, tau, below)
        cut = 0.5 * (tau + below)
        top = a[0:1] + b[0:1]
        z = jnp.sum(jnp.where(cand >= tau, jnp.exp(cand - top), 0.0), axis=0, keepdims=True)
        thr = cut - sc[0]
        e1 = jnp.exp(sc[0] - a[0:1]) * (0.5 / z)
        for c in range(thr_ref.shape[1]):
            thr_ref[h, c] = thr[:, c * LANES:(c + 1) * LANES]
            e1_ref[h, c] = e1[:, c * LANES:(c + 1) * LANES]
        s2_ref[h] = sc[1]
        e2_ref[h] = jnp.exp(sc[1] - b[0:1])


def _peer_route(q, sub_keys, tt):
    T = q.shape[0]
    tt = min(tt, T)
    nk = PEER_N_KEYS
    assert tt % LANES == 0
    big = jax.ShapeDtypeStruct((PEER_HEADS, nk, T), F32)
    big_spec = pl.BlockSpec((PEER_HEADS, nk, tt), lambda i: (0, 0, i))
    slab = jax.ShapeDtypeStruct((PEER_HEADS, T // LANES, nk, LANES), F32)
    slab_spec = pl.BlockSpec((PEER_HEADS, tt // LANES, nk, LANES), lambda i: (0, i, 0, 0))
    return pl.pallas_call(
        _peer_route_body,
        grid=(T // tt,),
        in_specs=[pl.BlockSpec((tt, q.shape[1]), lambda i: (i, 0)),
                  pl.BlockSpec(sub_keys.shape, lambda i: (0, 0, 0))],
        out_specs=[slab_spec, slab_spec, big_spec, big_spec],
        out_shape=[slab, slab, big, big],
        compiler_params=_params("parallel"),
        name="peer_route",
    )(q, sub_keys.astype(F32))


def _peer_main_body(h_ref, u_ref, vt_ref, thr_ref, e1_ref, s2_ref, e2_ref, x_ref, g_ref,
                    o_ref, acc_ref, gate_ref, ga_ref, ht_ref, *, rows_per_step):
    nk = PEER_N_KEYS
    lanes = LANES
    j = pl.program_id(1)
    tt = h_ref.shape[0]

    @pl.when(j == 0)
    def _():
        acc_ref[...] = jnp.zeros_like(acc_ref)
        ht_ref[...] = h_ref[...].astype(F32).T.astype(ht_ref.dtype)

    i1_base = pl.multiple_of(j * rows_per_step, rows_per_step)
    group = 4
    sub = 64
    n_sub = nk // sub

    def gate_tile(ti, carry):
        cs = pl.ds(pl.multiple_of((ti // n_sub) * lanes, lanes), lanes)
        k0 = pl.multiple_of((ti % n_sub) * sub, sub)
        for r0 in range(0, rows_per_step, group):
            gates = [None] * group
            for h in range(PEER_HEADS):
                s2 = s2_ref[h, pl.ds(k0, sub), cs]
                e2 = e2_ref[h, pl.ds(k0, sub), cs]
                for g in range(group):
                    r = r0 + g
                    row = pl.ds(i1_base + r, SUBLANES, stride=0)
                    thr = thr_ref[h, ti // n_sub, row, :]
                    e1 = e1_ref[h, ti // n_sub, row, :]
                    tiles = (sub // SUBLANES, SUBLANES, lanes)
                    w = (jnp.where(s2.reshape(tiles) >= thr, e2.reshape(tiles), 0.0) * e1).reshape(sub, lanes)
                    gates[g] = w if gates[g] is None else gates[g] + w
            for g in range(group):
                rs = pl.ds(pl.multiple_of((r0 + g) * nk + k0, sub), sub)
                gate_ref[rs, cs] = gates[g]
        return carry

    lax.fori_loop(0, (tt // lanes) * n_sub, gate_tile, 0)
    pre = jnp.dot(u_ref[...], ht_ref[...], preferred_element_type=F32)
    ga_ref[...] = (gate_ref[...] * (pre * (1.0 + lax.erf(pre * (2.0 ** -0.5))))).astype(ga_ref.dtype)
    acc_ref[...] += jnp.dot(vt_ref[0], ga_ref[...], preferred_element_type=F32)

    @pl.when(j == pl.num_programs(1) - 1)
    def _():
        o_ref[...] = _rms(x_ref[...] + acc_ref[...].T, g_ref[...])


def _peer_main(hn, u, v, thr, e1, s2, e2, x2d, g_final, tt, rows_per_step):
    T, D = x2d.shape
    tt = min(tt, T)
    nk = PEER_N_KEYS
    eb = rows_per_step * nk
    n_exp = u.shape[0]
    assert n_exp == nk * nk and nk % rows_per_step == 0
    u = u.astype(BF16)
    vt = v.astype(BF16).reshape(n_exp // eb, eb, D).transpose(0, 2, 1)
    score_spec = pl.BlockSpec((PEER_HEADS, nk, tt), lambda i, j: (0, 0, i))
    slab_spec = pl.BlockSpec((PEER_HEADS, tt // LANES, nk, LANES), lambda i, j: (0, i, 0, 0))
    return pl.pallas_call(
        functools.partial(_peer_main_body, rows_per_step=rows_per_step),
        grid=(T // tt, n_exp // eb),
        in_specs=[pl.BlockSpec((tt, D), lambda i, j: (i, 0)),
                  pl.BlockSpec((eb, D), lambda i, j: (j, 0)),
                  pl.BlockSpec((1, D, eb), lambda i, j: (j, 0, 0)),
                  slab_spec, slab_spec, score_spec, score_spec,
                  pl.BlockSpec((tt, D), lambda i, j: (i, 0)),
                  pl.BlockSpec((1, D), lambda i, j: (0, 0))],
        out_specs=pl.BlockSpec((tt, D), lambda i, j: (i, 0)),
        out_shape=jax.ShapeDtypeStruct((T, D), F32),
        scratch_shapes=[pltpu.VMEM((D, tt), F32), pltpu.VMEM((eb, tt), F32), pltpu.VMEM((eb, tt), BF16),
                        pltpu.VMEM((D, tt), BF16)],
        compiler_params=_params("parallel", "arbitrary"),
        name="peer_main",
    )(hn, u, vt, thr, e1, s2, e2, x2d, g_final.reshape(1, D).astype(F32))


def kernel(x, mem, rel_bias, g_mix, w_in, b_gate, w_branch_a, w_branch_b, w_out, g_cross, g_mem, w_q_cross,
           w_kv_cross, w_o_cross, g_ffn, w_q_peer, peer_sub_keys, peer_u, peer_v, g_final):
    B, S, D = x.shape
    T = B * S
    depth = w_in.shape[0]
    ndil = len(DIL_GROUPS) * DIL_HEADS
    x2d = x.reshape(T, D)

    dil_bias = [_bias_lookup(_dil_bucket_map(d), rel_bias[:, gi * DIL_HEADS:(gi + 1) * DIL_HEADS], f"dil_bias_g{gi}")
                for gi, (_, d) in enumerate(DIL_GROUPS)]
    mmap = _moba_bucket_map()
    moba_bias = _bias_lookup(mmap.reshape(-1, MOBA_BLOCK), rel_bias[:, ndil:], "moba_bias")
    moba_bias = moba_bias.reshape(MOBA_HEADS // 2, 2, MOBA_BIAS_TILES, MOBA_BLOCK, MOBA_BLOCK)

    assert depth == 1, "the final norm is fused into the last PEER sweep; one layer supported"
    l = 0
    gc = DIL_GROUP_COLS
    assert DIL_GROUPS[0][1] == 1
    w_tok = jnp.concatenate([w_in[l][:, :gc], w_in[l][:, DIL_WIDTH:QKV_WIDTH]], axis=1)
    tok = _norm_matmul(x2d, g_mix[l], w_tok, out_dtype=BF16, tm=1024, tn=w_tok.shape[1], name="proj_tok")
    dil_qkv = [tok.reshape(B, 1, S, -1)]
    for gi, (_, d) in enumerate(DIL_GROUPS[1:], start=1):
        dil_qkv.append(_norm_matmul(x2d, g_mix[l], w_in[l][:, gi * gc:(gi + 1) * gc], out_dtype=BF16, tm=1024, tn=gc,
                                    dil=d, seq=S, name=f"proj_dil_g{gi}"))
    gates = _norm_matmul(x2d, g_mix[l], w_in[l][:, QKV_WIDTH:], out_dtype=BF16, tm=1024, tn=2 * D,
                         bias=b_gate[l], name="proj_gates")
    dil_out = [_dilated_group(a, dil_bias[gi], f"dilated_g{gi}") for gi, a in enumerate(dil_qkv)]
    o_moba = _moba(tok.reshape(B, S, -1), moba_bias, first_col=gc).reshape(T, MOBA_HEADS * HEAD_DIM)
    x2d = _merge(dil_out, o_moba, gates, x2d, w_branch_a[l], w_branch_b[l], w_out[l], seq=S, tm=512)
    M = mem.shape[1]
    q_c = _norm_matmul(x2d, g_cross[l], w_q_cross[l], out_dtype=BF16, tm=1024, tn=512, name="cross_q")
    kv = _norm_matmul(mem.reshape(B * M, D), g_mem[l], w_kv_cross[l], out_dtype=BF16, tm=1024, tn=1024, name="cross_kv")
    x2d = _cross(q_c.reshape(B, S, -1), kv.reshape(B, M, -1), x2d.reshape(B, S, D), w_o_cross[l], tm=1024).reshape(T, D)
    q_p, hn = _norm_matmul(x2d, g_ffn[l], w_q_peer[l], out_dtype=F32, tm=512, tn=w_q_peer.shape[-1], emit_normed=True,
                           name="peer_q")
    thr, e1, s2, e2 = _peer_route(q_p, peer_sub_keys[l], tt=512)
    y = _peer_main(hn, peer_u[l], peer_v[l], thr, e1, s2, e2, x2d, g_final,
                   tt=512, rows_per_step=8)
    return y.reshape(B, S, D)
```

```python
import functools
import math

import jax
import jax.numpy as jnp
import numpy as np
from jax import lax
from jax.experimental import pallas as pl
from jax.experimental.pallas import tpu as pltpu

HEAD_DIM = 64
DIL_GROUPS = ((128, 1), (512, 4), (2048, 16))
DIL_HEADS = 4
DIL_BLOCK = 128
MOBA_HEADS = 8
MOBA_BLOCK = 256
MOBA_TOPK = 3
MOBA_BIAS_TILES = 8
MOBA_GROUP = 4
MOBA_SPLIT = 2
MOBA_MASK = -(2.0 ** 100)
REL_BUCKETS = 32
REL_MAX_DIST = 2048
CROSS_HEADS = 4
CROSS_HEAD_DIM = 128
PEER_HEADS = 8
PEER_N_KEYS = 128
PEER_TOPK = 16
RMS_EPS = 1e-6
NEG_INF = -1e30
DIL_GROUP_COLS = 3 * DIL_HEADS * HEAD_DIM
DIL_WIDTH = len(DIL_GROUPS) * DIL_GROUP_COLS
MOBA_WIDTH = 3 * MOBA_HEADS * HEAD_DIM
QKV_WIDTH = DIL_WIDTH + MOBA_WIDTH

LANES = 128
SUBLANES = 8
V7X_VMEM_BYTES = 64 * 1024 * 1024
VMEM_LIMIT = V7X_VMEM_BYTES * 3 // 4

BF16 = jnp.bfloat16
F32 = jnp.float32


def _params(*sem):
    return pltpu.CompilerParams(dimension_semantics=sem, vmem_limit_bytes=VMEM_LIMIT)


def _rms(x, g):
    return x * lax.rsqrt(jnp.mean(x * x, axis=-1, keepdims=True) + RMS_EPS) * g


def _nt_dot(a, b, precision=None):
    return lax.dot_general(a, b, (((1,), (1,)), ((), ())), preferred_element_type=F32, precision=precision)


def _norm_matmul_body(x_ref, g_ref, w_ref, *rest, sigmoid_bias, dil, emit_normed):
    rest = list(rest)
    b_ref = rest.pop(0) if sigmoid_bias else None
    o_ref = rest.pop(0)
    hn_ref = rest.pop(0) if emit_normed else None
    rows = x_ref.shape[0] // dil

    if dil == 1:
        h = _rms(x_ref[...], g_ref[...]).astype(BF16)
    else:
        cols_ref = rest.pop(0)
        n_slab = cols_ref.shape[0]
        for c in range(n_slab):
            cols_ref[c] = x_ref[:, c * LANES:(c + 1) * LANES]
        h = jnp.concatenate(
            [_rms(jnp.concatenate([cols_ref[c, pl.ds(r, rows, stride=dil), :] for c in range(n_slab)], axis=-1),
                  g_ref[...]).astype(BF16) for r in range(dil)], axis=0)
    if emit_normed:
        hn_ref[...] = h

    acc = jnp.dot(h, w_ref[...], preferred_element_type=F32)
    if sigmoid_bias:
        acc = jax.nn.sigmoid(acc + b_ref[...])
    if dil == 1:
        o_ref[...] = acc.astype(o_ref.dtype)
    else:
        for r in range(dil):
            o_ref[0, r] = acc[r * rows:(r + 1) * rows].astype(o_ref.dtype)


def _norm_matmul(x2d, g, w, *, out_dtype, tm, tn, bias=None, dil=1, seq=None, emit_normed=False, name):
    T, D = x2d.shape
    N = w.shape[1]
    tm = min(tm, T if seq is None else seq)
    assert T % tm == 0 and N % tn == 0
    in_specs = [
        pl.BlockSpec((tm, D), lambda i, j: (i, 0)),
        pl.BlockSpec((1, D), lambda i, j: (0, 0)),
        pl.BlockSpec((D, tn), lambda i, j: (0, j)),
    ]
    args = [x2d, g.reshape(1, D).astype(F32), w.astype(BF16)]
    if bias is not None:
        in_specs.append(pl.BlockSpec((1, tn), lambda i, j: (0, j)))
        args.append(bias.reshape(1, N).astype(F32))
    assert not (emit_normed and dil > 1)
    scratch = []
    if dil == 1:
        out_spec = pl.BlockSpec((tm, tn), lambda i, j: (i, j))
        out_shape = jax.ShapeDtypeStruct((T, N), out_dtype)
    else:
        assert seq % tm == 0 and tm % (2 * SUBLANES * dil) == 0 and D % LANES == 0
        per_seq = seq // tm
        out_spec = pl.BlockSpec((1, dil, tm // dil, tn), lambda i, j: (i // per_seq, 0, i % per_seq, j))
        out_shape = jax.ShapeDtypeStruct((T // seq, dil, seq // dil, N), out_dtype)
        scratch.append(pltpu.VMEM((D // LANES, tm, LANES), F32))
    if emit_normed:
        out_spec = [out_spec, pl.BlockSpec((tm, D), lambda i, j: (i, 0))]
        out_shape = [out_shape, jax.ShapeDtypeStruct((T, D), BF16)]
    return pl.pallas_call(
        functools.partial(_norm_matmul_body, sigmoid_bias=bias is not None, dil=dil, emit_normed=emit_normed),
        grid=(T // tm, N // tn),
        in_specs=in_specs,
        out_specs=out_spec,
        out_shape=out_shape,
        scratch_shapes=scratch,
        compiler_params=_params("parallel", "arbitrary"),
        name=name,
    )(*args)


def _t5_bucket_np(n):
    max_exact = REL_BUCKETS // 2
    nf = np.maximum(n, 1).astype(np.float32)
    large = max_exact + (np.log(nf / np.float32(max_exact)) / np.float32(math.log(REL_MAX_DIST / max_exact))
                         * np.float32(REL_BUCKETS - max_exact)).astype(np.int32)
    large = np.minimum(large, REL_BUCKETS - 1)
    return np.where(n < max_exact, n, large).astype(np.int32)


def _dil_bucket_map(d):
    blk = DIL_BLOCK
    rel = (blk + np.arange(blk))[:, None] - np.arange(2 * blk)[None, :]
    return _t5_bucket_np(np.clip(rel, 0, None) * d)


def _moba_bucket_map():
    blk = MOBA_BLOCK
    rel = np.arange(blk)[:, None] - np.arange(blk)[None, :]
    tiles = [_t5_bucket_np(np.clip(rel + dlt * blk, 0, None)) for dlt in range(MOBA_BIAS_TILES)]
    tiles[0] = np.where(rel >= 0, tiles[0], -1)
    far = _t5_bucket_np(np.arange((MOBA_BIAS_TILES - 2) * blk + 1, 64 * blk))
    assert (far == REL_BUCKETS - 1).all()
    return np.stack(tiles)


def _bias_lookup_body(idx_ref, range_ref, tab_ref, o_ref):
    h, i = pl.program_id(0), pl.program_id(1)
    idx = idx_ref[...]

    def bucket(b, acc):
        return jnp.where(idx == b, tab_ref[b, h], acc)

    o_ref[0] = lax.fori_loop(range_ref[i, 0], range_ref[i, 1], bucket, jnp.full(idx.shape, NEG_INF, F32))


def _bias_lookup(bucket_map, table, name):
    R, C = bucket_map.shape
    H = table.shape[1]
    rt = min(R, 256)
    assert R % rt == 0
    tiles = bucket_map.reshape(R // rt, rt * C)
    ranges = np.stack([np.where(tiles >= 0, tiles, REL_BUCKETS).min(axis=1), tiles.max(axis=1) + 1], axis=1)
    return pl.pallas_call(
        _bias_lookup_body,
        grid=(H, R // rt),
        in_specs=[pl.BlockSpec((rt, C), lambda h, i: (i, 0)), pl.BlockSpec(memory_space=pltpu.SMEM),
                  pl.BlockSpec(memory_space=pltpu.SMEM)],
        out_specs=pl.BlockSpec((1, rt, C), lambda h, i: (h, i, 0)),
        out_shape=jax.ShapeDtypeStruct((H, R, C), F32),
        compiler_params=_params("parallel", "parallel"),
        name=name,
    )(jnp.asarray(bucket_map), jnp.asarray(ranges.astype(np.int32)), table.astype(F32))


def _dilated_body(q_ref, kp_ref, kc_ref, vp_ref, vc_ref, bias_ref, o_ref, *, qblocks):
    blk, hd = DIL_BLOCK, HEAD_DIM
    gw = DIL_HEADS * hd
    n = pl.program_id(2)
    k = jnp.concatenate([kp_ref[0, 0], kc_ref[0, 0]], axis=0)
    v = jnp.concatenate([vp_ref[0, 0], vc_ref[0, 0]], axis=0)
    lane = lax.broadcasted_iota(jnp.int32, (1, gw), 1)
    qi = lax.broadcasted_iota(jnp.int32, (blk, 2 * blk), 0)
    kj = lax.broadcasted_iota(jnp.int32, (blk, 2 * blk), 1)
    rel = blk + qi - kj
    band = (rel >= 0) & (rel <= blk)
    first_key = jnp.where(n > 0, 0, blk)
    for i in range(qblocks):
        rows = slice(i * blk, (i + 1) * blk)
        q = q_ref[0, 0, rows, :]
        kw = k[i * blk:(i + 2) * blk]
        vw = v[i * blk:(i + 2) * blk]
        mask = band & (kj >= first_key) if i == 0 else band
        head_lanes = [(lane >= h * hd) & (lane < (h + 1) * hd) for h in range(DIL_HEADS)]
        q_all = jnp.concatenate([jnp.where(hl, q, jnp.zeros_like(q)) for hl in head_lanes], axis=0)
        s = _nt_dot(q_all, kw) * (hd ** -0.5) + bias_ref[...]
        s = jnp.where(jnp.concatenate([mask] * DIL_HEADS, axis=0), s, NEG_INF)
        m = jnp.max(s, axis=-1, keepdims=True)
        p = jnp.exp(s - m)
        l = jnp.sum(p, axis=-1, keepdims=True)
        pv = jnp.dot(p.astype(BF16), vw, preferred_element_type=F32) / l
        lse_all = m + jnp.log(l)
        out = jnp.zeros((blk, gw), F32)
        lse = jnp.zeros((blk, gw), F32)
        for h, hl in enumerate(head_lanes):
            out = jnp.where(hl, pv[h * blk:(h + 1) * blk], out)
            lse = jnp.where(hl, lse_all[h * blk:(h + 1) * blk], lse)
        o_ref[0, 0, rows, :gw] = out
        o_ref[0, 0, rows, gw:] = lse


def _dilated_group(qkv, bias, name):
    B, d, L, _ = qkv.shape
    blk, gw = DIL_BLOCK, DIL_HEADS * HEAD_DIM
    assert L % blk == 0
    qblocks = min(4, L // blk)
    assert L % (qblocks * blk) == 0
    rows = qblocks * blk

    def spec(j, prev):
        if prev:
            return pl.BlockSpec((1, 1, blk, gw), lambda b, r, n: (b, r, jnp.maximum(n * qblocks - 1, 0), j))
        return pl.BlockSpec((1, 1, rows, gw), lambda b, r, n: (b, r, n, j))

    return pl.pallas_call(
        functools.partial(_dilated_body, qblocks=qblocks),
        grid=(B, d, L // rows),
        in_specs=[spec(0, False), spec(1, True), spec(1, False), spec(2, True), spec(2, False),
                  pl.BlockSpec((DIL_HEADS * blk, 2 * blk), lambda b, r, n: (0, 0))],
        out_specs=pl.BlockSpec((1, 1, rows, 2 * gw), lambda b, r, n: (b, r, n, 0)),
        out_shape=jax.ShapeDtypeStruct((B, d, L, 2 * gw), F32),
        compiler_params=_params("parallel", "parallel", "arbitrary"),
        name=name,
    )(qkv, qkv, qkv, qkv, qkv, bias.reshape(DIL_HEADS * blk, 2 * blk))


def _moba_body(q_ref, k_ref, v_ref, bias_ref, o_ref, qa_ref, ka_ref, *, S):
    blk, hd = MOBA_BLOCK, HEAD_DIM
    nblk = S // blk
    scale = hd ** -0.5
    lane = lax.broadcasted_iota(jnp.int32, (1, 2 * hd), 1)
    kmean = jnp.mean(k_ref[0].astype(F32).reshape(nblk, blk, 2 * hd), axis=1)
    grp = MOBA_GROUP * blk
    row_blk = lax.broadcasted_iota(jnp.int32, (S, 2 * hd), 0) // blk
    q2, k2 = q_ref[0], k_ref[0]

    ka_ref[:, :2 * hd] = k2
    ka_ref[:, 2 * hd:] = jnp.where(lane == row_blk, 1.0, 0.0).astype(BF16)
    blk_t = lax.broadcasted_iota(jnp.int32, (nblk, S), 0)
    q_blk_t = lax.broadcasted_iota(jnp.int32, (nblk, S), 1) // blk
    past = blk_t < q_blk_t
    blk_f = blk_t.astype(F32)
    for h in range(2):
        head_lanes = (lane >= h * hd) & (lane < (h + 1) * hd)
        qh = jnp.where(head_lanes, q2, jnp.zeros_like(q2))
        gs = _nt_dot(jnp.where(head_lanes, kmean, 0.0), qh.astype(F32), precision=lax.Precision.HIGHEST)
        gs = jnp.where(past, gs, NEG_INF)
        sel = jnp.zeros(gs.shape, F32)
        for _r in range(MOBA_TOPK):
            gmax = jnp.max(gs, axis=0, keepdims=True)
            first = jnp.min(jnp.where(gs == gmax, blk_f, float(nblk)), axis=0, keepdims=True)
            pick = blk_f == first
            sel = jnp.where(pick & past, 1.0, sel)
            gs = jnp.where(pick, -jnp.inf, gs)
        penalty_t = jnp.where((sel > 0.5) | (blk_t == q_blk_t), 0.0, MOBA_MASK)
        penalty_t = jnp.concatenate([penalty_t, jnp.zeros((2 * hd - nblk, S), F32)], axis=0)
        qa_ref[h, :, :2 * hd] = qh * scale
        qa_ref[h, :, 2 * hd:] = penalty_t.T.astype(BF16)

    def q_block(qb, _):
        row0 = pl.multiple_of(qb * blk, blk)
        n_grp = qb // MOBA_GROUP + 1
        qa = jnp.concatenate([qa_ref[h, pl.ds(row0, blk), :] for h in range(2)], axis=0)

        def key_group(i, carry):
            m, l, acc = carry
            g = n_grp - 1 - i
            col0 = pl.multiple_of(g * grp, grp)
            per = MOBA_GROUP // MOBA_SPLIT
            half = per * blk
            cols = [pl.ds(pl.multiple_of(col0 + i * half, half), half) for i in range(MOBA_SPLIT)]
            scores = [_nt_dot(qa, ka_ref[c, :]) for c in cols]
            for i in range(MOBA_SPLIT):
                parts = []
                for jj in range(per):
                    n = g * MOBA_GROUP + i * per + jj
                    tile = jnp.clip(qb - n, 0, MOBA_BIAS_TILES - 1)
                    bias = jnp.concatenate([bias_ref[0, h, tile] for h in range(2)], axis=0)
                    parts.append(scores[i][:, jj * blk:(jj + 1) * blk] + bias)
                m_new = jnp.maximum(m, jnp.max(functools.reduce(jnp.maximum, parts), axis=-1, keepdims=True))
                a = jnp.exp(m - m_new)
                ps = [jnp.exp(p - m_new) for p in parts]
                l = l * a + jnp.sum(functools.reduce(jnp.add, ps), axis=-1, keepdims=True)
                p_all = jnp.concatenate([p.astype(BF16) for p in ps], axis=-1)
                pv = jnp.dot(p_all, v_ref[0, cols[i], :], preferred_element_type=F32)
                m, acc = m_new, acc * a + pv
            return m, l, acc

        init = (jnp.full((2 * blk, 1), -jnp.inf, F32), jnp.zeros((2 * blk, 1), F32), jnp.zeros((2 * blk, 2 * hd), F32))
        _, l, acc = lax.fori_loop(0, n_grp, key_group, init)
        o = acc / l
        o_ref[0, pl.ds(row0, blk), :] = jnp.where(lane < hd, o[:blk], o[blk:]).astype(o_ref.dtype)
        return 0

    lax.fori_loop(0, nblk, q_block, 0)


def _moba(qkv, bias, first_col):
    B, S, C = qkv.shape
    pw = 2 * HEAD_DIM
    npair = MOBA_HEADS // 2
    assert S % (MOBA_GROUP * MOBA_BLOCK) == 0 and S // MOBA_BLOCK <= HEAD_DIM and first_col % pw == 0
    c0 = first_col // pw

    def spec(j):
        return pl.BlockSpec((1, S, pw), lambda hp, b: (b, 0, c0 + j * npair + hp))

    return pl.pallas_call(
        functools.partial(_moba_body, S=S),
        grid=(npair, B),
        in_specs=[spec(0), spec(1), spec(2),
                  pl.BlockSpec((1, 2, MOBA_BIAS_TILES, MOBA_BLOCK, MOBA_BLOCK), lambda hp, b: (hp, 0, 0, 0, 0))],
        out_specs=pl.BlockSpec((1, S, pw), lambda hp, b: (b, 0, hp)),
        out_shape=jax.ShapeDtypeStruct((B, S, MOBA_HEADS * HEAD_DIM), BF16),
        scratch_shapes=[pltpu.VMEM((2, S, 2 * pw), BF16), pltpu.VMEM((S, 2 * pw), BF16)],
        compiler_params=_params("parallel", "parallel"),
        name="moba",
    )(qkv, qkv, qkv, bias)


def _merge_body(d0_ref, d1_ref, d2_ref, om_ref, gate_ref, x_ref, wa_ref, wb_ref, wo_ref, x1_ref, t1_ref, t2_ref):
    D = x_ref.shape[-1]
    gw = DIL_HEADS * HEAD_DIM
    n_slab = t1_ref.shape[0]
    for src, dst in ((d1_ref, t1_ref), (d2_ref, t2_ref)):
        dil, rows = src.shape[1], src.shape[2]
        for r in range(dil):
            for c in range(n_slab):
                dst[c, pl.ds(r, rows, stride=dil), :] = src[0, r, :, c * LANES:(c + 1) * LANES]
    g0 = d0_ref[...]
    g1 = jnp.concatenate([t1_ref[c] for c in range(n_slab)], axis=-1)
    g2 = jnp.concatenate([t2_ref[c] for c in range(n_slab)], axis=-1)
    l0, l1, l2 = g0[:, gw:], g1[:, gw:], g2[:, gw:]
    mx = jnp.maximum(jnp.maximum(l0, l1), l2)
    e0, e1, e2 = jnp.exp(l0 - mx), jnp.exp(l1 - mx), jnp.exp(l2 - mx)
    o_dil = (e0 * g0[:, :gw] + e1 * g1[:, :gw] + e2 * g2[:, :gw]) / (e0 + e1 + e2)
    a = jnp.dot(o_dil.astype(BF16), wa_ref[...], preferred_element_type=F32)
    b = jnp.dot(om_ref[...], wb_ref[...], preferred_element_type=F32)
    merged = gate_ref[:, :D].astype(F32) * a + gate_ref[:, D:].astype(F32) * b
    x1_ref[...] = x_ref[...] + jnp.dot(merged.astype(BF16), wo_ref[...], preferred_element_type=F32)


def _merge(dil_out, o_moba, gates, x2d, w_a, w_b, w_o, seq, tm):
    T, D = x2d.shape
    tm = min(tm, seq)
    per_seq = seq // tm
    assert seq % tm == 0 and dil_out[0].shape[1] == 1

    def rows(width):
        return pl.BlockSpec((tm, width), lambda i: (i, 0))

    def whole(w):
        return pl.BlockSpec(w.shape, lambda i: (0, 0))

    def residue_major(a):
        d, width = a.shape[1], a.shape[3]
        assert tm % (8 * d) == 0
        return pl.BlockSpec((1, d, tm // d, width), lambda i: (i // per_seq, 0, i % per_seq, 0))

    ws = [w_a.astype(BF16), w_b.astype(BF16), w_o.astype(BF16)]
    width = dil_out[0].shape[-1]
    return pl.pallas_call(
        _merge_body,
        grid=(T // tm,),
        in_specs=[rows(width), residue_major(dil_out[1]), residue_major(dil_out[2]),
                  rows(o_moba.shape[-1]), rows(2 * D), rows(D)] + [whole(w) for w in ws],
        out_specs=rows(D),
        out_shape=jax.ShapeDtypeStruct((T, D), F32),
        scratch_shapes=[pltpu.VMEM((width // LANES, tm, LANES), F32)] * 2,
        compiler_params=_params("parallel"),
        name="merge",
    )(dil_out[0].reshape(T, width), dil_out[1], dil_out[2], o_moba, gates, x2d, *ws)


def _cross_body(q_ref, kv_ref, x_ref, wo_ref, x2_ref):
    hd = CROSS_HEAD_DIM
    width = CROSS_HEADS * hd
    heads = []
    for h in range(CROSS_HEADS):
        sl = slice(h * hd, (h + 1) * hd)
        s = _nt_dot(q_ref[0, :, sl], kv_ref[0, :, sl]) * (hd ** -0.5)
        m = jnp.max(s, axis=-1, keepdims=True)
        p = jnp.exp(s - m)
        l = jnp.sum(p, axis=-1, keepdims=True)
        v = kv_ref[0, :, width + h * hd: width + (h + 1) * hd]
        heads.append((jnp.dot(p.astype(BF16), v, preferred_element_type=F32) / l).astype(BF16))
    o = jnp.concatenate(heads, axis=-1)
    x2_ref[0] = x_ref[0] + jnp.dot(o, wo_ref[...], preferred_element_type=F32)


def _cross(q, kv, x, w_o, tm):
    B, S, D = x.shape
    M = kv.shape[1]
    tm = min(tm, S)
    w_o = w_o.astype(BF16)
    return pl.pallas_call(
        _cross_body,
        grid=(B, S // tm),
        in_specs=[pl.BlockSpec((1, tm, q.shape[-1]), lambda b, i: (b, i, 0)),
                  pl.BlockSpec((1, M, kv.shape[-1]), lambda b, i: (b, 0, 0)),
                  pl.BlockSpec((1, tm, D), lambda b, i: (b, i, 0)),
                  pl.BlockSpec(w_o.shape, lambda b, i: (0, 0))],
        out_specs=pl.BlockSpec((1, tm, D), lambda b, i: (b, i, 0)),
        out_shape=jax.ShapeDtypeStruct((B, S, D), F32),
        compiler_params=_params("parallel", "parallel"),
        name="cross",
    )(q, kv, x, w_o)


def _merge_exchange_pairs(n):
    pairs = []
    p = 1
    while p < n:
        k = p
        while k >= 1:
            for j in range(k % p, n - k, 2 * k):
                for i in range(min(k, n - j - k)):
                    if (i + j) // (2 * p) == (i + j + k) // (2 * p):
                        pairs.append((i + j, i + j + k))
            k //= 2
        p *= 2
    return pairs


def _top16(s):
    n, sub = PEER_TOPK, SUBLANES
    assert s.shape[0] % sub == 0 and sub * n >= s.shape[0] >= 2 * sub
    v = [s[k * sub:(k + 1) * sub] for k in range(s.shape[0] // sub)] + [None] * (n - s.shape[0] // sub)

    def exchange(i, j):
        if v[j] is None:
            return
        if v[i] is None:
            v[i], v[j] = v[j], None
        else:
            v[i], v[j] = jnp.maximum(v[i], v[j]), jnp.minimum(v[i], v[j])

    for i, j in _merge_exchange_pairs(n):
        exchange(i, j)
    for shift in (sub // 2, sub // 4, sub // 8):
        w = [None if x is None else pltpu.roll(x, shift, 0) for x in v]
        for k in range(n):
            other = w[n - 1 - k]
            if other is not None:
                v[k] = other if v[k] is None else jnp.maximum(v[k], other)
        d = n // 2
        while d >= 1:
            for i in range(n):
                if i & d == 0:
                    exchange(i, i + d)
            d //= 2
    return v


def _top_sorted(s):
    v = _top16(s)
    nxt = jnp.max(jnp.where(s < v[-1][0:1], s, -jnp.inf), axis=0, keepdims=True)
    return jnp.concatenate([x[0:1] for x in v] + [nxt], axis=0)


def _pair_sum_candidates(a, b):
    assert PEER_TOPK == 16
    row = lax.broadcasted_iota(jnp.int32, (PEER_TOPK, 1), 0)
    blocks = [a[0:1] + b]
    blocks += [a[i:i + 1] + b[0:8] for i in (1, 2, 3)]
    blocks += [jnp.where(row >= 4, b[0:1] + a, -jnp.inf)]
    blocks += [jnp.where(row[0:8] >= 4, b[j:j + 1] + a[0:8], -jnp.inf) for j in (1, 2)]
    return jnp.concatenate(blocks, axis=0)


def _peer_route_body(q_ref, keys_ref, thr_ref, e1_ref, s2_ref, e2_ref):
    nk, K = PEER_N_KEYS, PEER_TOPK
    kd = keys_ref.shape[-1]
    for h in range(PEER_HEADS):
        sc = []
        for p in range(2):
            c0 = (2 * h + p) * kd
            sc.append(_nt_dot(keys_ref[p], q_ref[:, c0:c0 + kd], precision=lax.Precision.HIGHEST))
        a = _top_sorted(sc[0])
        b = _top_sorted(sc[1])
        cand = _pair_sum_candidates(a[:K], b[:K])
        tau = _top16(cand)[K - 1][0:1]
        below = jnp.max(jnp.where(cand < tau, cand, -jnp.inf), axis=0, keepdims=True)
        for corner in (a[0:1] + b[K:K + 1], a[K:K + 1] + b[0:1]):
            below = jnp.maximum(below, jnp.where(corner < tau, corner, -jnp.inf))
        below = jnp.where(below == -jnp.inf, tau, below)
        cut = 0.5 * (tau + below)
        top = a[0:1] + b[0:1]
        z = jnp.sum(jnp.where(cand >= tau, jnp.exp(cand - top), 0.0), axis=0, keepdims=True)
        thr = cut - sc[0]
        e1 = jnp.exp(sc[0] - a[0:1]) * (0.5 / z)
        for c in range(thr_ref.shape[1]):
            thr_ref[h, c] = thr[:, c * LANES:(c + 1) * LANES]
            e1_ref[h, c] = e1[:, c * LANES:(c + 1) * LANES]
        s2_ref[h] = sc[1]
        e2_ref[h] = jnp.exp(sc[1] - b[0:1])


def _peer_route(q, sub_keys, tt):
    T = q.shape[0]
    tt = min(tt, T)
    nk = PEER_N_KEYS
    assert tt % LANES == 0
    big = jax.ShapeDtypeStruct((PEER_HEADS, nk, T), F32)
    big_spec = pl.BlockSpec((PEER_HEADS, nk, tt), lambda i: (0, 0, i))
    slab = jax.ShapeDtypeStruct((PEER_HEADS, T // LANES, nk, LANES), F32)
    slab_spec = pl.BlockSpec((PEER_HEADS, tt // LANES, nk, LANES), lambda i: (0, i, 0, 0))
    return pl.pallas_call(
        _peer_route_body,
        grid=(T // tt,),
        in_specs=[pl.BlockSpec((tt, q.shape[1]), lambda i: (i, 0)),
                  pl.BlockSpec(sub_keys.shape, lambda i: (0, 0, 0))],
        out_specs=[slab_spec, slab_spec, big_spec, big_spec],
        out_shape=[slab, slab, big, big],
        compiler_params=_params("parallel"),
        name="peer_route",
    )(q, sub_keys.astype(F32))


def _peer_main_body(h_ref, u_ref, vt_ref, thr_ref, e1_ref, s2_ref, e2_ref, x_ref, g_ref,
                    o_ref, acc_ref, gate_ref, ga_ref, ht_ref, *, rows_per_step):
    nk = PEER_N_KEYS
    lanes = LANES
    j = pl.program_id(1)
    tt = h_ref.shape[0]

    @pl.when(j == 0)
    def _():
        acc_ref[...] = jnp.zeros_like(acc_ref)
        ht_ref[...] = h_ref[...].astype(F32).T.astype(ht_ref.dtype)

    i1_base = pl.multiple_of(j * rows_per_step, rows_per_step)
    group = 4
    sub = 64
    n_sub = nk // sub

    def gate_tile(ti, carry):
        cs = pl.ds(pl.multiple_of((ti // n_sub) * lanes, lanes), lanes)
        k0 = pl.multiple_of((ti % n_sub) * sub, sub)
        for r0 in range(0, rows_per_step, group):
            gates = [None] * group
            for h in range(PEER_HEADS):
                s2 = s2_ref[h, pl.ds(k0, sub), cs]
                e2 = e2_ref[h, pl.ds(k0, sub), cs]
                for g in range(group):
                    r = r0 + g
                    row = pl.ds(i1_base + r, SUBLANES, stride=0)
                    thr = thr_ref[h, ti // n_sub, row, :]
                    e1 = e1_ref[h, ti // n_sub, row, :]
                    tiles = (sub // SUBLANES, SUBLANES, lanes)
                    w = (jnp.where(s2.reshape(tiles) >= thr, e2.reshape(tiles), 0.0) * e1).reshape(sub, lanes)
                    gates[g] = w if gates[g] is None else gates[g] + w
            for g in range(group):
                rs = pl.ds(pl.multiple_of((r0 + g) * nk + k0, sub), sub)
                gate_ref[rs, cs] = gates[g]
        return carry

    lax.fori_loop(0, (tt // lanes) * n_sub, gate_tile, 0)
    pre = jnp.dot(u_ref[...].astype(BF16), ht_ref[...], preferred_element_type=F32)
    ga_ref[...] = (gate_ref[...] * (pre * (1.0 + lax.erf(pre * (2.0 ** -0.5))))).astype(ga_ref.dtype)
    acc_ref[...] += jnp.dot(vt_ref[0], ga_ref[...], preferred_element_type=F32)

    @pl.when(j == pl.num_programs(1) - 1)
    def _():
        o_ref[...] = _rms(x_ref[...] + acc_ref[...].T, g_ref[...])


def _peer_main(hn, u, v, thr, e1, s2, e2, x2d, g_final, tt, rows_per_step):
    T, D = x2d.shape
    tt = min(tt, T)
    nk = PEER_N_KEYS
    eb = rows_per_step * nk
    n_exp = u.shape[0]
    assert n_exp == nk * nk and nk % rows_per_step == 0
    vt = v.astype(BF16).reshape(n_exp // eb, eb, D).transpose(0, 2, 1)
    score_spec = pl.BlockSpec((PEER_HEADS, nk, tt), lambda i, j: (0, 0, i))
    slab_spec = pl.BlockSpec((PEER_HEADS, tt // LANES, nk, LANES), lambda i, j: (0, i, 0, 0))
    return pl.pallas_call(
        functools.partial(_peer_main_body, rows_per_step=rows_per_step),
        grid=(T // tt, n_exp // eb),
        in_specs=[pl.BlockSpec((tt, D), lambda i, j: (i, 0)),
                  pl.BlockSpec((eb, D), lambda i, j: (j, 0)),
                  pl.BlockSpec((1, D, eb), lambda i, j: (j, 0, 0)),
                  slab_spec, slab_spec, score_spec, score_spec,
                  pl.BlockSpec((tt, D), lambda i, j: (i, 0)),
                  pl.BlockSpec((1, D), lambda i, j: (0, 0))],
        out_specs=pl.BlockSpec((tt, D), lambda i, j: (i, 0)),
        out_shape=jax.ShapeDtypeStruct((T, D), F32),
        scratch_shapes=[pltpu.VMEM((D, tt), F32), pltpu.VMEM((eb, tt), F32), pltpu.VMEM((eb, tt), BF16),
                        pltpu.VMEM((D, tt), BF16)],
        compiler_params=_params("parallel", "arbitrary"),
        name="peer_main",
    )(hn, u, vt, thr, e1, s2, e2, x2d, g_final.reshape(1, D).astype(F32))


def kernel(x, mem, rel_bias, g_mix, w_in, b_gate, w_branch_a, w_branch_b, w_out, g_cross, g_mem, w_q_cross,
           w_kv_cross, w_o_cross, g_ffn, w_q_peer, peer_sub_keys, peer_u, peer_v, g_final):
    B, S, D = x.shape
    T = B * S
    depth = w_in.shape[0]
    ndil = len(DIL_GROUPS) * DIL_HEADS
    x2d = x.reshape(T, D)

    dil_bias = [_bias_lookup(_dil_bucket_map(d), rel_bias[:, gi * DIL_HEADS:(gi + 1) * DIL_HEADS], f"dil_bias_g{gi}")
                for gi, (_, d) in enumerate(DIL_GROUPS)]
    mmap = _moba_bucket_map()
    moba_bias = _bias_lookup(mmap.reshape(-1, MOBA_BLOCK), rel_bias[:, ndil:], "moba_bias")
    moba_bias = moba_bias.reshape(MOBA_HEADS // 2, 2, MOBA_BIAS_TILES, MOBA_BLOCK, MOBA_BLOCK)

    assert depth == 1, "the final norm is fused into the last PEER sweep; one layer supported"
    l = 0
    gc = DIL_GROUP_COLS
    assert DIL_GROUPS[0][1] == 1
    w_tok = jnp.concatenate([w_in[l][:, :gc], w_in[l][:, DIL_WIDTH:QKV_WIDTH]], axis=1)
    tok = _norm_matmul(x2d, g_mix[l], w_tok, out_dtype=BF16, tm=1024, tn=w_tok.shape[1], name="proj_tok")
    dil_qkv = [tok.reshape(B, 1, S, -1)]
    for gi, (_, d) in enumerate(DIL_GROUPS[1:], start=1):
        dil_qkv.append(_norm_matmul(x2d, g_mix[l], w_in[l][:, gi * gc:(gi + 1) * gc], out_dtype=BF16, tm=1024, tn=gc,
                                    dil=d, seq=S, name=f"proj_dil_g{gi}"))
    gates = _norm_matmul(x2d, g_mix[l], w_in[l][:, QKV_WIDTH:], out_dtype=BF16, tm=1024, tn=2 * D,
                         bias=b_gate[l], name="proj_gates")
    dil_out = [_dilated_group(a, dil_bias[gi], f"dilated_g{gi}") for gi, a in enumerate(dil_qkv)]
    o_moba = _moba(tok.reshape(B, S, -1), moba_bias, first_col=gc).reshape(T, MOBA_HEADS * HEAD_DIM)
    x2d = _merge(dil_out, o_moba, gates, x2d, w_branch_a[l], w_branch_b[l], w_out[l], seq=S, tm=512)
    M = mem.shape[1]
    q_c = _norm_matmul(x2d, g_cross[l], w_q_cross[l], out_dtype=BF16, tm=1024, tn=512, name="cross_q")
    kv = _norm_matmul(mem.reshape(B * M, D), g_mem[l], w_kv_cross[l], out_dtype=BF16, tm=1024, tn=1024, name="cross_kv")
    x2d = _cross(q_c.reshape(B, S, -1), kv.reshape(B, M, -1), x2d.reshape(B, S, D), w_o_cross[l], tm=1024).reshape(T, D)
    q_p, hn = _norm_matmul(x2d, g_ffn[l], w_q_peer[l], out_dtype=F32, tm=512, tn=w_q_peer.shape[-1], emit_normed=True,
                           name="peer_q")
    thr, e1, s2, e2 = _peer_route(q_p, peer_sub_keys[l], tt=512)
    y = _peer_main(hn, peer_u[l], peer_v[l], thr, e1, s2, e2, x2d, g_final,
                   tt=512, rows_per_step=8)
    return y.reshape(B, S, D)
```

```python
import functools
import math

import jax
import jax.numpy as jnp
import numpy as np
from jax import lax
from jax.experimental import pallas as pl
from jax.experimental.pallas import tpu as pltpu

HEAD_DIM = 64
DIL_GROUPS = ((128, 1), (512, 4), (2048, 16))
DIL_HEADS = 4
DIL_BLOCK = 128
MOBA_HEADS = 8
MOBA_BLOCK = 256
MOBA_TOPK = 3
MOBA_BIAS_TILES = 8
MOBA_GROUP = 4
MOBA_SPLIT = 2
MOBA_MASK = -(2.0 ** 100)
REL_BUCKETS = 32
REL_MAX_DIST = 2048
CROSS_HEADS = 4
CROSS_HEAD_DIM = 128
PEER_HEADS = 8
PEER_N_KEYS = 128
PEER_TOPK = 16
RMS_EPS = 1e-6
NEG_INF = -1e30
DIL_GROUP_COLS = 3 * DIL_HEADS * HEAD_DIM
DIL_WIDTH = len(DIL_GROUPS) * DIL_GROUP_COLS
MOBA_WIDTH = 3 * MOBA_HEADS * HEAD_DIM
QKV_WIDTH = DIL_WIDTH + MOBA_WIDTH

LANES = 128
SUBLANES = 8
V7X_VMEM_BYTES = 64 * 1024 * 1024
VMEM_LIMIT = V7X_VMEM_BYTES * 3 // 4

BF16 = jnp.bfloat16
F32 = jnp.float32


def _params(*sem):
    return pltpu.CompilerParams(dimension_semantics=sem, vmem_limit_bytes=VMEM_LIMIT)


def _rms(x, g):
    return x * lax.rsqrt(jnp.mean(x * x, axis=-1, keepdims=True) + RMS_EPS) * g


def _nt_dot(a, b, precision=None):
    return lax.dot_general(a, b, (((1,), (1,)), ((), ())), preferred_element_type=F32, precision=precision)


def _norm_matmul_body(x_ref, g_ref, w_ref, *rest, sigmoid_bias, dil, emit_normed):
    rest = list(rest)
    b_ref = rest.pop(0) if sigmoid_bias else None
    o_ref = rest.pop(0)
    hn_ref = rest.pop(0) if emit_normed else None
    rows = x_ref.shape[0] // dil

    if dil == 1:
        h = _rms(x_ref[...], g_ref[...]).astype(BF16)
    else:
        cols_ref = rest.pop(0)
        n_slab = cols_ref.shape[0]
        for c in range(n_slab):
            cols_ref[c] = x_ref[:, c * LANES:(c + 1) * LANES]
        h = jnp.concatenate(
            [_rms(jnp.concatenate([cols_ref[c, pl.ds(r, rows, stride=dil), :] for c in range(n_slab)], axis=-1),
                  g_ref[...]).astype(BF16) for r in range(dil)], axis=0)
    if emit_normed:
        hn_ref[...] = h

    acc = jnp.dot(h, w_ref[...], preferred_element_type=F32)
    if sigmoid_bias:
        acc = jax.nn.sigmoid(acc + b_ref[...])
    if dil == 1:
        o_ref[...] = acc.astype(o_ref.dtype)
    else:
        for r in range(dil):
            o_ref[0, r] = acc[r * rows:(r + 1) * rows].astype(o_ref.dtype)


def _norm_matmul(x2d, g, w, *, out_dtype, tm, tn, bias=None, dil=1, seq=None, emit_normed=False, name):
    T, D = x2d.shape
    N = w.shape[1]
    tm = min(tm, T if seq is None else seq)
    assert T % tm == 0 and N % tn == 0
    in_specs = [
        pl.BlockSpec((tm, D), lambda i, j: (i, 0)),
        pl.BlockSpec((1, D), lambda i, j: (0, 0)),
        pl.BlockSpec((D, tn), lambda i, j: (0, j)),
    ]
    args = [x2d, g.reshape(1, D).astype(F32), w.astype(BF16)]
    if bias is not None:
        in_specs.append(pl.BlockSpec((1, tn), lambda i, j: (0, j)))
        args.append(bias.reshape(1, N).astype(F32))
    assert not (emit_normed and dil > 1)
    scratch = []
    if dil == 1:
        out_spec = pl.BlockSpec((tm, tn), lambda i, j: (i, j))
        out_shape = jax.ShapeDtypeStruct((T, N), out_dtype)
    else:
        assert seq % tm == 0 and tm % (2 * SUBLANES * dil) == 0 and D % LANES == 0
        per_seq = seq // tm
        out_spec = pl.BlockSpec((1, dil, tm // dil, tn), lambda i, j: (i // per_seq, 0, i % per_seq, j))
        out_shape = jax.ShapeDtypeStruct((T // seq, dil, seq // dil, N), out_dtype)
        scratch.append(pltpu.VMEM((D // LANES, tm, LANES), F32))
    if emit_normed:
        out_spec = [out_spec, pl.BlockSpec((tm, D), lambda i, j: (i, 0))]
        out_shape = [out_shape, jax.ShapeDtypeStruct((T, D), BF16)]
    return pl.pallas_call(
        functools.partial(_norm_matmul_body, sigmoid_bias=bias is not None, dil=dil, emit_normed=emit_normed),
        grid=(T // tm, N // tn),
        in_specs=in_specs,
        out_specs=out_spec,
        out_shape=out_shape,
        scratch_shapes=scratch,
        compiler_params=_params("parallel", "arbitrary"),
        name=name,
    )(*args)


def _t5_bucket_np(n):
    max_exact = REL_BUCKETS // 2
    nf = np.maximum(n, 1).astype(np.float32)
    large = max_exact + (np.log(nf / np.float32(max_exact)) / np.float32(math.log(REL_MAX_DIST / max_exact))
                         * np.float32(REL_BUCKETS - max_exact)).astype(np.int32)
    large = np.minimum(large, REL_BUCKETS - 1)
    return np.where(n < max_exact, n, large).astype(np.int32)


def _dil_bucket_map(d):
    blk = DIL_BLOCK
    rel = (blk + np.arange(blk))[:, None] - np.arange(2 * blk)[None, :]
    return _t5_bucket_np(np.clip(rel, 0, None) * d)


def _moba_bucket_map():
    blk = MOBA_BLOCK
    rel = np.arange(blk)[:, None] - np.arange(blk)[None, :]
    tiles = [_t5_bucket_np(np.clip(rel + dlt * blk, 0, None)) for dlt in range(MOBA_BIAS_TILES)]
    tiles[0] = np.where(rel >= 0, tiles[0], -1)
    far = _t5_bucket_np(np.arange((MOBA_BIAS_TILES - 2) * blk + 1, 64 * blk))
    assert (far == REL_BUCKETS - 1).all()
    return np.stack(tiles)


def _bias_lookup_body(idx_ref, range_ref, tab_ref, o_ref):
    h, i = pl.program_id(0), pl.program_id(1)
    idx = idx_ref[...]

    def bucket(b, acc):
        return jnp.where(idx == b, tab_ref[b, h], acc)

    o_ref[0] = lax.fori_loop(range_ref[i, 0], range_ref[i, 1], bucket, jnp.full(idx.shape, NEG_INF, F32))


def _bias_lookup(bucket_map, table, name):
    R, C = bucket_map.shape
    H = table.shape[1]
    rt = min(R, 256)
    assert R % rt == 0
    tiles = bucket_map.reshape(R // rt, rt * C)
    ranges = np.stack([np.where(tiles >= 0, tiles, REL_BUCKETS).min(axis=1), tiles.max(axis=1) + 1], axis=1)
    return pl.pallas_call(
        _bias_lookup_body,
        grid=(H, R // rt),
        in_specs=[pl.BlockSpec((rt, C), lambda h, i: (i, 0)), pl.BlockSpec(memory_space=pltpu.SMEM),
                  pl.BlockSpec(memory_space=pltpu.SMEM)],
        out_specs=pl.BlockSpec((1, rt, C), lambda h, i: (h, i, 0)),
        out_shape=jax.ShapeDtypeStruct((H, R, C), F32),
        compiler_params=_params("parallel", "parallel"),
        name=name,
    )(jnp.asarray(bucket_map), jnp.asarray(ranges.astype(np.int32)), table.astype(F32))


def _dilated_body(q_ref, kp_ref, kc_ref, vp_ref, vc_ref, bias_ref, o_ref, *, qblocks):
    blk, hd = DIL_BLOCK, HEAD_DIM
    gw = DIL_HEADS * hd
    n = pl.program_id(2)
    k = jnp.concatenate([kp_ref[0, 0], kc_ref[0, 0]], axis=0)
    v = jnp.concatenate([vp_ref[0, 0], vc_ref[0, 0]], axis=0)
    lane = lax.broadcasted_iota(jnp.int32, (1, gw), 1)
    qi = lax.broadcasted_iota(jnp.int32, (blk, 2 * blk), 0)
    kj = lax.broadcasted_iota(jnp.int32, (blk, 2 * blk), 1)
    rel = blk + qi - kj
    band = (rel >= 0) & (rel <= blk)
    first_key = jnp.where(n > 0, 0, blk)
    for i in range(qblocks):
        rows = slice(i * blk, (i + 1) * blk)
        q = q_ref[0, 0, rows, :]
        kw = k[i * blk:(i + 2) * blk]
        vw = v[i * blk:(i + 2) * blk]
        mask = band & (kj >= first_key) if i == 0 else band
        head_lanes = [(lane >= h * hd) & (lane < (h + 1) * hd) for h in range(DIL_HEADS)]
        q_all = jnp.concatenate([jnp.where(hl, q, jnp.zeros_like(q)) for hl in head_lanes], axis=0)
        s = _nt_dot(q_all, kw) * (hd ** -0.5) + bias_ref[...]
        s = jnp.where(jnp.concatenate([mask] * DIL_HEADS, axis=0), s, NEG_INF)
        m = jnp.max(s, axis=-1, keepdims=True)
        p = jnp.exp(s - m)
        l = jnp.sum(p, axis=-1, keepdims=True)
        pv = jnp.dot(p.astype(BF16), vw, preferred_element_type=F32) / l
        lse_all = m + jnp.log(l)
        out = jnp.zeros((blk, gw), F32)
        lse = jnp.zeros((blk, gw), F32)
        for h, hl in enumerate(head_lanes):
            out = jnp.where(hl, pv[h * blk:(h + 1) * blk], out)
            lse = jnp.where(hl, lse_all[h * blk:(h + 1) * blk], lse)
        o_ref[0, 0, rows, :gw] = out
        o_ref[0, 0, rows, gw:] = lse


def _dilated_group(qkv, bias, name):
    B, d, L, _ = qkv.shape
    blk, gw = DIL_BLOCK, DIL_HEADS * HEAD_DIM
    assert L % blk == 0
    qblocks = min(4, L // blk)
    assert L % (qblocks * blk) == 0
    rows = qblocks * blk

    def spec(j, prev):
        if prev:
            return pl.BlockSpec((1, 1, blk, gw), lambda b, r, n: (b, r, jnp.maximum(n * qblocks - 1, 0), j))
        return pl.BlockSpec((1, 1, rows, gw), lambda b, r, n: (b, r, n, j))

    return pl.pallas_call(
        functools.partial(_dilated_body, qblocks=qblocks),
        grid=(B, d, L // rows),
        in_specs=[spec(0, False), spec(1, True), spec(1, False), spec(2, True), spec(2, False),
                  pl.BlockSpec((DIL_HEADS * blk, 2 * blk), lambda b, r, n: (0, 0))],
        out_specs=pl.BlockSpec((1, 1, rows, 2 * gw), lambda b, r, n: (b, r, n, 0)),
        out_shape=jax.ShapeDtypeStruct((B, d, L, 2 * gw), F32),
        compiler_params=_params("parallel", "parallel", "arbitrary"),
        name=name,
    )(qkv, qkv, qkv, qkv, qkv, bias.reshape(DIL_HEADS * blk, 2 * blk))


def _moba_body(q_ref, k_ref, v_ref, bias_ref, o_ref, qa_ref, ka_ref, *, S):
    blk, hd = MOBA_BLOCK, HEAD_DIM
    nblk = S // blk
    scale = hd ** -0.5
    lane = lax.broadcasted_iota(jnp.int32, (1, 2 * hd), 1)
    kmean = jnp.mean(k_ref[0].astype(F32).reshape(nblk, blk, 2 * hd), axis=1)
    grp = MOBA_GROUP * blk
    row_blk = lax.broadcasted_iota(jnp.int32, (S, 2 * hd), 0) // blk
    q2, k2 = q_ref[0], k_ref[0]

    ka_ref[:, :2 * hd] = k2
    ka_ref[:, 2 * hd:] = jnp.where(lane == row_blk, 1.0, 0.0).astype(BF16)
    blk_t = lax.broadcasted_iota(jnp.int32, (nblk, S), 0)
    q_blk_t = lax.broadcasted_iota(jnp.int32, (nblk, S), 1) // blk
    past = blk_t < q_blk_t
    blk_f = blk_t.astype(F32)
    for h in range(2):
        head_lanes = (lane >= h * hd) & (lane < (h + 1) * hd)
        qh = jnp.where(head_lanes, q2, jnp.zeros_like(q2))
        gs = _nt_dot(jnp.where(head_lanes, kmean, 0.0), qh.astype(F32), precision=lax.Precision.HIGHEST)
        gs = jnp.where(past, gs, NEG_INF)
        sel = jnp.zeros(gs.shape, F32)
        for _r in range(MOBA_TOPK):
            gmax = jnp.max(gs, axis=0, keepdims=True)
            first = jnp.min(jnp.where(gs == gmax, blk_f, float(nblk)), axis=0, keepdims=True)
            pick = blk_f == first
            sel = jnp.where(pick & past, 1.0, sel)
            gs = jnp.where(pick, -jnp.inf, gs)
        penalty_t = jnp.where((sel > 0.5) | (blk_t == q_blk_t), 0.0, MOBA_MASK)
        penalty_t = jnp.concatenate([penalty_t, jnp.zeros((2 * hd - nblk, S), F32)], axis=0)
        qa_ref[h, :, :2 * hd] = qh * scale
        qa_ref[h, :, 2 * hd:] = penalty_t.T.astype(BF16)

    def q_block(qb, _):
        row0 = pl.multiple_of(qb * blk, blk)
        n_grp = qb // MOBA_GROUP + 1
        qa = jnp.concatenate([qa_ref[h, pl.ds(row0, blk), :] for h in range(2)], axis=0)

        def key_group(i, carry):
            m, l, acc = carry
            g = n_grp - 1 - i
            col0 = pl.multiple_of(g * grp, grp)
            per = MOBA_GROUP // MOBA_SPLIT
            half = per * blk
            cols = [pl.ds(pl.multiple_of(col0 + i * half, half), half) for i in range(MOBA_SPLIT)]
            scores = [_nt_dot(qa, ka_ref[c, :]) for c in cols]
            for i in range(MOBA_SPLIT):
                parts = []
                for jj in range(per):
                    n = g * MOBA_GROUP + i * per + jj
                    tile = jnp.clip(qb - n, 0, MOBA_BIAS_TILES - 1)
                    bias = jnp.concatenate([bias_ref[0, h, tile] for h in range(2)], axis=0)
                    parts.append(scores[i][:, jj * blk:(jj + 1) * blk] + bias)
                m_new = jnp.maximum(m, jnp.max(functools.reduce(jnp.maximum, parts), axis=-1, keepdims=True))
                a = jnp.exp(m - m_new)
                ps = [jnp.exp(p - m_new) for p in parts]
                l = l * a + jnp.sum(functools.reduce(jnp.add, ps), axis=-1, keepdims=True)
                p_all = jnp.concatenate([p.astype(BF16) for p in ps], axis=-1)
                pv = jnp.dot(p_all, v_ref[0, cols[i], :], preferred_element_type=F32)
                m, acc = m_new, acc * a + pv
            return m, l, acc

        init = (jnp.full((2 * blk, 1), -jnp.inf, F32), jnp.zeros((2 * blk, 1), F32), jnp.zeros((2 * blk, 2 * hd), F32))
        _, l, acc = lax.fori_loop(0, n_grp, key_group, init)
        o = acc / l
        o_ref[0, pl.ds(row0, blk), :] = jnp.where(lane < hd, o[:blk], o[blk:]).astype(o_ref.dtype)
        return 0

    lax.fori_loop(0, nblk, q_block, 0)


def _moba(qkv, bias, first_col):
    B, S, C = qkv.shape
    pw = 2 * HEAD_DIM
    npair = MOBA_HEADS // 2
    assert S % (MOBA_GROUP * MOBA_BLOCK) == 0 and S // MOBA_BLOCK <= HEAD_DIM and first_col % pw == 0
    c0 = first_col // pw

    def spec(j):
        return pl.BlockSpec((1, S, pw), lambda hp, b: (b, 0, c0 + j * npair + hp))

    return pl.pallas_call(
        functools.partial(_moba_body, S=S),
        grid=(npair, B),
        in_specs=[spec(0), spec(1), spec(2),
                  pl.BlockSpec((1, 2, MOBA_BIAS_TILES, MOBA_BLOCK, MOBA_BLOCK), lambda hp, b: (hp, 0, 0, 0, 0))],
        out_specs=pl.BlockSpec((1, S, pw), lambda hp, b: (b, 0, hp)),
        out_shape=jax.ShapeDtypeStruct((B, S, MOBA_HEADS * HEAD_DIM), BF16),
        scratch_shapes=[pltpu.VMEM((2, S, 2 * pw), BF16), pltpu.VMEM((S, 2 * pw), BF16)],
        compiler_params=_params("parallel", "parallel"),
        name="moba",
    )(qkv, qkv, qkv, bias)


def _merge_body(d0_ref, d1_ref, d2_ref, om_ref, gate_ref, x_ref, wa_ref, wb_ref, wo_ref, x1_ref, t1_ref, t2_ref):
    D = x_ref.shape[-1]
    gw = DIL_HEADS * HEAD_DIM
    n_slab = t1_ref.shape[0]
    for src, dst in ((d1_ref, t1_ref), (d2_ref, t2_ref)):
        dil, rows = src.shape[1], src.shape[2]
        for r in range(dil):
            for c in range(n_slab):
                dst[c, pl.ds(r, rows, stride=dil), :] = src[0, r, :, c * LANES:(c + 1) * LANES]
    g0 = d0_ref[...]
    g1 = jnp.concatenate([t1_ref[c] for c in range(n_slab)], axis=-1)
    g2 = jnp.concatenate([t2_ref[c] for c in range(n_slab)], axis=-1)
    l0, l1, l2 = g0[:, gw:], g1[:, gw:], g2[:, gw:]
    mx = jnp.maximum(jnp.maximum(l0, l1), l2)
    e0, e1, e2 = jnp.exp(l0 - mx), jnp.exp(l1 - mx), jnp.exp(l2 - mx)
    o_dil = (e0 * g0[:, :gw] + e1 * g1[:, :gw] + e2 * g2[:, :gw]) / (e0 + e1 + e2)
    a = jnp.dot(o_dil.astype(BF16), wa_ref[...], preferred_element_type=F32)
    b = jnp.dot(om_ref[...], wb_ref[...], preferred_element_type=F32)
    merged = gate_ref[:, :D].astype(F32) * a + gate_ref[:, D:].astype(F32) * b
    x1_ref[...] = x_ref[...] + jnp.dot(merged.astype(BF16), wo_ref[...], preferred_element_type=F32)


def _merge(dil_out, o_moba, gates, x2d, w_a, w_b, w_o, seq, tm):
    T, D = x2d.shape
    tm = min(tm, seq)
    per_seq = seq // tm
    assert seq % tm == 0 and dil_out[0].shape[1] == 1

    def rows(width):
        return pl.BlockSpec((tm, width), lambda i: (i, 0))

    def whole(w):
        return pl.BlockSpec(w.shape, lambda i: (0, 0))

    def residue_major(a):
        d, width = a.shape[1], a.shape[3]
        assert tm % (8 * d) == 0
        return pl.BlockSpec((1, d, tm // d, width), lambda i: (i // per_seq, 0, i % per_seq, 0))

    ws = [w_a.astype(BF16), w_b.astype(BF16), w_o.astype(BF16)]
    width = dil_out[0].shape[-1]
    return pl.pallas_call(
        _merge_body,
        grid=(T // tm,),
        in_specs=[rows(width), residue_major(dil_out[1]), residue_major(dil_out[2]),
                  rows(o_moba.shape[-1]), rows(2 * D), rows(D)] + [whole(w) for w in ws],
        out_specs=rows(D),
        out_shape=jax.ShapeDtypeStruct((T, D), F32),
        scratch_shapes=[pltpu.VMEM((width // LANES, tm, LANES), F32)] * 2,
        compiler_params=_params("parallel"),
        name="merge",
    )(dil_out[0].reshape(T, width), dil_out[1], dil_out[2], o_moba, gates, x2d, *ws)


def _cross_body(q_ref, kv_ref, x_ref, wo_ref, x2_ref):
    hd = CROSS_HEAD_DIM
    width = CROSS_HEADS * hd
    heads = []
    for h in range(CROSS_HEADS):
        sl = slice(h * hd, (h + 1) * hd)
        s = _nt_dot(q_ref[0, :, sl], kv_ref[0, :, sl]) * (hd ** -0.5)
        m = jnp.max(s, axis=-1, keepdims=True)
        p = jnp.exp(s - m)
        l = jnp.sum(p, axis=-1, keepdims=True)
        v = kv_ref[0, :, width + h * hd: width + (h + 1) * hd]
        heads.append((jnp.dot(p.astype(BF16), v, preferred_element_type=F32) / l).astype(BF16))
    o = jnp.concatenate(heads, axis=-1)
    x2_ref[0] = x_ref[0] + jnp.dot(o, wo_ref[...], preferred_element_type=F32)


def _cross(q, kv, x, w_o, tm):
    B, S, D = x.shape
    M = kv.shape[1]
    tm = min(tm, S)
    w_o = w_o.astype(BF16)
    return pl.pallas_call(
        _cross_body,
        grid=(B, S // tm),
        in_specs=[pl.BlockSpec((1, tm, q.shape[-1]), lambda b, i: (b, i, 0)),
                  pl.BlockSpec((1, M, kv.shape[-1]), lambda b, i: (b, 0, 0)),
                  pl.BlockSpec((1, tm, D), lambda b, i: (b, i, 0)),
                  pl.BlockSpec(w_o.shape, lambda b, i: (0, 0))],
        out_specs=pl.BlockSpec((1, tm, D), lambda b, i: (b, i, 0)),
        out_shape=jax.ShapeDtypeStruct((B, S, D), F32),
        compiler_params=_params("parallel", "parallel"),
        name="cross",
    )(q, kv, x, w_o)


def _merge_exchange_pairs(n):
    pairs = []
    p = 1
    while p < n:
        k = p
        while k >= 1:
            for j in range(k % p, n - k, 2 * k):
                for i in range(min(k, n - j - k)):
                    if (i + j) // (2 * p) == (i + j + k) // (2 * p):
                        pairs.append((i + j, i + j + k))
            k //= 2
        p *= 2
    return pairs


def _top16(s):
    n, sub = PEER_TOPK, SUBLANES
    assert s.shape[0] % sub == 0 and sub * n >= s.shape[0] >= 2 * sub
    v = [s[k * sub:(k + 1) * sub] for k in range(s.shape[0] // sub)] + [None] * (n - s.shape[0] // sub)

    def exchange(i, j):
        if v[j] is None:
            return
        if v[i] is None:
            v[i], v[j] = v[j], None
        else:
            v[i], v[j] = jnp.maximum(v[i], v[j]), jnp.minimum(v[i], v[j])

    for i, j in _merge_exchange_pairs(n):
        exchange(i, j)
    for shift in (sub // 2, sub // 4, sub // 8):
        w = [None if x is None else pltpu.roll(x, shift, 0) for x in v]
        for k in range(n):
            other = w[n - 1 - k]
            if other is not None:
                v[k] = other if v[k] is None else jnp.maximum(v[k], other)
        d = n // 2
        while d >= 1:
            for i in range(n):
                if i & d == 0:
                    exchange(i, i + d)
            d //= 2
    return v


def _top_sorted(s):
    v = _top16(s)
    nxt = jnp.max(jnp.where(s < v[-1][0:1], s, -jnp.inf), axis=0, keepdims=True)
    return jnp.concatenate([x[0:1] for x in v] + [nxt], axis=0)


def _pair_sum_candidates(a, b):
    assert PEER_TOPK == 16
    row = lax.broadcasted_iota(jnp.int32, (PEER_TOPK, 1), 0)
    blocks = [a[0:1] + b]
    blocks += [a[i:i + 1] + b[0:8] for i in (1, 2, 3)]
    blocks += [jnp.where(row >= 4, b[0:1] + a, -jnp.inf)]
    blocks += [jnp.where(row[0:8] >= 4, b[j:j + 1] + a[0:8], -jnp.inf) for j in (1, 2)]
    return jnp.concatenate(blocks, axis=0)


def _peer_route_body(q_ref, keys_ref, thr_ref, e1_ref, s2_ref, e2_ref):
    nk, K = PEER_N_KEYS, PEER_TOPK
    kd = keys_ref.shape[-1]
    for h in range(PEER_HEADS):
        sc = []
        for p in range(2):
            c0 = (2 * h + p) * kd
            sc.append(_nt_dot(keys_ref[p], q_ref[:, c0:c0 + kd], precision=lax.Precision.HIGHEST))
        a = _top_sorted(sc[0])
        b = _top_sorted(sc[1])
        cand = _pair_sum_candidates(a[:K], b[:K])
        tau = _top16(cand)[K - 1][0:1]
        below = jnp.max(jnp.where(cand < tau, cand, -jnp.inf), axis=0, keepdims=True)
        for corner in (a[0:1] + b[K:K + 1], a[K:K + 1] + b[0:1]):
            below = jnp.maximum(below, jnp.where(corner < tau, corner, -jnp.inf))
        below = jnp.where(below == -jnp.inf, tau, below)
        cut = 0.5 * (tau + below)
        top = a[0:1] + b[0:1]
        z = jnp.sum(jnp.where(cand >= tau, jnp.exp(cand - top), 0.0), axis=0, keepdims=True)
        thr = cut - sc[0]
        e1 = jnp.exp(sc[0] - a[0:1]) * (0.5 / z)
        for c in range(thr_ref.shape[1]):
            thr_ref[h, c] = thr[:, c * LANES:(c + 1) * LANES]
            e1_ref[h, c] = e1[:, c * LANES:(c + 1) * LANES]
        s2_ref[h] = sc[1]
        e2_ref[h] = jnp.exp(sc[1] - b[0:1])


def _peer_route(q, sub_keys, tt):
    T = q.shape[0]
    tt = min(tt, T)
    nk = PEER_N_KEYS
    assert tt % LANES == 0
    big = jax.ShapeDtypeStruct((PEER_HEADS, nk, T), F32)
    big_spec = pl.BlockSpec((PEER_HEADS, nk, tt), lambda i: (0, 0, i))
    slab = jax.ShapeDtypeStruct((PEER_HEADS, T // LANES, nk, LANES), F32)
    slab_spec = pl.BlockSpec((PEER_HEADS, tt // LANES, nk, LANES), lambda i: (0, i, 0, 0))
    return pl.pallas_call(
        _peer_route_body,
        grid=(T // tt,),
        in_specs=[pl.BlockSpec((tt, q.shape[1]), lambda i: (i, 0)),
                  pl.BlockSpec(sub_keys.shape, lambda i: (0, 0, 0))],
        out_specs=[slab_spec, slab_spec, big_spec, big_spec],
        out_shape=[slab, slab, big, big],
        compiler_params=_params("parallel"),
        name="peer_route",
    )(q, sub_keys.astype(F32))


def _peer_main_body(h_ref, u_ref, vt_ref, thr_ref, e1_ref, s2_ref, e2_ref, x_ref, g_ref,
                    o_ref, acc_ref, gate_ref, ga_ref, ht_ref, *, rows_per_step):
    nk = PEER_N_KEYS
    lanes = LANES
    j = pl.program_id(1)
    tt = h_ref.shape[0]

    @pl.when(j == 0)
    def _():
        acc_ref[...] = jnp.zeros_like(acc_ref)
        ht_ref[...] = h_ref[...].astype(F32).T.astype(ht_ref.dtype)

    i1_base = pl.multiple_of(j * rows_per_step, rows_per_step)
    group = 4
    sub = 64
    n_sub = nk // sub

    def gate_tile(ti, carry):
        cs = pl.ds(pl.multiple_of((ti // n_sub) * lanes, lanes), lanes)
        k0 = pl.multiple_of((ti % n_sub) * sub, sub)
        for r0 in range(0, rows_per_step, group):
            gates = [None] * group
            for h in range(PEER_HEADS):
                s2 = s2_ref[h, pl.ds(k0, sub), cs]
                e2 = e2_ref[h, pl.ds(k0, sub), cs]
                for g in range(group):
                    r = r0 + g
                    row = pl.ds(i1_base + r, SUBLANES, stride=0)
                    thr = thr_ref[h, ti // n_sub, row, :]
                    e1 = e1_ref[h, ti // n_sub, row, :]
                    tiles = (sub // SUBLANES, SUBLANES, lanes)
                    w = (jnp.where(s2.reshape(tiles) >= thr, e2.reshape(tiles), 0.0) * e1).reshape(sub, lanes)
                    gates[g] = w if gates[g] is None else gates[g] + w
            for g in range(group):
                rs = pl.ds(pl.multiple_of((r0 + g) * nk + k0, sub), sub)
                gate_ref[rs, cs] = gates[g]
        return carry

    lax.fori_loop(0, (tt // lanes) * n_sub, gate_tile, 0)
    pre = jnp.dot(u_ref[...], ht_ref[...], preferred_element_type=F32)
    ga_ref[...] = (gate_ref[...] * (pre * (1.0 + lax.erf(pre * (2.0 ** -0.5))))).astype(ga_ref.dtype)
    acc_ref[...] += jnp.dot(vt_ref[0], ga_ref[...], preferred_element_type=F32)

    @pl.when(j == pl.num_programs(1) - 1)
    def _():
        o_ref[...] = _rms(x_ref[...] + acc_ref[...].T, g_ref[...])


def _peer_main(hn, u, v, thr, e1, s2, e2, x2d, g_final, tt, rows_per_step):
    T, D = x2d.shape
    tt = min(tt, T)
    nk = PEER_N_KEYS
    eb = rows_per_step * nk
    n_exp = u.shape[0]
    assert n_exp == nk * nk and nk % rows_per_step == 0
    u = u.astype(BF16)
    vt = v.astype(BF16).reshape(n_exp // eb, eb, D).transpose(0, 2, 1)
    score_spec = pl.BlockSpec((PEER_HEADS, nk, tt), lambda i, j: (0, 0, i))
    slab_spec = pl.BlockSpec((PEER_HEADS, tt // LANES, nk, LANES), lambda i, j: (0, i, 0, 0))
    return pl.pallas_call(
        functools.partial(_peer_main_body, rows_per_step=rows_per_step),
        grid=(T // tt, n_exp // eb),
        in_specs=[pl.BlockSpec((tt, D), lambda i, j: (i, 0)),
                  pl.BlockSpec((eb, D), lambda i, j: (j, 0)),
                  pl.BlockSpec((1, D, eb), lambda i, j: (j, 0, 0)),
                  slab_spec, slab_spec, score_spec, score_spec,
                  pl.BlockSpec((tt, D), lambda i, j: (i, 0)),
                  pl.BlockSpec((1, D), lambda i, j: (0, 0))],
        out_specs=pl.BlockSpec((tt, D), lambda i, j: (i, 0)),
        out_shape=jax.ShapeDtypeStruct((T, D), F32),
        scratch_shapes=[pltpu.VMEM((D, tt), F32), pltpu.VMEM((eb, tt), F32), pltpu.VMEM((eb, tt), BF16),
                        pltpu.VMEM((D, tt), BF16)],
        compiler_params=_params("parallel", "arbitrary"),
        name="peer_main",
    )(hn, u, vt, thr, e1, s2, e2, x2d, g_final.reshape(1, D).astype(F32))


def kernel(x, mem, rel_bias, g_mix, w_in, b_gate, w_branch_a, w_branch_b, w_out, g_cross, g_mem, w_q_cross,
           w_kv_cross, w_o_cross, g_ffn, w_q_peer, peer_sub_keys, peer_u, peer_v, g_final):
    B, S, D = x.shape
    T = B * S
    depth = w_in.shape[0]
    ndil = len(DIL_GROUPS) * DIL_HEADS
    x2d = x.reshape(T, D)

    dil_bias = [_bias_lookup(_dil_bucket_map(d), rel_bias[:, gi * DIL_HEADS:(gi + 1) * DIL_HEADS], f"dil_bias_g{gi}")
                for gi, (_, d) in enumerate(DIL_GROUPS)]
    mmap = _moba_bucket_map()
    moba_bias = _bias_lookup(mmap.reshape(-1, MOBA_BLOCK), rel_bias[:, ndil:], "moba_bias")
    moba_bias = moba_bias.reshape(MOBA_HEADS // 2, 2, MOBA_BIAS_TILES, MOBA_BLOCK, MOBA_BLOCK)

    assert depth == 1, "the final norm is fused into the last PEER sweep; one layer supported"
    l = 0
    gc = DIL_GROUP_COLS
    assert DIL_GROUPS[0][1] == 1
    w_tok = jnp.concatenate([w_in[l][:, :gc], w_in[l][:, DIL_WIDTH:QKV_WIDTH]], axis=1)
    tok = _norm_matmul(x2d, g_mix[l], w_tok, out_dtype=BF16, tm=1024, tn=w_tok.shape[1], name="proj_tok")
    dil_qkv = [tok.reshape(B, 1, S, -1)]
    for gi, (_, d) in enumerate(DIL_GROUPS[1:], start=1):
        dil_qkv.append(_norm_matmul(x2d, g_mix[l], w_in[l][:, gi * gc:(gi + 1) * gc], out_dtype=BF16, tm=1024, tn=gc,
                                    dil=d, seq=S, name=f"proj_dil_g{gi}"))
    gates = _norm_matmul(x2d, g_mix[l], w_in[l][:, QKV_WIDTH:], out_dtype=BF16, tm=1024, tn=2 * D,
                         bias=b_gate[l], name="proj_gates")
    dil_out = [_dilated_group(a, dil_bias[gi], f"dilated_g{gi}") for gi, a in enumerate(dil_qkv)]
    o_moba = _moba(tok.reshape(B, S, -1), moba_bias, first_col=gc).reshape(T, MOBA_HEADS * HEAD_DIM)
    x2d = _merge(dil_out, o_moba, gates, x2d, w_branch_a[l], w_branch_b[l], w_out[l], seq=S, tm=512)
    M = mem.shape[1]
    q_c = _norm_matmul(x2d, g_cross[l], w_q_cross[l], out_dtype=BF16, tm=1024, tn=512, name="cross_q")
    kv = _norm_matmul(mem.reshape(B * M, D), g_mem[l], w_kv_cross[l], out_dtype=BF16, tm=1024, tn=1024, name="cross_kv")
    x2d = _cross(q_c.reshape(B, S, -1), kv.reshape(B, M, -1), x2d.reshape(B, S, D), w_o_cross[l], tm=1024).reshape(T, D)
    q_p, hn = _norm_matmul(x2d, g_ffn[l], w_q_peer[l], out_dtype=F32, tm=512, tn=w_q_peer.shape[-1], emit_normed=True,
                           name="peer_q")
    thr, e1, s2, e2 = _peer_route(q_p, peer_sub_keys[l], tt=512)
    y = _peer_main(hn, peer_u[l], peer_v[l], thr, e1, s2, e2, x2d, g_final,
                   tt=512, rows_per_step=8)
    return y.reshape(B, S, D)
```

```python
import functools
import math

import jax
import jax.numpy as jnp
import numpy as np
from jax import lax
from jax.experimental import pallas as pl
from jax.experimental.pallas import tpu as pltpu

HEAD_DIM = 64
DIL_GROUPS = ((128, 1), (512, 4), (2048, 16))
DIL_HEADS = 4
DIL_BLOCK = 128
MOBA_HEADS = 8
MOBA_BLOCK = 256
MOBA_TOPK = 3
MOBA_BIAS_TILES = 8
MOBA_GROUP = 4
MOBA_SPLIT = 2
MOBA_MASK = -(2.0 ** 100)
REL_BUCKETS = 32
REL_MAX_DIST = 2048
CROSS_HEADS = 4
CROSS_HEAD_DIM = 128
PEER_HEADS = 8
PEER_N_KEYS = 128
PEER_TOPK = 16
RMS_EPS = 1e-6
NEG_INF = -1e30
DIL_GROUP_COLS = 3 * DIL_HEADS * HEAD_DIM
DIL_WIDTH = len(DIL_GROUPS) * DIL_GROUP_COLS
MOBA_WIDTH = 3 * MOBA_HEADS * HEAD_DIM
QKV_WIDTH = DIL_WIDTH + MOBA_WIDTH

LANES = 128
SUBLANES = 8
V7X_VMEM_BYTES = 64 * 1024 * 1024
VMEM_LIMIT = V7X_VMEM_BYTES * 3 // 4

BF16 = jnp.bfloat16
F32 = jnp.float32


def _params(*sem):
    return pltpu.CompilerParams(dimension_semantics=sem, vmem_limit_bytes=VMEM_LIMIT)


def _rms(x, g):
    return x * lax.rsqrt(jnp.mean(x * x, axis=-1, keepdims=True) + RMS_EPS) * g


def _nt_dot(a, b, precision=None):
    return lax.dot_general(a, b, (((1,), (1,)), ((), ())), preferred_element_type=F32, precision=precision)


def _norm_matmul_body(x_ref, g_ref, w_ref, *rest, sigmoid_bias, dil, emit_normed):
    rest = list(rest)
    b_ref = rest.pop(0) if sigmoid_bias else None
    o_ref = rest.pop(0)
    hn_ref = rest.pop(0) if emit_normed else None
    rows = x_ref.shape[0] // dil

    if dil == 1:
        h = _rms(x_ref[...], g_ref[...]).astype(BF16)
    else:
        cols_ref = rest.pop(0)
        n_slab = cols_ref.shape[0]
        for c in range(n_slab):
            cols_ref[c] = x_ref[:, c * LANES:(c + 1) * LANES]
        h = jnp.concatenate(
            [_rms(jnp.concatenate([cols_ref[c, pl.ds(r, rows, stride=dil), :] for c in range(n_slab)], axis=-1),
                  g_ref[...]).astype(BF16) for r in range(dil)], axis=0)
    if emit_normed:
        hn_ref[...] = h

    acc = jnp.dot(h, w_ref[...], preferred_element_type=F32)
    if sigmoid_bias:
        acc = jax.nn.sigmoid(acc + b_ref[...])
    if dil == 1:
        o_ref[...] = acc.astype(o_ref.dtype)
    else:
        for r in range(dil):
            o_ref[0, r] = acc[r * rows:(r + 1) * rows].astype(o_ref.dtype)


def _norm_matmul(x2d, g, w, *, out_dtype, tm, tn, bias=None, dil=1, seq=None, emit_normed=False, name):
    T, D = x2d.shape
    N = w.shape[1]
    tm = min(tm, T if seq is None else seq)
    assert T % tm == 0 and N % tn == 0
    in_specs = [
        pl.BlockSpec((tm, D), lambda i, j: (i, 0)),
        pl.BlockSpec((1, D), lambda i, j: (0, 0)),
        pl.BlockSpec((D, tn), lambda i, j: (0, j)),
    ]
    args = [x2d, g.reshape(1, D).astype(F32), w.astype(BF16)]
    if bias is not None:
        in_specs.append(pl.BlockSpec((1, tn), lambda i, j: (0, j)))
        args.append(bias.reshape(1, N).astype(F32))
    assert not (emit_normed and dil > 1)
    scratch = []
    if dil == 1:
        out_spec = pl.BlockSpec((tm, tn), lambda i, j: (i, j))
        out_shape = jax.ShapeDtypeStruct((T, N), out_dtype)
    else:
        assert seq % tm == 0 and tm % (2 * SUBLANES * dil) == 0 and D % LANES == 0
        per_seq = seq // tm
        out_spec = pl.BlockSpec((1, dil, tm // dil, tn), lambda i, j: (i // per_seq, 0, i % per_seq, j))
        out_shape = jax.ShapeDtypeStruct((T // seq, dil, seq // dil, N), out_dtype)
        scratch.append(pltpu.VMEM((D // LANES, tm, LANES), F32))
    if emit_normed:
        out_spec = [out_spec, pl.BlockSpec((tm, D), lambda i, j: (i, 0))]
        out_shape = [out_shape, jax.ShapeDtypeStruct((T, D), BF16)]
    return pl.pallas_call(
        functools.partial(_norm_matmul_body, sigmoid_bias=bias is not None, dil=dil, emit_normed=emit_normed),
        grid=(T // tm, N // tn),
        in_specs=in_specs,
        out_specs=out_spec,
        out_shape=out_shape,
        scratch_shapes=scratch,
        compiler_params=_params("parallel", "arbitrary"),
        name=name,
    )(*args)


def _t5_bucket_np(n):
    max_exact = REL_BUCKETS // 2
    nf = np.maximum(n, 1).astype(np.float32)
    large = max_exact + (np.log(nf / np.float32(max_exact)) / np.float32(math.log(REL_MAX_DIST / max_exact))
                         * np.float32(REL_BUCKETS - max_exact)).astype(np.int32)
    large = np.minimum(large, REL_BUCKETS - 1)
    return np.where(n < max_exact, n, large).astype(np.int32)


def _dil_bucket_map(d):
    blk = DIL_BLOCK
    rel = (blk + np.arange(blk))[:, None] - np.arange(2 * blk)[None, :]
    return _t5_bucket_np(np.clip(rel, 0, None) * d)


def _moba_bucket_map():
    blk = MOBA_BLOCK
    rel = np.arange(blk)[:, None] - np.arange(blk)[None, :]
    tiles = [_t5_bucket_np(np.clip(rel + dlt * blk, 0, None)) for dlt in range(MOBA_BIAS_TILES)]
    tiles[0] = np.where(rel >= 0, tiles[0], -1)
    far = _t5_bucket_np(np.arange((MOBA_BIAS_TILES - 2) * blk + 1, 64 * blk))
    assert (far == REL_BUCKETS - 1).all()
    return np.stack(tiles)


def _bias_lookup_body(idx_ref, range_ref, tab_ref, o_ref):
    h, i = pl.program_id(0), pl.program_id(1)
    idx = idx_ref[...]

    def bucket(b, acc):
        return jnp.where(idx == b, tab_ref[b, h], acc)

    o_ref[0] = lax.fori_loop(range_ref[i, 0], range_ref[i, 1], bucket, jnp.full(idx.shape, NEG_INF, F32))


def _bias_lookup(bucket_map, table, name):
    R, C = bucket_map.shape
    H = table.shape[1]
    rt = min(R, 256)
    assert R % rt == 0
    tiles = bucket_map.reshape(R // rt, rt * C)
    ranges = np.stack([np.where(tiles >= 0, tiles, REL_BUCKETS).min(axis=1), tiles.max(axis=1) + 1], axis=1)
    return pl.pallas_call(
        _bias_lookup_body,
        grid=(H, R // rt),
        in_specs=[pl.BlockSpec((rt, C), lambda h, i: (i, 0)), pl.BlockSpec(memory_space=pltpu.SMEM),
                  pl.BlockSpec(memory_space=pltpu.SMEM)],
        out_specs=pl.BlockSpec((1, rt, C), lambda h, i: (h, i, 0)),
        out_shape=jax.ShapeDtypeStruct((H, R, C), F32),
        compiler_params=_params("parallel", "parallel"),
        name=name,
    )(jnp.asarray(bucket_map), jnp.asarray(ranges.astype(np.int32)), table.astype(F32))


def _dilated_body(q_ref, kp_ref, kc_ref, vp_ref, vc_ref, bias_ref, o_ref, *, qblocks):
    blk, hd = DIL_BLOCK, HEAD_DIM
    gw = DIL_HEADS * hd
    n = pl.program_id(2)
    k = jnp.concatenate([kp_ref[0, 0], kc_ref[0, 0]], axis=0)
    v = jnp.concatenate([vp_ref[0, 0], vc_ref[0, 0]], axis=0)
    lane = lax.broadcasted_iota(jnp.int32, (1, gw), 1)
    qi = lax.broadcasted_iota(jnp.int32, (blk, 2 * blk), 0)
    kj = lax.broadcasted_iota(jnp.int32, (blk, 2 * blk), 1)
    rel = blk + qi - kj
    band = (rel >= 0) & (rel <= blk)
    first_key = jnp.where(n > 0, 0, blk)
    for i in range(qblocks):
        rows = slice(i * blk, (i + 1) * blk)
        q = q_ref[0, 0, rows, :]
        kw = k[i * blk:(i + 2) * blk]
        vw = v[i * blk:(i + 2) * blk]
        mask = band & (kj >= first_key) if i == 0 else band
        head_lanes = [(lane >= h * hd) & (lane < (h + 1) * hd) for h in range(DIL_HEADS)]
        q_all = jnp.concatenate([jnp.where(hl, q, jnp.zeros_like(q)) for hl in head_lanes], axis=0)
        s = _nt_dot(q_all, kw) * (hd ** -0.5) + bias_ref[...]
        s = jnp.where(jnp.concatenate([mask] * DIL_HEADS, axis=0), s, NEG_INF)
        m = jnp.max(s, axis=-1, keepdims=True)
        p = jnp.exp(s - m)
        l = jnp.sum(p, axis=-1, keepdims=True)
        pv = jnp.dot(p.astype(BF16), vw, preferred_element_type=F32) / l
        lse_all = m + jnp.log(l)
        out = jnp.zeros((blk, gw), F32)
        lse = jnp.zeros((blk, gw), F32)
        for h, hl in enumerate(head_lanes):
            out = jnp.where(hl, pv[h * blk:(h + 1) * blk], out)
            lse = jnp.where(hl, lse_all[h * blk:(h + 1) * blk], lse)
        o_ref[0, 0, rows, :gw] = out
        o_ref[0, 0, rows, gw:] = lse


def _dilated_group(qkv, bias, name):
    B, d, L, _ = qkv.shape
    blk, gw = DIL_BLOCK, DIL_HEADS * HEAD_DIM
    assert L % blk == 0
    qblocks = min(4, L // blk)
    assert L % (qblocks * blk) == 0
    rows = qblocks * blk

    def spec(j, prev):
        if prev:
            return pl.BlockSpec((1, 1, blk, gw), lambda b, r, n: (b, r, jnp.maximum(n * qblocks - 1, 0), j))
        return pl.BlockSpec((1, 1, rows, gw), lambda b, r, n: (b, r, n, j))

    return pl.pallas_call(
        functools.partial(_dilated_body, qblocks=qblocks),
        grid=(B, d, L // rows),
        in_specs=[spec(0, False), spec(1, True), spec(1, False), spec(2, True), spec(2, False),
                  pl.BlockSpec((DIL_HEADS * blk, 2 * blk), lambda b, r, n: (0, 0))],
        out_specs=pl.BlockSpec((1, 1, rows, 2 * gw), lambda b, r, n: (b, r, n, 0)),
        out_shape=jax.ShapeDtypeStruct((B, d, L, 2 * gw), F32),
        compiler_params=_params("parallel", "parallel", "arbitrary"),
        name=name,
    )(qkv, qkv, qkv, qkv, qkv, bias.reshape(DIL_HEADS * blk, 2 * blk))


def _moba_body(q_ref, k_ref, v_ref, bias_ref, o_ref, qa_ref, ka_ref, *, S):
    blk, hd = MOBA_BLOCK, HEAD_DIM
    nblk = S // blk
    scale = hd ** -0.5
    lane = lax.broadcasted_iota(jnp.int32, (1, 2 * hd), 1)
    kmean = jnp.mean(k_ref[0].astype(F32).reshape(nblk, blk, 2 * hd), axis=1)
    grp = MOBA_GROUP * blk
    row_blk = lax.broadcasted_iota(jnp.int32, (S, 2 * hd), 0) // blk
    q2, k2 = q_ref[0], k_ref[0]

    ka_ref[:, :2 * hd] = k2
    ka_ref[:, 2 * hd:] = jnp.where(lane == row_blk, 1.0, 0.0).astype(BF16)
    blk_t = lax.broadcasted_iota(jnp.int32, (nblk, S), 0)
    q_blk_t = lax.broadcasted_iota(jnp.int32, (nblk, S), 1) // blk
    past = blk_t < q_blk_t
    blk_f = blk_t.astype(F32)
    for h in range(2):
        head_lanes = (lane >= h * hd) & (lane < (h + 1) * hd)
        qh = jnp.where(head_lanes, q2, jnp.zeros_like(q2))
        gs = _nt_dot(jnp.where(head_lanes, kmean, 0.0), qh.astype(F32), precision=lax.Precision.HIGHEST)
        gs = jnp.where(past, gs, NEG_INF)
        sel = jnp.zeros(gs.shape, F32)
        for _r in range(MOBA_TOPK):
            gmax = jnp.max(gs, axis=0, keepdims=True)
            first = jnp.min(jnp.where(gs == gmax, blk_f, float(nblk)), axis=0, keepdims=True)
            pick = blk_f == first
            sel = jnp.where(pick & past, 1.0, sel)
            gs = jnp.where(pick, -jnp.inf, gs)
        penalty_t = jnp.where((sel > 0.5) | (blk_t == q_blk_t), 0.0, MOBA_MASK)
        penalty_t = jnp.concatenate([penalty_t, jnp.zeros((2 * hd - nblk, S), F32)], axis=0)
        qa_ref[h, :, :2 * hd] = qh * scale
        qa_ref[h, :, 2 * hd:] = penalty_t.T.astype(BF16)

    def q_block(qb, _):
        row0 = pl.multiple_of(qb * blk, blk)
        n_grp = qb // MOBA_GROUP + 1
        qa = jnp.concatenate([qa_ref[h, pl.ds(row0, blk), :] for h in range(2)], axis=0)

        def key_group(i, carry):
            m, l, acc = carry
            g = n_grp - 1 - i
            col0 = pl.multiple_of(g * grp, grp)
            per = MOBA_GROUP // MOBA_SPLIT
            half = per * blk
            cols = [pl.ds(pl.multiple_of(col0 + i * half, half), half) for i in range(MOBA_SPLIT)]
            scores = [_nt_dot(qa, ka_ref[c, :]) for c in cols]
            for i in range(MOBA_SPLIT):
                parts = []
                for jj in range(per):
                    n = g * MOBA_GROUP + i * per + jj
                    tile = jnp.clip(qb - n, 0, MOBA_BIAS_TILES - 1)
                    bias = jnp.concatenate([bias_ref[0, h, tile] for h in range(2)], axis=0)
                    parts.append(scores[i][:, jj * blk:(jj + 1) * blk] + bias)
                m_new = jnp.maximum(m, jnp.max(functools.reduce(jnp.maximum, parts), axis=-1, keepdims=True))
                a = jnp.exp(m - m_new)
                ps = [jnp.exp(p - m_new) for p in parts]
                l = l * a + jnp.sum(functools.reduce(jnp.add, ps), axis=-1, keepdims=True)
                p_all = jnp.concatenate([p.astype(BF16) for p in ps], axis=-1)
                pv = jnp.dot(p_all, v_ref[0, cols[i], :], preferred_element_type=F32)
                m, acc = m_new, acc * a + pv
            return m, l, acc

        init = (jnp.full((2 * blk, 1), -jnp.inf, F32), jnp.zeros((2 * blk, 1), F32), jnp.zeros((2 * blk, 2 * hd), F32))
        _, l, acc = lax.fori_loop(0, n_grp, key_group, init)
        o = acc / l
        o_ref[0, pl.ds(row0, blk), :] = jnp.where(lane < hd, o[:blk], o[blk:]).astype(o_ref.dtype)
        return 0

    lax.fori_loop(0, nblk, q_block, 0)


def _moba(qkv, bias, first_col):
    B, S, C = qkv.shape
    pw = 2 * HEAD_DIM
    npair = MOBA_HEADS // 2
    assert S % (MOBA_GROUP * MOBA_BLOCK) == 0 and S // MOBA_BLOCK <= HEAD_DIM and first_col % pw == 0
    c0 = first_col // pw

    def spec(j):
        return pl.BlockSpec((1, S, pw), lambda hp, b: (b, 0, c0 + j * npair + hp))

    return pl.pallas_call(
        functools.partial(_moba_body, S=S),
        grid=(npair, B),
        in_specs=[spec(0), spec(1), spec(2),
                  pl.BlockSpec((1, 2, MOBA_BIAS_TILES, MOBA_BLOCK, MOBA_BLOCK), lambda hp, b: (hp, 0, 0, 0, 0))],
        out_specs=pl.BlockSpec((1, S, pw), lambda hp, b: (b, 0, hp)),
        out_shape=jax.ShapeDtypeStruct((B, S, MOBA_HEADS * HEAD_DIM), BF16),
        scratch_shapes=[pltpu.VMEM((2, S, 2 * pw), BF16), pltpu.VMEM((S, 2 * pw), BF16)],
        compiler_params=_params("parallel", "parallel"),
        name="moba",
    )(qkv, qkv, qkv, bias)


def _merge_body(d0_ref, d1_ref, d2_ref, om_ref, gate_ref, x_ref, wa_ref, wb_ref, wo_ref, x1_ref, t1_ref, t2_ref):
    D = x_ref.shape[-1]
    gw = DIL_HEADS * HEAD_DIM
    n_slab = t1_ref.shape[0]
    for src, dst in ((d1_ref, t1_ref), (d2_ref, t2_ref)):
        dil, rows = src.shape[1], src.shape[2]
        for r in range(dil):
            for c in range(n_slab):
                dst[c, pl.ds(r, rows, stride=dil), :] = src[0, r, :, c * LANES:(c + 1) * LANES]
    g0 = d0_ref[...]
    g1 = jnp.concatenate([t1_ref[c] for c in range(n_slab)], axis=-1)
    g2 = jnp.concatenate([t2_ref[c] for c in range(n_slab)], axis=-1)
    l0, l1, l2 = g0[:, gw:], g1[:, gw:], g2[:, gw:]
    mx = jnp.maximum(jnp.maximum(l0, l1), l2)
    e0, e1, e2 = jnp.exp(l0 - mx), jnp.exp(l1 - mx), jnp.exp(l2 - mx)
    o_dil = (e0 * g0[:, :gw] + e1 * g1[:, :gw] + e2 * g2[:, :gw]) / (e0 + e1 + e2)
    a = jnp.dot(o_dil.astype(BF16), wa_ref[...], preferred_element_type=F32)
    b = jnp.dot(om_ref[...], wb_ref[...], preferred_element_type=F32)
    merged = gate_ref[:, :D].astype(F32) * a + gate_ref[:, D:].astype(F32) * b
    x1_ref[...] = x_ref[...] + jnp.dot(merged.astype(BF16), wo_ref[...], preferred_element_type=F32)


def _merge(dil_out, o_moba, gates, x2d, w_a, w_b, w_o, seq, tm):
    T, D = x2d.shape
    tm = min(tm, seq)
    per_seq = seq // tm
    assert seq % tm == 0 and dil_out[0].shape[1] == 1

    def rows(width):
        return pl.BlockSpec((tm, width), lambda i: (i, 0))

    def whole(w):
        return pl.BlockSpec(w.shape, lambda i: (0, 0))

    def residue_major(a):
        d, width = a.shape[1], a.shape[3]
        assert tm % (8 * d) == 0
        return pl.BlockSpec((1, d, tm // d, width), lambda i: (i // per_seq, 0, i % per_seq, 0))

    ws = [w_a.astype(BF16), w_b.astype(BF16), w_o.astype(BF16)]
    width = dil_out[0].shape[-1]
    return pl.pallas_call(
        _merge_body,
        grid=(T // tm,),
        in_specs=[rows(width), residue_major(dil_out[1]), residue_major(dil_out[2]),
                  rows(o_moba.shape[-1]), rows(2 * D), rows(D)] + [whole(w) for w in ws],
        out_specs=rows(D),
        out_shape=jax.ShapeDtypeStruct((T, D), F32),
        scratch_shapes=[pltpu.VMEM((width // LANES, tm, LANES), F32)] * 2,
        compiler_params=_params("parallel"),
        name="merge",
    )(dil_out[0].reshape(T, width), dil_out[1], dil_out[2], o_moba, gates, x2d, *ws)


def _cross_body(q_ref, kv_ref, x_ref, wo_ref, x2_ref):
    hd = CROSS_HEAD_DIM
    width = CROSS_HEADS * hd
    heads = []
    for h in range(CROSS_HEADS):
        sl = slice(h * hd, (h + 1) * hd)
        s = _nt_dot(q_ref[0, :, sl], kv_ref[0, :, sl]) * (hd ** -0.5)
        m = jnp.max(s, axis=-1, keepdims=True)
        p = jnp.exp(s - m)
        l = jnp.sum(p, axis=-1, keepdims=True)
        v = kv_ref[0, :, width + h * hd: width + (h + 1) * hd]
        heads.append((jnp.dot(p.astype(BF16), v, preferred_element_type=F32) / l).astype(BF16))
    o = jnp.concatenate(heads, axis=-1)
    x2_ref[0] = x_ref[0] + jnp.dot(o, wo_ref[...], preferred_element_type=F32)


def _cross(q, kv, x, w_o, tm):
    B, S, D = x.shape
    M = kv.shape[1]
    tm = min(tm, S)
    w_o = w_o.astype(BF16)
    return pl.pallas_call(
        _cross_body,
        grid=(B, S // tm),
        in_specs=[pl.BlockSpec((1, tm, q.shape[-1]), lambda b, i: (b, i, 0)),
                  pl.BlockSpec((1, M, kv.shape[-1]), lambda b, i: (b, 0, 0)),
                  pl.BlockSpec((1, tm, D), lambda b, i: (b, i, 0)),
                  pl.BlockSpec(w_o.shape, lambda b, i: (0, 0))],
        out_specs=pl.BlockSpec((1, tm, D), lambda b, i: (b, i, 0)),
        out_shape=jax.ShapeDtypeStruct((B, S, D), F32),
        compiler_params=_params("parallel", "parallel"),
        name="cross",
    )(q, kv, x, w_o)


def _merge_exchange_pairs(n):
    pairs = []
    p = 1
    while p < n:
        k = p
        while k >= 1:
            for j in range(k % p, n - k, 2 * k):
                for i in range(min(k, n - j - k)):
                    if (i + j) // (2 * p) == (i + j + k) // (2 * p):
                        pairs.append((i + j, i + j + k))
            k //= 2
        p *= 2
    return pairs


def _top16(s):
    n, sub = PEER_TOPK, SUBLANES
    assert s.shape[0] % sub == 0 and sub * n >= s.shape[0] >= 2 * sub
    v = [s[k * sub:(k + 1) * sub] for k in range(s.shape[0] // sub)] + [None] * (n - s.shape[0] // sub)

    def exchange(i, j):
        if v[j] is None:
            return
        if v[i] is None:
            v[i], v[j] = v[j], None
        else:
            v[i], v[j] = jnp.maximum(v[i], v[j]), jnp.minimum(v[i], v[j])

    for i, j in _merge_exchange_pairs(n):
        exchange(i, j)
    for shift in (sub // 2, sub // 4, sub // 8):
        w = [None if x is None else pltpu.roll(x, shift, 0) for x in v]
        for k in range(n):
            other = w[n - 1 - k]
            if other is not None:
                v[k] = other if v[k] is None else jnp.maximum(v[k], other)
        d = n // 2
        while d >= 1:
            for i in range(n):
                if i & d == 0:
                    exchange(i, i + d)
            d //= 2
    return v


def _top_sorted(s):
    v = _top16(s)
    nxt = jnp.max(jnp.where(s < v[-1][0:1], s, -jnp.inf), axis=0, keepdims=True)
    return jnp.concatenate([x[0:1] for x in v] + [nxt], axis=0)


def _pair_sum_candidates(a, b):
    assert PEER_TOPK == 16
    row = lax.broadcasted_iota(jnp.int32, (PEER_TOPK, 1), 0)
    blocks = [a[0:1] + b]
    blocks += [a[i:i + 1] + b[0:8] for i in (1, 2, 3)]
    blocks += [jnp.where(row >= 4, b[0:1] + a, -jnp.inf)]
    blocks += [jnp.where(row[0:8] >= 4, b[j:j + 1] + a[0:8], -jnp.inf) for j in (1, 2)]
    return jnp.concatenate(blocks, axis=0)


def _peer_route_body(q_ref, keys_ref, thr_ref, e1_ref, s2_ref, e2_ref):
    nk, K = PEER_N_KEYS, PEER_TOPK
    kd = keys_ref.shape[-1]
    for h in range(PEER_HEADS):
        sc = []
        for p in range(2):
            c0 = (2 * h + p) * kd
            sc.append(_nt_dot(keys_ref[p], q_ref[:, c0:c0 + kd], precision=lax.Precision.HIGHEST))
        a = _top_sorted(sc[0])
        b = _top_sorted(sc[1])
        cand = _pair_sum_candidates(a[:K], b[:K])
        tau = _top16(cand)[K - 1][0:1]
        below = jnp.max(jnp.where(cand < tau, cand, -jnp.inf), axis=0, keepdims=True)
        for corner in (a[0:1] + b[K:K + 1], a[K:K + 1] + b[0:1]):
            below = jnp.maximum(below, jnp.where(corner < tau, corner, -jnp.inf))
        below = jnp.where(below == -jnp.inf, tau, below)
        cut = 0.5 * (tau + below)
        top = a[0:1] + b[0:1]
        z = jnp.sum(jnp.where(cand >= tau, jnp.exp(cand - top), 0.0), axis=0, keepdims=True)
        thr = cut - sc[0]
        e1 = jnp.exp(sc[0] - a[0:1]) * (0.5 / z)
        for c in range(thr_ref.shape[1]):
            thr_ref[h, c] = thr[:, c * LANES:(c + 1) * LANES]
            e1_ref[h, c] = e1[:, c * LANES:(c + 1) * LANES]
        s2_ref[h] = sc[1]
        e2_ref[h] = jnp.exp(sc[1] - b[0:1])


def _peer_route(q, sub_keys, tt):
    T = q.shape[0]
    tt = min(tt, T)
    nk = PEER_N_KEYS
    assert tt % LANES == 0
    big = jax.ShapeDtypeStruct((PEER_HEADS, nk, T), F32)
    big_spec = pl.BlockSpec((PEER_HEADS, nk, tt), lambda i: (0, 0, i))
    slab = jax.ShapeDtypeStruct((PEER_HEADS, T // LANES, nk, LANES), F32)
    slab_spec = pl.BlockSpec((PEER_HEADS, tt // LANES, nk, LANES), lambda i: (0, i, 0, 0))
    return pl.pallas_call(
        _peer_route_body,
        grid=(T // tt,),
        in_specs=[pl.BlockSpec((tt, q.shape[1]), lambda i: (i, 0)),
                  pl.BlockSpec(sub_keys.shape, lambda i: (0, 0, 0))],
        out_specs=[slab_spec, slab_spec, big_spec, big_spec],
        out_shape=[slab, slab, big, big],
        compiler_params=_params("parallel"),
        name="peer_route",
    )(q, sub_keys.astype(F32))


def _peer_main_body(h_ref, u_ref, vt_ref, thr_ref, e1_ref, s2_ref, e2_ref, x_ref, g_ref,
                    o_ref, acc_ref, gate_ref, ga_ref, ht_ref, *, rows_per_step):
    nk = PEER_N_KEYS
    lanes = LANES
    j = pl.program_id(1)
    tt = h_ref.shape[0]

    @pl.when(j == 0)
    def _():
        acc_ref[...] = jnp.zeros_like(acc_ref)
        ht_ref[...] = h_ref[...].astype(F32).T.astype(ht_ref.dtype)

    i1_base = pl.multiple_of(j * rows_per_step, rows_per_step)
    group = 4
    sub = 64
    n_sub = nk // sub

    def gate_tile(ti, carry):
        cs = pl.ds(pl.multiple_of((ti // n_sub) * lanes, lanes), lanes)
        k0 = pl.multiple_of((ti % n_sub) * sub, sub)
        for r0 in range(0, rows_per_step, group):
            gates = [None] * group
            for h in range(PEER_HEADS):
                s2 = s2_ref[h, pl.ds(k0, sub), cs]
                e2 = e2_ref[h, pl.ds(k0, sub), cs]
                for g in range(group):
                    r = r0 + g
                    row = pl.ds(i1_base + r, SUBLANES, stride=0)
                    thr = thr_ref[h, ti // n_sub, row, :]
                    e1 = e1_ref[h, ti // n_sub, row, :]
                    tiles = (sub // SUBLANES, SUBLANES, lanes)
                    w = (jnp.where(s2.reshape(tiles) >= thr, e2.reshape(tiles), 0.0) * e1).reshape(sub, lanes)
                    gates[g] = w if gates[g] is None else gates[g] + w
            for g in range(group):
                rs = pl.ds(pl.multiple_of((r0 + g) * nk + k0, sub), sub)
                gate_ref[rs, cs] = gates[g]
        return carry

    lax.fori_loop(0, (tt // lanes) * n_sub, gate_tile, 0, unroll=4)
    pre = jnp.dot(u_ref[...], ht_ref[...], preferred_element_type=F32)
    ga_ref[...] = (gate_ref[...] * (pre * (1.0 + lax.erf(pre * (2.0 ** -0.5))))).astype(ga_ref.dtype)
    acc_ref[...] += jnp.dot(vt_ref[0], ga_ref[...], preferred_element_type=F32)

    @pl.when(j == pl.num_programs(1) - 1)
    def _():
        o_ref[...] = _rms(x_ref[...] + acc_ref[...].T, g_ref[...])


def _peer_main(hn, u, v, thr, e1, s2, e2, x2d, g_final, tt, rows_per_step):
    T, D = x2d.shape
    tt = min(tt, T)
    nk = PEER_N_KEYS
    eb = rows_per_step * nk
    n_exp = u.shape[0]
    assert n_exp == nk * nk and nk % rows_per_step == 0
    u = u.astype(BF16)
    vt = v.astype(BF16).reshape(n_exp // eb, eb, D).transpose(0, 2, 1)
    score_spec = pl.BlockSpec((PEER_HEADS, nk, tt), lambda i, j: (0, 0, i))
    slab_spec = pl.BlockSpec((PEER_HEADS, tt // LANES, nk, LANES), lambda i, j: (0, i, 0, 0))
    return pl.pallas_call(
        functools.partial(_peer_main_body, rows_per_step=rows_per_step),
        grid=(T // tt, n_exp // eb),
        in_specs=[pl.BlockSpec((tt, D), lambda i, j: (i, 0)),
                  pl.BlockSpec((eb, D), lambda i, j: (j, 0)),
                  pl.BlockSpec((1, D, eb), lambda i, j: (j, 0, 0)),
                  slab_spec, slab_spec, score_spec, score_spec,
                  pl.BlockSpec((tt, D), lambda i, j: (i, 0)),
                  pl.BlockSpec((1, D), lambda i, j: (0, 0))],
        out_specs=pl.BlockSpec((tt, D), lambda i, j: (i, 0)),
        out_shape=jax.ShapeDtypeStruct((T, D), F32),
        scratch_shapes=[pltpu.VMEM((D, tt), F32), pltpu.VMEM((eb, tt), F32), pltpu.VMEM((eb, tt), BF16),
                        pltpu.VMEM((D, tt), BF16)],
        compiler_params=_params("parallel", "arbitrary"),
        name="peer_main",
    )(hn, u, vt, thr, e1, s2, e2, x2d, g_final.reshape(1, D).astype(F32))


def kernel(x, mem, rel_bias, g_mix, w_in, b_gate, w_branch_a, w_branch_b, w_out, g_cross, g_mem, w_q_cross,
           w_kv_cross, w_o_cross, g_ffn, w_q_peer, peer_sub_keys, peer_u, peer_v, g_final):
    B, S, D = x.shape
    T = B * S
    depth = w_in.shape[0]
    ndil = len(DIL_GROUPS) * DIL_HEADS
    x2d = x.reshape(T, D)

    dil_bias = [_bias_lookup(_dil_bucket_map(d), rel_bias[:, gi * DIL_HEADS:(gi + 1) * DIL_HEADS], f"dil_bias_g{gi}")
                for gi, (_, d) in enumerate(DIL_GROUPS)]
    mmap = _moba_bucket_map()
    moba_bias = _bias_lookup(mmap.reshape(-1, MOBA_BLOCK), rel_bias[:, ndil:], "moba_bias")
    moba_bias = moba_bias.reshape(MOBA_HEADS // 2, 2, MOBA_BIAS_TILES, MOBA_BLOCK, MOBA_BLOCK)

    assert depth == 1, "the final norm is fused into the last PEER sweep; one layer supported"
    l = 0
    gc = DIL_GROUP_COLS
    assert DIL_GROUPS[0][1] == 1
    w_tok = jnp.concatenate([w_in[l][:, :gc], w_in[l][:, DIL_WIDTH:QKV_WIDTH]], axis=1)
    tok = _norm_matmul(x2d, g_mix[l], w_tok, out_dtype=BF16, tm=1024, tn=w_tok.shape[1], name="proj_tok")
    dil_qkv = [tok.reshape(B, 1, S, -1)]
    for gi, (_, d) in enumerate(DIL_GROUPS[1:], start=1):
        dil_qkv.append(_norm_matmul(x2d, g_mix[l], w_in[l][:, gi * gc:(gi + 1) * gc], out_dtype=BF16, tm=1024, tn=gc,
                                    dil=d, seq=S, name=f"proj_dil_g{gi}"))
    gates = _norm_matmul(x2d, g_mix[l], w_in[l][:, QKV_WIDTH:], out_dtype=BF16, tm=1024, tn=2 * D,
                         bias=b_gate[l], name="proj_gates")
    dil_out = [_dilated_group(a, dil_bias[gi], f"dilated_g{gi}") for gi, a in enumerate(dil_qkv)]
    o_moba = _moba(tok.reshape(B, S, -1), moba_bias, first_col=gc).reshape(T, MOBA_HEADS * HEAD_DIM)
    x2d = _merge(dil_out, o_moba, gates, x2d, w_branch_a[l], w_branch_b[l], w_out[l], seq=S, tm=512)
    M = mem.shape[1]
    q_c = _norm_matmul(x2d, g_cross[l], w_q_cross[l], out_dtype=BF16, tm=1024, tn=512, name="cross_q")
    kv = _norm_matmul(mem.reshape(B * M, D), g_mem[l], w_kv_cross[l], out_dtype=BF16, tm=1024, tn=1024, name="cross_kv")
    x2d = _cross(q_c.reshape(B, S, -1), kv.reshape(B, M, -1), x2d.reshape(B, S, D), w_o_cross[l], tm=1024).reshape(T, D)
    q_p, hn = _norm_matmul(x2d, g_ffn[l], w_q_peer[l], out_dtype=F32, tm=512, tn=w_q_peer.shape[-1], emit_normed=True,
                           name="peer_q")
    thr, e1, s2, e2 = _peer_route(q_p, peer_sub_keys[l], tt=512)
    y = _peer_main(hn, peer_u[l], peer_v[l], thr, e1, s2, e2, x2d, g_final,
                   tt=512, rows_per_step=8)
    return y.reshape(B, S, D)
```

```python
import functools
import math

import jax
import jax.numpy as jnp
import numpy as np
from jax import lax
from jax.experimental import pallas as pl
from jax.experimental.pallas import tpu as pltpu

HEAD_DIM = 64
DIL_GROUPS = ((128, 1), (512, 4), (2048, 16))
DIL_HEADS = 4
DIL_BLOCK = 128
MOBA_HEADS = 8
MOBA_BLOCK = 256
MOBA_TOPK = 3
MOBA_BIAS_TILES = 8
MOBA_GROUP = 4
MOBA_SPLIT = 2
MOBA_MASK = -(2.0 ** 100)
REL_BUCKETS = 32
REL_MAX_DIST = 2048
CROSS_HEADS = 4
CROSS_HEAD_DIM = 128
PEER_HEADS = 8
PEER_N_KEYS = 128
PEER_TOPK = 16
RMS_EPS = 1e-6
NEG_INF = -1e30
DIL_GROUP_COLS = 3 * DIL_HEADS * HEAD_DIM
DIL_WIDTH = len(DIL_GROUPS) * DIL_GROUP_COLS
MOBA_WIDTH = 3 * MOBA_HEADS * HEAD_DIM
QKV_WIDTH = DIL_WIDTH + MOBA_WIDTH

LANES = 128
SUBLANES = 8
V7X_VMEM_BYTES = 64 * 1024 * 1024
VMEM_LIMIT = V7X_VMEM_BYTES * 3 // 4

BF16 = jnp.bfloat16
F32 = jnp.float32


def _params(*sem):
    return pltpu.CompilerParams(dimension_semantics=sem, vmem_limit_bytes=VMEM_LIMIT)


def _rms(x, g):
    return x * lax.rsqrt(jnp.mean(x * x, axis=-1, keepdims=True) + RMS_EPS) * g


def _nt_dot(a, b, precision=None):
    return lax.dot_general(a, b, (((1,), (1,)), ((), ())), preferred_element_type=F32, precision=precision)


def _norm_matmul_body(x_ref, g_ref, w_ref, *rest, sigmoid_bias, dil, emit_normed):
    rest = list(rest)
    b_ref = rest.pop(0) if sigmoid_bias else None
    o_ref = rest.pop(0)
    hn_ref = rest.pop(0) if emit_normed else None
    rows = x_ref.shape[0] // dil

    if dil == 1:
        h = _rms(x_ref[...], g_ref[...]).astype(BF16)
    else:
        cols_ref = rest.pop(0)
        n_slab = cols_ref.shape[0]
        for c in range(n_slab):
            cols_ref[c] = x_ref[:, c * LANES:(c + 1) * LANES]
        h = jnp.concatenate(
            [_rms(jnp.concatenate([cols_ref[c, pl.ds(r, rows, stride=dil), :] for c in range(n_slab)], axis=-1),
                  g_ref[...]).astype(BF16) for r in range(dil)], axis=0)
    if emit_normed:
        hn_ref[...] = h

    acc = jnp.dot(h, w_ref[...], preferred_element_type=F32)
    if sigmoid_bias:
        acc = jax.nn.sigmoid(acc + b_ref[...])
    if dil == 1:
        o_ref[...] = acc.astype(o_ref.dtype)
    else:
        for r in range(dil):
            o_ref[0, r] = acc[r * rows:(r + 1) * rows].astype(o_ref.dtype)


def _norm_matmul(x2d, g, w, *, out_dtype, tm, tn, bias=None, dil=1, seq=None, emit_normed=False, name):
    T, D = x2d.shape
    N = w.shape[1]
    tm = min(tm, T if seq is None else seq)
    assert T % tm == 0 and N % tn == 0
    in_specs = [
        pl.BlockSpec((tm, D), lambda i, j: (i, 0)),
        pl.BlockSpec((1, D), lambda i, j: (0, 0)),
        pl.BlockSpec((D, tn), lambda i, j: (0, j)),
    ]
    args = [x2d, g.reshape(1, D).astype(F32), w.astype(BF16)]
    if bias is not None:
        in_specs.append(pl.BlockSpec((1, tn), lambda i, j: (0, j)))
        args.append(bias.reshape(1, N).astype(F32))
    assert not (emit_normed and dil > 1)
    scratch = []
    if dil == 1:
        out_spec = pl.BlockSpec((tm, tn), lambda i, j: (i, j))
        out_shape = jax.ShapeDtypeStruct((T, N), out_dtype)
    else:
        assert seq % tm == 0 and tm % (2 * SUBLANES * dil) == 0 and D % LANES == 0
        per_seq = seq // tm
        out_spec = pl.BlockSpec((1, dil, tm // dil, tn), lambda i, j: (i // per_seq, 0, i % per_seq, j))
        out_shape = jax.ShapeDtypeStruct((T // seq, dil, seq // dil, N), out_dtype)
        scratch.append(pltpu.VMEM((D // LANES, tm, LANES), F32))
    if emit_normed:
        out_spec = [out_spec, pl.BlockSpec((tm, D), lambda i, j: (i, 0))]
        out_shape = [out_shape, jax.ShapeDtypeStruct((T, D), BF16)]
    return pl.pallas_call(
        functools.partial(_norm_matmul_body, sigmoid_bias=bias is not None, dil=dil, emit_normed=emit_normed),
        grid=(T // tm, N // tn),
        in_specs=in_specs,
        out_specs=out_spec,
        out_shape=out_shape,
        scratch_shapes=scratch,
        compiler_params=_params("parallel", "arbitrary"),
        name=name,
    )(*args)


def _t5_bucket_np(n):
    max_exact = REL_BUCKETS // 2
    nf = np.maximum(n, 1).astype(np.float32)
    large = max_exact + (np.log(nf / np.float32(max_exact)) / np.float32(math.log(REL_MAX_DIST / max_exact))
                         * np.float32(REL_BUCKETS - max_exact)).astype(np.int32)
    large = np.minimum(large, REL_BUCKETS - 1)
    return np.where(n < max_exact, n, large).astype(np.int32)


def _dil_bucket_map(d):
    blk = DIL_BLOCK
    rel = (blk + np.arange(blk))[:, None] - np.arange(2 * blk)[None, :]
    return _t5_bucket_np(np.clip(rel, 0, None) * d)


def _moba_bucket_map():
    blk = MOBA_BLOCK
    rel = np.arange(blk)[:, None] - np.arange(blk)[None, :]
    tiles = [_t5_bucket_np(np.clip(rel + dlt * blk, 0, None)) for dlt in range(MOBA_BIAS_TILES)]
    tiles[0] = np.where(rel >= 0, tiles[0], -1)
    far = _t5_bucket_np(np.arange((MOBA_BIAS_TILES - 2) * blk + 1, 64 * blk))
    assert (far == REL_BUCKETS - 1).all()
    return np.stack(tiles)


def _bias_lookup_body(idx_ref, range_ref, tab_ref, o_ref):
    h, i = pl.program_id(0), pl.program_id(1)
    idx = idx_ref[...]

    def bucket(b, acc):
        return jnp.where(idx == b, tab_ref[b, h], acc)

    o_ref[0] = lax.fori_loop(range_ref[i, 0], range_ref[i, 1], bucket, jnp.full(idx.shape, NEG_INF, F32))


def _bias_lookup(bucket_map, table, name):
    R, C = bucket_map.shape
    H = table.shape[1]
    rt = min(R, 256)
    assert R % rt == 0
    tiles = bucket_map.reshape(R // rt, rt * C)
    ranges = np.stack([np.where(tiles >= 0, tiles, REL_BUCKETS).min(axis=1), tiles.max(axis=1) + 1], axis=1)
    return pl.pallas_call(
        _bias_lookup_body,
        grid=(H, R // rt),
        in_specs=[pl.BlockSpec((rt, C), lambda h, i: (i, 0)), pl.BlockSpec(memory_space=pltpu.SMEM),
                  pl.BlockSpec(memory_space=pltpu.SMEM)],
        out_specs=pl.BlockSpec((1, rt, C), lambda h, i: (h, i, 0)),
        out_shape=jax.ShapeDtypeStruct((H, R, C), F32),
        compiler_params=_params("parallel", "parallel"),
        name=name,
    )(jnp.asarray(bucket_map), jnp.asarray(ranges.astype(np.int32)), table.astype(F32))


def _dilated_body(q_ref, kp_ref, kc_ref, vp_ref, vc_ref, bias_ref, o_ref, *, qblocks):
    blk, hd = DIL_BLOCK, HEAD_DIM
    gw = DIL_HEADS * hd
    n = pl.program_id(2)
    k = jnp.concatenate([kp_ref[0, 0], kc_ref[0, 0]], axis=0)
    v = jnp.concatenate([vp_ref[0, 0], vc_ref[0, 0]], axis=0)
    lane = lax.broadcasted_iota(jnp.int32, (1, gw), 1)
    qi = lax.broadcasted_iota(jnp.int32, (blk, 2 * blk), 0)
    kj = lax.broadcasted_iota(jnp.int32, (blk, 2 * blk), 1)
    rel = blk + qi - kj
    band = (rel >= 0) & (rel <= blk)
    first_key = jnp.where(n > 0, 0, blk)
    for i in range(qblocks):
        rows = slice(i * blk, (i + 1) * blk)
        q = q_ref[0, 0, rows, :]
        kw = k[i * blk:(i + 2) * blk]
        vw = v[i * blk:(i + 2) * blk]
        mask = band & (kj >= first_key) if i == 0 else band
        head_lanes = [(lane >= h * hd) & (lane < (h + 1) * hd) for h in range(DIL_HEADS)]
        q_all = jnp.concatenate([jnp.where(hl, q, jnp.zeros_like(q)) for hl in head_lanes], axis=0)
        s = _nt_dot(q_all, kw) * (hd ** -0.5) + bias_ref[...]
        s = jnp.where(jnp.concatenate([mask] * DIL_HEADS, axis=0), s, NEG_INF)
        m = jnp.max(s, axis=-1, keepdims=True)
        p = jnp.exp(s - m)
        l = jnp.sum(p, axis=-1, keepdims=True)
        pv = jnp.dot(p.astype(BF16), vw, preferred_element_type=F32) / l
        lse_all = m + jnp.log(l)
        out = jnp.zeros((blk, gw), F32)
        lse = jnp.zeros((blk, gw), F32)
        for h, hl in enumerate(head_lanes):
            out = jnp.where(hl, pv[h * blk:(h + 1) * blk], out)
            lse = jnp.where(hl, lse_all[h * blk:(h + 1) * blk], lse)
        o_ref[0, 0, rows, :gw] = out
        o_ref[0, 0, rows, gw:] = lse


def _dilated_group(qkv, bias, name):
    B, d, L, _ = qkv.shape
    blk, gw = DIL_BLOCK, DIL_HEADS * HEAD_DIM
    assert L % blk == 0
    qblocks = min(4, L // blk)
    assert L % (qblocks * blk) == 0
    rows = qblocks * blk

    def spec(j, prev):
        if prev:
            return pl.BlockSpec((1, 1, blk, gw), lambda b, r, n: (b, r, jnp.maximum(n * qblocks - 1, 0), j))
        return pl.BlockSpec((1, 1, rows, gw), lambda b, r, n: (b, r, n, j))

    return pl.pallas_call(
        functools.partial(_dilated_body, qblocks=qblocks),
        grid=(B, d, L // rows),
        in_specs=[spec(0, False), spec(1, True), spec(1, False), spec(2, True), spec(2, False),
                  pl.BlockSpec((DIL_HEADS * blk, 2 * blk), lambda b, r, n: (0, 0))],
        out_specs=pl.BlockSpec((1, 1, rows, 2 * gw), lambda b, r, n: (b, r, n, 0)),
        out_shape=jax.ShapeDtypeStruct((B, d, L, 2 * gw), F32),
        compiler_params=_params("parallel", "parallel", "arbitrary"),
        name=name,
    )(qkv, qkv, qkv, qkv, qkv, bias.reshape(DIL_HEADS * blk, 2 * blk))


def _moba_body(q_ref, k_ref, v_ref, bias_ref, o_ref, qa_ref, ka_ref, *, S):
    blk, hd = MOBA_BLOCK, HEAD_DIM
    nblk = S // blk
    scale = hd ** -0.5
    lane = lax.broadcasted_iota(jnp.int32, (1, 2 * hd), 1)
    kmean = jnp.mean(k_ref[0].astype(F32).reshape(nblk, blk, 2 * hd), axis=1)
    grp = MOBA_GROUP * blk
    row_blk = lax.broadcasted_iota(jnp.int32, (S, 2 * hd), 0) // blk
    q2, k2 = q_ref[0], k_ref[0]

    ka_ref[:, :2 * hd] = k2
    ka_ref[:, 2 * hd:] = jnp.where(lane == row_blk, 1.0, 0.0).astype(BF16)
    blk_t = lax.broadcasted_iota(jnp.int32, (nblk, S), 0)
    q_blk_t = lax.broadcasted_iota(jnp.int32, (nblk, S), 1) // blk
    past = blk_t < q_blk_t
    blk_f = blk_t.astype(F32)
    for h in range(2):
        head_lanes = (lane >= h * hd) & (lane < (h + 1) * hd)
        qh = jnp.where(head_lanes, q2, jnp.zeros_like(q2))
        gs = _nt_dot(jnp.where(head_lanes, kmean, 0.0), qh.astype(F32), precision=lax.Precision.HIGHEST)
        gs = jnp.where(past, gs, NEG_INF)
        sel = jnp.zeros(gs.shape, F32)
        for _r in range(MOBA_TOPK):
            gmax = jnp.max(gs, axis=0, keepdims=True)
            first = jnp.min(jnp.where(gs == gmax, blk_f, float(nblk)), axis=0, keepdims=True)
            pick = blk_f == first
            sel = jnp.where(pick & past, 1.0, sel)
            gs = jnp.where(pick, -jnp.inf, gs)
        penalty_t = jnp.where((sel > 0.5) | (blk_t == q_blk_t), 0.0, MOBA_MASK)
        penalty_t = jnp.concatenate([penalty_t, jnp.zeros((2 * hd - nblk, S), F32)], axis=0)
        qa_ref[h, :, :2 * hd] = qh * scale
        qa_ref[h, :, 2 * hd:] = penalty_t.T.astype(BF16)

    def q_block(qb, _):
        row0 = pl.multiple_of(qb * blk, blk)
        n_grp = qb // MOBA_GROUP + 1
        qa = jnp.concatenate([qa_ref[h, pl.ds(row0, blk), :] for h in range(2)], axis=0)

        def key_group(i, carry):
            m, l, acc = carry
            g = n_grp - 1 - i
            col0 = pl.multiple_of(g * grp, grp)
            per = MOBA_GROUP // MOBA_SPLIT
            half = per * blk
            cols = [pl.ds(pl.multiple_of(col0 + i * half, half), half) for i in range(MOBA_SPLIT)]
            scores = [_nt_dot(qa, ka_ref[c, :]) for c in cols]
            state = [list(m), list(l), list(acc)]
            for i in range(MOBA_SPLIT):
                for h in range(2):
                    rows = slice(h * blk, (h + 1) * blk)
                    parts = []
                    for jj in range(per):
                        n = g * MOBA_GROUP + i * per + jj
                        tile = jnp.clip(qb - n, 0, MOBA_BIAS_TILES - 1)
                        parts.append(scores[i][rows, jj * blk:(jj + 1) * blk] + bias_ref[0, h, tile])
                    m_old = state[0][h]
                    m_new = jnp.maximum(m_old, jnp.max(functools.reduce(jnp.maximum, parts), axis=-1, keepdims=True))
                    a = jnp.exp(m_old - m_new)
                    ps = [jnp.exp(p - m_new) for p in parts]
                    state[1][h] = state[1][h] * a + jnp.sum(functools.reduce(jnp.add, ps), axis=-1, keepdims=True)
                    p_all = jnp.concatenate([p.astype(BF16) for p in ps], axis=-1)
                    pv = jnp.dot(p_all, v_ref[0, cols[i], :], preferred_element_type=F32)
                    state[0][h], state[2][h] = m_new, state[2][h] * a + pv
            return tuple(tuple(x) for x in state)

        def per_head(x):
            return (x, x)

        init = (per_head(jnp.full((blk, 1), -jnp.inf, F32)), per_head(jnp.zeros((blk, 1), F32)),
                per_head(jnp.zeros((blk, 2 * hd), F32)))
        _, l, acc = lax.fori_loop(0, n_grp, key_group, init)
        o_ref[0, pl.ds(row0, blk), :] = jnp.where(lane < hd, acc[0] / l[0], acc[1] / l[1]).astype(o_ref.dtype)
        return 0

    lax.fori_loop(0, nblk, q_block, 0)


def _moba(qkv, bias, first_col):
    B, S, C = qkv.shape
    pw = 2 * HEAD_DIM
    npair = MOBA_HEADS // 2
    assert S % (MOBA_GROUP * MOBA_BLOCK) == 0 and S // MOBA_BLOCK <= HEAD_DIM and first_col % pw == 0
    c0 = first_col // pw

    def spec(j):
        return pl.BlockSpec((1, S, pw), lambda hp, b: (b, 0, c0 + j * npair + hp))

    return pl.pallas_call(
        functools.partial(_moba_body, S=S),
        grid=(npair, B),
        in_specs=[spec(0), spec(1), spec(2),
                  pl.BlockSpec((1, 2, MOBA_BIAS_TILES, MOBA_BLOCK, MOBA_BLOCK), lambda hp, b: (hp, 0, 0, 0, 0))],
        out_specs=pl.BlockSpec((1, S, pw), lambda hp, b: (b, 0, hp)),
        out_shape=jax.ShapeDtypeStruct((B, S, MOBA_HEADS * HEAD_DIM), BF16),
        scratch_shapes=[pltpu.VMEM((2, S, 2 * pw), BF16), pltpu.VMEM((S, 2 * pw), BF16)],
        compiler_params=_params("parallel", "parallel"),
        name="moba",
    )(qkv, qkv, qkv, bias)


def _merge_body(d0_ref, d1_ref, d2_ref, om_ref, gate_ref, x_ref, wa_ref, wb_ref, wo_ref, x1_ref, t1_ref, t2_ref):
    D = x_ref.shape[-1]
    gw = DIL_HEADS * HEAD_DIM
    n_slab = t1_ref.shape[0]
    for src, dst in ((d1_ref, t1_ref), (d2_ref, t2_ref)):
        dil, rows = src.shape[1], src.shape[2]
        for r in range(dil):
            for c in range(n_slab):
                dst[c, pl.ds(r, rows, stride=dil), :] = src[0, r, :, c * LANES:(c + 1) * LANES]
    g0 = d0_ref[...]
    g1 = jnp.concatenate([t1_ref[c] for c in range(n_slab)], axis=-1)
    g2 = jnp.concatenate([t2_ref[c] for c in range(n_slab)], axis=-1)
    l0, l1, l2 = g0[:, gw:], g1[:, gw:], g2[:, gw:]
    mx = jnp.maximum(jnp.maximum(l0, l1), l2)
    e0, e1, e2 = jnp.exp(l0 - mx), jnp.exp(l1 - mx), jnp.exp(l2 - mx)
    o_dil = (e0 * g0[:, :gw] + e1 * g1[:, :gw] + e2 * g2[:, :gw]) / (e0 + e1 + e2)
    a = jnp.dot(o_dil.astype(BF16), wa_ref[...], preferred_element_type=F32)
    b = jnp.dot(om_ref[...], wb_ref[...], preferred_element_type=F32)
    merged = gate_ref[:, :D].astype(F32) * a + gate_ref[:, D:].astype(F32) * b
    x1_ref[...] = x_ref[...] + jnp.dot(merged.astype(BF16), wo_ref[...], preferred_element_type=F32)


def _merge(dil_out, o_moba, gates, x2d, w_a, w_b, w_o, seq, tm):
    T, D = x2d.shape
    tm = min(tm, seq)
    per_seq = seq // tm
    assert seq % tm == 0 and dil_out[0].shape[1] == 1

    def rows(width):
        return pl.BlockSpec((tm, width), lambda i: (i, 0))

    def whole(w):
        return pl.BlockSpec(w.shape, lambda i: (0, 0))

    def residue_major(a):
        d, width = a.shape[1], a.shape[3]
        assert tm % (8 * d) == 0
        return pl.BlockSpec((1, d, tm // d, width), lambda i: (i // per_seq, 0, i % per_seq, 0))

    ws = [w_a.astype(BF16), w_b.astype(BF16), w_o.astype(BF16)]
    width = dil_out[0].shape[-1]
    return pl.pallas_call(
        _merge_body,
        grid=(T // tm,),
        in_specs=[rows(width), residue_major(dil_out[1]), residue_major(dil_out[2]),
                  rows(o_moba.shape[-1]), rows(2 * D), rows(D)] + [whole(w) for w in ws],
        out_specs=rows(D),
        out_shape=jax.ShapeDtypeStruct((T, D), F32),
        scratch_shapes=[pltpu.VMEM((width // LANES, tm, LANES), F32)] * 2,
        compiler_params=_params("parallel"),
        name="merge",
    )(dil_out[0].reshape(T, width), dil_out[1], dil_out[2], o_moba, gates, x2d, *ws)


def _cross_body(q_ref, kv_ref, x_ref, wo_ref, x2_ref):
    hd = CROSS_HEAD_DIM
    width = CROSS_HEADS * hd
    heads = []
    for h in range(CROSS_HEADS):
        sl = slice(h * hd, (h + 1) * hd)
        s = _nt_dot(q_ref[0, :, sl], kv_ref[0, :, sl]) * (hd ** -0.5)
        m = jnp.max(s, axis=-1, keepdims=True)
        p = jnp.exp(s - m)
        l = jnp.sum(p, axis=-1, keepdims=True)
        v = kv_ref[0, :, width + h * hd: width + (h + 1) * hd]
        heads.append((jnp.dot(p.astype(BF16), v, preferred_element_type=F32) / l).astype(BF16))
    o = jnp.concatenate(heads, axis=-1)
    x2_ref[0] = x_ref[0] + jnp.dot(o, wo_ref[...], preferred_element_type=F32)


def _cross(q, kv, x, w_o, tm):
    B, S, D = x.shape
    M = kv.shape[1]
    tm = min(tm, S)
    w_o = w_o.astype(BF16)
    return pl.pallas_call(
        _cross_body,
        grid=(B, S // tm),
        in_specs=[pl.BlockSpec((1, tm, q.shape[-1]), lambda b, i: (b, i, 0)),
                  pl.BlockSpec((1, M, kv.shape[-1]), lambda b, i: (b, 0, 0)),
                  pl.BlockSpec((1, tm, D), lambda b, i: (b, i, 0)),
                  pl.BlockSpec(w_o.shape, lambda b, i: (0, 0))],
        out_specs=pl.BlockSpec((1, tm, D), lambda b, i: (b, i, 0)),
        out_shape=jax.ShapeDtypeStruct((B, S, D), F32),
        compiler_params=_params("parallel", "parallel"),
        name="cross",
    )(q, kv, x, w_o)


def _merge_exchange_pairs(n):
    pairs = []
    p = 1
    while p < n:
        k = p
        while k >= 1:
            for j in range(k % p, n - k, 2 * k):
                for i in range(min(k, n - j - k)):
                    if (i + j) // (2 * p) == (i + j + k) // (2 * p):
                        pairs.append((i + j, i + j + k))
            k //= 2
        p *= 2
    return pairs


def _top16(s):
    n, sub = PEER_TOPK, SUBLANES
    assert s.shape[0] % sub == 0 and sub * n >= s.shape[0] >= 2 * sub
    v = [s[k * sub:(k + 1) * sub] for k in range(s.shape[0] // sub)] + [None] * (n - s.shape[0] // sub)

    def exchange(i, j):
        if v[j] is None:
            return
        if v[i] is None:
            v[i], v[j] = v[j], None
        else:
            v[i], v[j] = jnp.maximum(v[i], v[j]), jnp.minimum(v[i], v[j])

    for i, j in _merge_exchange_pairs(n):
        exchange(i, j)
    for shift in (sub // 2, sub // 4, sub // 8):
        w = [None if x is None else pltpu.roll(x, shift, 0) for x in v]
        for k in range(n):
            other = w[n - 1 - k]
            if other is not None:
                v[k] = other if v[k] is None else jnp.maximum(v[k], other)
        d = n // 2
        while d >= 1:
            for i in range(n):
                if i & d == 0:
                    exchange(i, i + d)
            d //= 2
    return v


def _top_sorted(s):
    v = _top16(s)
    nxt = jnp.max(jnp.where(s < v[-1][0:1], s, -jnp.inf), axis=0, keepdims=True)
    return jnp.concatenate([x[0:1] for x in v] + [nxt], axis=0)


def _pair_sum_candidates(a, b):
    assert PEER_TOPK == 16
    row = lax.broadcasted_iota(jnp.int32, (PEER_TOPK, 1), 0)
    blocks = [a[0:1] + b]
    blocks += [a[i:i + 1] + b[0:8] for i in (1, 2, 3)]
    blocks += [jnp.where(row >= 4, b[0:1] + a, -jnp.inf)]
    blocks += [jnp.where(row[0:8] >= 4, b[j:j + 1] + a[0:8], -jnp.inf) for j in (1, 2)]
    return jnp.concatenate(blocks, axis=0)


def _peer_route_body(q_ref, keys_ref, thr_ref, e1_ref, s2_ref, e2_ref):
    nk, K = PEER_N_KEYS, PEER_TOPK
    kd = keys_ref.shape[-1]
    for h in range(PEER_HEADS):
        sc = []
        for p in range(2):
            c0 = (2 * h + p) * kd
            sc.append(_nt_dot(keys_ref[p], q_ref[:, c0:c0 + kd], precision=lax.Precision.HIGHEST))
        a = _top_sorted(sc[0])
        b = _top_sorted(sc[1])
        cand = _pair_sum_candidates(a[:K], b[:K])
        tau = _top16(cand)[K - 1][0:1]
        below = jnp.max(jnp.where(cand < tau, cand, -jnp.inf), axis=0, keepdims=True)
        for corner in (a[0:1] + b[K:K + 1], a[K:K + 1] + b[0:1]):
            below = jnp.maximum(below, jnp.where(corner < tau, corner, -jnp.inf))
        below = jnp.where(below == -jnp.inf, tau, below)
        cut = 0.5 * (tau + below)
        top = a[0:1] + b[0:1]
        z = jnp.sum(jnp.where(cand >= tau, jnp.exp(cand - top), 0.0), axis=0, keepdims=True)
        thr = cut - sc[0]
        e1 = jnp.exp(sc[0] - a[0:1]) * (0.5 / z)
        for c in range(thr_ref.shape[1]):
            thr_ref[h, c] = thr[:, c * LANES:(c + 1) * LANES]
            e1_ref[h, c] = e1[:, c * LANES:(c + 1) * LANES]
        s2_ref[h] = sc[1]
        e2_ref[h] = jnp.exp(sc[1] - b[0:1])


def _peer_route(q, sub_keys, tt):
    T = q.shape[0]
    tt = min(tt, T)
    nk = PEER_N_KEYS
    assert tt % LANES == 0
    big = jax.ShapeDtypeStruct((PEER_HEADS, nk, T), F32)
    big_spec = pl.BlockSpec((PEER_HEADS, nk, tt), lambda i: (0, 0, i))
    slab = jax.ShapeDtypeStruct((PEER_HEADS, T // LANES, nk, LANES), F32)
    slab_spec = pl.BlockSpec((PEER_HEADS, tt // LANES, nk, LANES), lambda i: (0, i, 0, 0))
    return pl.pallas_call(
        _peer_route_body,
        grid=(T // tt,),
        in_specs=[pl.BlockSpec((tt, q.shape[1]), lambda i: (i, 0)),
                  pl.BlockSpec(sub_keys.shape, lambda i: (0, 0, 0))],
        out_specs=[slab_spec, slab_spec, big_spec, big_spec],
        out_shape=[slab, slab, big, big],
        compiler_params=_params("parallel"),
        name="peer_route",
    )(q, sub_keys.astype(F32))


def _peer_main_body(h_ref, u_ref, vt_ref, thr_ref, e1_ref, s2_ref, e2_ref, x_ref, g_ref,
                    o_ref, acc_ref, gate_ref, ga_ref, ht_ref, *, rows_per_step):
    nk = PEER_N_KEYS
    lanes = LANES
    j = pl.program_id(1)
    tt = h_ref.shape[0]

    @pl.when(j == 0)
    def _():
        acc_ref[...] = jnp.zeros_like(acc_ref)
        ht_ref[...] = h_ref[...].astype(F32).T.astype(ht_ref.dtype)

    i1_base = pl.multiple_of(j * rows_per_step, rows_per_step)
    group = 4
    sub = 64
    n_sub = nk // sub

    def gate_tile(ti, carry):
        cs = pl.ds(pl.multiple_of((ti // n_sub) * lanes, lanes), lanes)
        k0 = pl.multiple_of((ti % n_sub) * sub, sub)
        for r0 in range(0, rows_per_step, group):
            gates = [None] * group
            for h in range(PEER_HEADS):
                s2 = s2_ref[h, pl.ds(k0, sub), cs]
                e2 = e2_ref[h, pl.ds(k0, sub), cs]
                for g in range(group):
                    r = r0 + g
                    row = pl.ds(i1_base + r, SUBLANES, stride=0)
                    thr = thr_ref[h, ti // n_sub, row, :]
                    e1 = e1_ref[h, ti // n_sub, row, :]
                    tiles = (sub // SUBLANES, SUBLANES, lanes)
                    w = (jnp.where(s2.reshape(tiles) >= thr, e2.reshape(tiles), 0.0) * e1).reshape(sub, lanes)
                    gates[g] = w if gates[g] is None else gates[g] + w
            for g in range(group):
                rs = pl.ds(pl.multiple_of((r0 + g) * nk + k0, sub), sub)
                gate_ref[rs, cs] = gates[g]
        return carry

    lax.fori_loop(0, (tt // lanes) * n_sub, gate_tile, 0, unroll=4)
    pre = jnp.dot(u_ref[...], ht_ref[...], preferred_element_type=F32)
    ga_ref[...] = (gate_ref[...] * (pre * (1.0 + lax.erf(pre * (2.0 ** -0.5))))).astype(ga_ref.dtype)
    acc_ref[...] += jnp.dot(vt_ref[0], ga_ref[...], preferred_element_type=F32)

    @pl.when(j == pl.num_programs(1) - 1)
    def _():
        o_ref[...] = _rms(x_ref[...] + acc_ref[...].T, g_ref[...])


def _peer_main(hn, u, v, thr, e1, s2, e2, x2d, g_final, tt, rows_per_step):
    T, D = x2d.shape
    tt = min(tt, T)
    nk = PEER_N_KEYS
    eb = rows_per_step * nk
    n_exp = u.shape[0]
    assert n_exp == nk * nk and nk % rows_per_step == 0
    u = u.astype(BF16)
    vt = v.astype(BF16).reshape(n_exp // eb, eb, D).transpose(0, 2, 1)
    score_spec = pl.BlockSpec((PEER_HEADS, nk, tt), lambda i, j: (0, 0, i))
    slab_spec = pl.BlockSpec((PEER_HEADS, tt // LANES, nk, LANES), lambda i, j: (0, i, 0, 0))
    return pl.pallas_call(
        functools.partial(_peer_main_body, rows_per_step=rows_per_step),
        grid=(T // tt, n_exp // eb),
        in_specs=[pl.BlockSpec((tt, D), lambda i, j: (i, 0)),
                  pl.BlockSpec((eb, D), lambda i, j: (j, 0)),
                  pl.BlockSpec((1, D, eb), lambda i, j: (j, 0, 0)),
                  slab_spec, slab_spec, score_spec, score_spec,
                  pl.BlockSpec((tt, D), lambda i, j: (i, 0)),
                  pl.BlockSpec((1, D), lambda i, j: (0, 0))],
        out_specs=pl.BlockSpec((tt, D), lambda i, j: (i, 0)),
        out_shape=jax.ShapeDtypeStruct((T, D), F32),
        scratch_shapes=[pltpu.VMEM((D, tt), F32), pltpu.VMEM((eb, tt), F32), pltpu.VMEM((eb, tt), BF16),
                        pltpu.VMEM((D, tt), BF16)],
        compiler_params=_params("parallel", "arbitrary"),
        name="peer_main",
    )(hn, u, vt, thr, e1, s2, e2, x2d, g_final.reshape(1, D).astype(F32))


def kernel(x, mem, rel_bias, g_mix, w_in, b_gate, w_branch_a, w_branch_b, w_out, g_cross, g_mem, w_q_cross,
           w_kv_cross, w_o_cross, g_ffn, w_q_peer, peer_sub_keys, peer_u, peer_v, g_final):
    B, S, D = x.shape
    T = B * S
    depth = w_in.shape[0]
    ndil = len(DIL_GROUPS) * DIL_HEADS
    x2d = x.reshape(T, D)

    dil_bias = [_bias_lookup(_dil_bucket_map(d), rel_bias[:, gi * DIL_HEADS:(gi + 1) * DIL_HEADS], f"dil_bias_g{gi}")
                for gi, (_, d) in enumerate(DIL_GROUPS)]
    mmap = _moba_bucket_map()
    moba_bias = _bias_lookup(mmap.reshape(-1, MOBA_BLOCK), rel_bias[:, ndil:], "moba_bias")
    moba_bias = moba_bias.reshape(MOBA_HEADS // 2, 2, MOBA_BIAS_TILES, MOBA_BLOCK, MOBA_BLOCK)

    assert depth == 1, "the final norm is fused into the last PEER sweep; one layer supported"
    l = 0
    gc = DIL_GROUP_COLS
    assert DIL_GROUPS[0][1] == 1
    w_tok = jnp.concatenate([w_in[l][:, :gc], w_in[l][:, DIL_WIDTH:QKV_WIDTH]], axis=1)
    tok = _norm_matmul(x2d, g_mix[l], w_tok, out_dtype=BF16, tm=1024, tn=w_tok.shape[1], name="proj_tok")
    dil_qkv = [tok.reshape(B, 1, S, -1)]
    for gi, (_, d) in enumerate(DIL_GROUPS[1:], start=1):
        dil_qkv.append(_norm_matmul(x2d, g_mix[l], w_in[l][:, gi * gc:(gi + 1) * gc], out_dtype=BF16, tm=1024, tn=gc,
                                    dil=d, seq=S, name=f"proj_dil_g{gi}"))
    gates = _norm_matmul(x2d, g_mix[l], w_in[l][:, QKV_WIDTH:], out_dtype=BF16, tm=1024, tn=2 * D,
                         bias=b_gate[l], name="proj_gates")
    dil_out = [_dilated_group(a, dil_bias[gi], f"dilated_g{gi}") for gi, a in enumerate(dil_qkv)]
    o_moba = _moba(tok.reshape(B, S, -1), moba_bias, first_col=gc).reshape(T, MOBA_HEADS * HEAD_DIM)
    x2d = _merge(dil_out, o_moba, gates, x2d, w_branch_a[l], w_branch_b[l], w_out[l], seq=S, tm=512)
    M = mem.shape[1]
    q_c = _norm_matmul(x2d, g_cross[l], w_q_cross[l], out_dtype=BF16, tm=1024, tn=512, name="cross_q")
    kv = _norm_matmul(mem.reshape(B * M, D), g_mem[l], w_kv_cross[l], out_dtype=BF16, tm=1024, tn=1024, name="cross_kv")
    x2d = _cross(q_c.reshape(B, S, -1), kv.reshape(B, M, -1), x2d.reshape(B, S, D), w_o_cross[l], tm=1024).reshape(T, D)
    q_p, hn = _norm_matmul(x2d, g_ffn[l], w_q_peer[l], out_dtype=F32, tm=512, tn=w_q_peer.shape[-1], emit_normed=True,
                           name="peer_q")
    thr, e1, s2, e2 = _peer_route(q_p, peer_sub_keys[l], tt=512)
    y = _peer_main(hn, peer_u[l], peer_v[l], thr, e1, s2, e2, x2d, g_final,
                   tt=512, rows_per_step=8)
    return y.reshape(B, S, D)
```
